```python
import math
import jax
import jax.numpy as jnp
from jax import lax
import numpy as np

D_MODEL = 1024
BATCH = 8
SEQ = 8192
DEPTH = 1

N_NSA_HEADS = 8
NSA_KV_GROUPS = 2
NSA_HEAD_DIM = 64
CMP_LEN = 32
CMP_STRIDE = 16
CMP_HIDDEN = 256
SLC_BLOCK = 64
SLC_TOP_N = 16
WINDOW = 512
N_MLA_HEADS = 8
MLA_NOPE_DIM = 64
MLA_ROPE_DIM = 32
MLA_V_DIM = 64
MLA_Q_LORA = 256
MLA_KV_LORA = 128
ROPE_THETA = 10000.0
MIX_WIDTH = N_NSA_HEADS * NSA_HEAD_DIM + N_MLA_HEADS * MLA_V_DIM
D_FF = -(-8 * D_MODEL // (3 * 256)) * 256
D_PLE = 256
QBLK = 128
ALPHA = (2 * DEPTH) ** 0.25
BETA = (8 * DEPTH) ** -0.25
LN_EPS = 1e-5
RMS_EPS = 1e-6
NEG_INF = -1e30
FORCE_SCORE = 1e6
NSA_KV_W = NSA_KV_GROUPS * NSA_HEAD_DIM
IN_SPLITS = (N_NSA_HEADS * NSA_HEAD_DIM, NSA_KV_W, NSA_KV_W, NSA_KV_W, NSA_KV_W, NSA_KV_W, NSA_KV_W,
             3 * N_NSA_HEADS, MLA_Q_LORA, MLA_KV_LORA, MLA_ROPE_DIM)
D_IN = sum(IN_SPLITS)

kernel_name = 'hybrid_nsa_mla_deepnorm_block'


def _split_cols(h):
    parts, off = [], 0
    for n in IN_SPLITS:
        parts.append(h[..., off:off + n])
        off += n
    return parts


def _layer_norm(x, g, b):
    xf = x.astype(jnp.float32)
    mu = xf.mean(-1, keepdims=True)
    var = jnp.square(xf - mu).mean(-1, keepdims=True)
    return ((xf - mu) * lax.rsqrt(var + LN_EPS) * g + b).astype(x.dtype)


def _rms_norm(x, g):
    xf = x.astype(jnp.float32)
    return (xf * lax.rsqrt(jnp.square(xf).mean(-1, keepdims=True) + RMS_EPS) * g).astype(x.dtype)


def _rope(x, cos, sin):
    half = x.shape[-1] // 2
    x1 = x[..., :half].astype(jnp.float32)
    x2 = x[..., half:].astype(jnp.float32)
    return jnp.concatenate([x1 * cos - x2 * sin, x2 * cos + x1 * sin], axis=-1).astype(x.dtype)


def _masked_softmax(s, mask):
    p = jax.nn.softmax(jnp.where(mask, s, NEG_INF), axis=-1)
    return jnp.where(mask, p, 0.0)


def _alibi_slopes(n):
    return jnp.exp2(-8.0 * jnp.arange(1, n + 1, dtype=jnp.float32) / n)


def _compress(kv, pos, w1, w2):
    B, T, G, D = kv.shape
    n_sub = T // CMP_STRIDE
    r = CMP_LEN // CMP_STRIDE
    nc = n_sub - r + 1
    sub = kv.reshape(B, n_sub, CMP_STRIDE, G, D)
    blk = jnp.concatenate([sub[:, j:j + nc] for j in range(r)], axis=2) + pos[None, None, :, None, :]
    blk = blk.transpose(0, 1, 3, 2, 4).reshape(B, nc, G, CMP_LEN * D)
    return jax.nn.gelu(blk @ w1) @ w2


def _nsa(q, kc, vc, ks, vs, kw, vw, gate_logits, w_ck1, w_ck2, pos_ck, w_cv1, w_cv2, pos_cv):
    B, T = q.shape[:2]
    G, H, D = NSA_KV_GROUPS, N_NSA_HEADS // NSA_KV_GROUPS, NSA_HEAD_DIM
    f32 = jnp.float32
    q = q.reshape(B, T, G, H, D) * (D ** -0.5)
    k_cmp = _compress(kc.reshape(B, T, G, D), pos_ck, w_ck1, w_ck2)
    v_cmp = _compress(vc.reshape(B, T, G, D), pos_cv, w_cv1, w_cv2)
    nc = k_cmp.shape[1]
    r = CMP_LEN // CMP_STRIDE
    cmp_end = jnp.arange(nc) * CMP_STRIDE + (CMP_LEN - 1)
    nb = T // SLC_BLOCK
    ratio = SLC_BLOCK // CMP_STRIDE
    n_top = min(SLC_TOP_N, nb)
    k_blk = ks.reshape(B, nb, SLC_BLOCK, G, D).transpose(0, 3, 1, 2, 4)
    v_blk = vs.reshape(B, nb, SLC_BLOCK, G, D).transpose(0, 3, 1, 2, 4)
    pad = ((0, 0), (WINDOW, 0), (0, 0), (0, 0))
    k_win = jnp.pad(kw.reshape(B, T, G, D), pad)
    v_win = jnp.pad(vw.reshape(B, T, G, D), pad)
    gates = jax.nn.sigmoid(gate_logits.reshape(B, T, G, H, 3).astype(f32)).astype(q.dtype)
    slopes = _alibi_slopes(N_NSA_HEADS).reshape(G, H)
    b_ix = jnp.arange(B)[:, None, None, None]
    g_ix = jnp.arange(G)[None, None, :, None]
    jb = jnp.arange(nb)

    def block(qb):
        q0 = qb * QBLK
        t = q0 + jnp.arange(QBLK)
        qblk = lax.dynamic_slice_in_dim(q, q0, QBLK, axis=1)
        s = jnp.einsum('bqghd,bcgd->bqghc', qblk, k_cmp).astype(f32)
        dist = (t[:, None] - cmp_end[None, :]).astype(f32)
        bias = -slopes[:, :, None] * dist[:, None, None, :]
        mask = (cmp_end[None, :] <= t[:, None])[:, None, None, :]
        p_cmp = _masked_softmax(s + bias, mask)
        o_cmp = jnp.einsum('bqghc,bcgd->bqghd', p_cmp.astype(v_cmp.dtype), v_cmp)
        cpad = ((0, 0),) * 4
        p_sub = sum(jnp.pad(p_cmp, cpad + ((j, r - 1 - j),)) for j in range(r))
        imp = p_sub.sum(3).reshape(B, QBLK, G, nb, ratio).sum(-1)
        cur = t // SLC_BLOCK
        forced = (jb[None, :] == 0) | (jb[None, :] == cur[:, None]) | (jb[None, :] == cur[:, None] - 1)
        future = jb[None, :] > cur[:, None]
        imp = jnp.where(forced[None, :, None, :], FORCE_SCORE, imp)
        imp = jnp.where(future[None, :, None, :], -1.0, imp)
        _, idx = lax.top_k(imp, n_top)
        k_sel = k_blk[b_ix, g_ix, idx]
        v_sel = v_blk[b_ix, g_ix, idx]
        s_pos = idx[..., None] * SLC_BLOCK + jnp.arange(SLC_BLOCK)
        dist = t[None, :, None, None, None] - s_pos
        s = jnp.einsum('bqghd,bqgnsd->bqghns', qblk, k_sel).astype(f32)
        bias = -slopes[None, None, :, :, None, None] * dist[:, :, :, None].astype(f32)
        mask = (dist >= 0)[:, :, :, None].reshape(B, QBLK, G, 1, n_top * SLC_BLOCK)
        p = _masked_softmax((s + bias).reshape(B, QBLK, G, H, n_top * SLC_BLOCK), mask)
        p = p.reshape(B, QBLK, G, H, n_top, SLC_BLOCK).astype(v_sel.dtype)
        o_slc = jnp.einsum('bqghns,bqgnsd->bqghd', p, v_sel)
        kwb = lax.dynamic_slice_in_dim(k_win, q0, QBLK + WINDOW, axis=1)
        vwb = lax.dynamic_slice_in_dim(v_win, q0, QBLK + WINDOW, axis=1)
        w_pos = q0 - WINDOW + jnp.arange(QBLK + WINDOW)
        dist = t[:, None] - w_pos[None, :]
        mask = ((dist >= 0) & (dist < WINDOW) & (w_pos[None, :] >= 0))[:, None, None, :]
        s = jnp.einsum('bqghd,bkgd->bqghk', qblk, kwb).astype(f32)
        bias = -slopes[:, :, None] * dist[:, None, None, :].astype(f32)
        p = _masked_softmax(s + bias, mask)
        o_win = jnp.einsum('bqghk,bkgd->bqghd', p.astype(vwb.dtype), vwb)
        g = lax.dynamic_slice_in_dim(gates, q0, QBLK, axis=1)
        o = g[..., 0:1] * o_cmp + g[..., 1:2] * o_slc + g[..., 2:3] * o_win
        return o.reshape(B, QBLK, N_NSA_HEADS * D)

    out = lax.map(block, jnp.arange(T // QBLK))
    return out.transpose(1, 0, 2, 3).reshape(B, T, N_NSA_HEADS * D)


def _mla(c_q, c_kv, k_pe, g_qn, w_uq, g_kvn, w_ukv, cos, sin):
    B, T = c_q.shape[:2]
    H = N_MLA_HEADS
    f32 = jnp.float32
    q = (_rms_norm(c_q, g_qn) @ w_uq).reshape(B, T, H, MLA_NOPE_DIM + MLA_ROPE_DIM)
    scale = (MLA_NOPE_DIM + MLA_ROPE_DIM) ** -0.5
    q_nope = q[..., :MLA_NOPE_DIM] * scale
    q_pe = _rope(q[..., MLA_NOPE_DIM:], cos[:, None, :], sin[:, None, :]) * scale
    kv = (_rms_norm(c_kv, g_kvn) @ w_ukv).reshape(B, T, H, MLA_NOPE_DIM + MLA_V_DIM)
    k_nope = kv[..., :MLA_NOPE_DIM]
    v = kv[..., MLA_NOPE_DIM:]
    k_pe = _rope(k_pe, cos, sin)
    kpos = jnp.arange(T)

    def block(qb):
        q0 = qb * QBLK
        t = q0 + jnp.arange(QBLK)
        qn = lax.dynamic_slice_in_dim(q_nope, q0, QBLK, axis=1)
        qr = lax.dynamic_slice_in_dim(q_pe, q0, QBLK, axis=1)
        s = (jnp.einsum('bqhd,bkhd->bhqk', qn, k_nope)
             + jnp.einsum('bqhr,bkr->bhqk', qr, k_pe)).astype(f32)
        p = _masked_softmax(s, (kpos[None, :] <= t[:, None])[None, None])
        o = jnp.einsum('bhqk,bkhd->bqhd', p.astype(v.dtype), v)
        return o.reshape(B, QBLK, H * MLA_V_DIM)

    out = lax.map(block, jnp.arange(T // QBLK))
    return out.transpose(1, 0, 2, 3).reshape(B, T, H * MLA_V_DIM)


def setup_inputs(seed: int = 0) -> dict:
    key = jax.random.key(seed)
    ks = jax.random.split(key, 26)

    def nrm(k, shape, scale):
        return jax.random.normal(k, shape, jnp.float32) * scale

    def gain(k, shape):
        return 1.0 + 0.01 * jax.random.normal(k, shape, jnp.float32)

    L, D, DH = DEPTH, D_MODEL, NSA_HEAD_DIM
    return {
        'x': nrm(ks[0], (BATCH, SEQ, D), 1.0),
        'p': nrm(ks[1], (DEPTH, BATCH, SEQ, D_PLE), 1.0),
        'w_in': nrm(ks[2], (L, D, D_IN), D ** -0.5),
        'w_ck1': nrm(ks[3], (L, CMP_LEN * DH, CMP_HIDDEN), (CMP_LEN * DH) ** -0.5),
        'w_ck2': nrm(ks[4], (L, CMP_HIDDEN, DH), CMP_HIDDEN ** -0.5),
        'pos_ck': nrm(ks[5], (L, CMP_LEN, DH), 0.1),
        'w_cv1': nrm(ks[6], (L, CMP_LEN * DH, CMP_HIDDEN), (CMP_LEN * DH) ** -0.5),
        'w_cv2': nrm(ks[7], (L, CMP_HIDDEN, DH), CMP_HIDDEN ** -0.5),
        'pos_cv': nrm(ks[8], (L, CMP_LEN, DH), 0.1),
        'mla_q_norm': gain(ks[9], (L, MLA_Q_LORA)),
        'w_uq': nrm(ks[10], (L, MLA_Q_LORA, N_MLA_HEADS * (MLA_NOPE_DIM + MLA_ROPE_DIM)), MLA_Q_LORA ** -0.5),
        'mla_kv_norm': gain(ks[11], (L, MLA_KV_LORA)),
        'w_ukv': nrm(ks[12], (L, MLA_KV_LORA, N_MLA_HEADS * (MLA_NOPE_DIM + MLA_V_DIM)), MLA_KV_LORA ** -0.5),
        'w_out': nrm(ks[13], (L, MIX_WIDTH, D), BETA * MIX_WIDTH ** -0.5),
        'ln1_g': gain(ks[14], (L, D)),
        'ln1_b': nrm(ks[15], (L, D), 0.01),
        'w_up': nrm(ks[16], (L, D, 2 * D_FF), D ** -0.5),
        'w_down': nrm(ks[17], (L, D_FF, D), BETA * D_FF ** -0.5),
        'ln2_g': gain(ks[18], (L, D)),
        'ln2_b': nrm(ks[19], (L, D), 0.01),
        'w_ple_gate': nrm(ks[20], (L, D, D), D ** -0.5),
        'w_ple': nrm(ks[21], (L, D_PLE, D), BETA * D_PLE ** -0.5),
        'ln3_g': gain(ks[22], (L, D)),
        'ln3_b': nrm(ks[23], (L, D), 0.01),
    }


def reference(x, p, w_in, w_ck1, w_ck2, pos_ck, w_cv1, w_cv2, pos_cv, mla_q_norm, w_uq,
              mla_kv_norm, w_ukv, w_out, ln1_g, ln1_b, w_up, w_down, ln2_g, ln2_b,
              w_ple_gate, w_ple, ln3_g, ln3_b):
    T = x.shape[1]
    pos = jnp.arange(T, dtype=jnp.float32)
    half = MLA_ROPE_DIM // 2
    inv_freq = ROPE_THETA ** (-jnp.arange(half, dtype=jnp.float32) / half)
    ang = pos[:, None] * inv_freq[None, :]
    cos, sin = jnp.cos(ang), jnp.sin(ang)
    for i in range(DEPTH):
        (nq, kc, vc, ks_, vs_, kw, vw, gl, c_q, c_kv, k_pe) = _split_cols(x @ w_in[i])
        o_nsa = _nsa(nq, kc, vc, ks_, vs_, kw, vw, gl,
                     w_ck1[i], w_ck2[i], pos_ck[i], w_cv1[i], w_cv2[i], pos_cv[i])
        o_mla = _mla(c_q, c_kv, k_pe, mla_q_norm[i], w_uq[i], mla_kv_norm[i], w_ukv[i], cos, sin)
        mix = jnp.concatenate([o_nsa, o_mla], axis=-1) @ w_out[i]
        x = _layer_norm(ALPHA * x + mix, ln1_g[i], ln1_b[i])
        gu = x @ w_up[i]
        ffn = (jax.nn.silu(gu[..., :D_FF]) * gu[..., D_FF:]) @ w_down[i]
        x = _layer_norm(ALPHA * x + ffn, ln2_g[i], ln2_b[i])
        gate = jax.nn.sigmoid((x @ w_ple_gate[i]).astype(jnp.float32)).astype(x.dtype)
        ple = gate * (p[i] @ w_ple[i])
        x = _layer_norm(ALPHA * x + ple, ln3_g[i], ln3_b[i])
    return x
```

```python
import functools

import jax
import jax.numpy as jnp
import numpy as np
from jax import lax
from jax.experimental import pallas as pl
from jax.experimental.pallas import tpu as pltpu

F32 = jnp.float32
BF16 = jnp.bfloat16

D_MODEL = 1024
N_NSA_HEADS = 8
NSA_GROUPS = 2
NSA_HPG = N_NSA_HEADS // NSA_GROUPS
NSA_D = 64
CMP_LEN = 32
CMP_STRIDE = 16
CMP_HIDDEN = 256
SLC_BLOCK = 64
SLC_TOP_N = 16
WINDOW = 512
N_MLA_HEADS = 8
MLA_NOPE = 64
MLA_ROPE = 32
MLA_V = 64
MLA_Q_LORA = 256
MLA_KV_LORA = 128
ROPE_THETA = 10000.0
D_FF = -(-8 * D_MODEL // (3 * 256)) * 256
D_PLE = 256
DEPTH = 1
ALPHA = (2 * DEPTH) ** 0.25
LN_EPS = 1e-5
RMS_EPS = 1e-6
NEG_INF = -1e30
FORCE_SCORE = 1e6
REMOVED_SCORE = -3e38

LANES = 128
TQ_NSA = 128
TK_SLC = 512
WIN_KEYS = WINDOW + TQ_NSA
TQ_MLA = 512
TK_MLA = 512
TM_PROJ = 512
TF_FFN = D_FF // 2
VMEM_LIMIT = 56 * 1024 * 1024

_C_NQ = 0
_C_KC = _C_NQ + N_NSA_HEADS * LANES
_C_VC = _C_KC + LANES
_C_KS = _C_VC + LANES
_C_VS = _C_KS + LANES
_C_KW = _C_VS + LANES
_C_VW = _C_KW + LANES
_C_GL = _C_VW + LANES
_C_CQ = _C_GL + LANES
_C_CKV = _C_CQ + MLA_Q_LORA
_C_KPA = _C_CKV + MLA_KV_LORA
_C_KPB = _C_KPA + LANES
_C_END = _C_KPB + LANES


def _nt_dot(a, b):
    return lax.dot_general(a, b, (((1,), (1,)), ((), ())), preferred_element_type=F32)


def _dot(a, b):
    return jnp.dot(a, b, preferred_element_type=F32)


def _layer_norm(v, g, b):
    mu = jnp.mean(v, axis=-1, keepdims=True)
    d = v - mu
    var = jnp.mean(d * d, axis=-1, keepdims=True)
    return d * lax.rsqrt(var + LN_EPS) * g + b


def _rms_norm(v, g):
    return v * lax.rsqrt(jnp.mean(v * v, axis=-1, keepdims=True) + RMS_EPS) * g


def _in_proj_kernel(x_ref, w_ref, wqa_ref, wqb_ref, wka_ref, wv_ref, gq_ref, gkv_ref,
                    cq_ref, sq_ref, ck_ref, sk_ref,
                    qn_ref, kc_ref, vc_ref, ks_ref, vs_ref, kw_ref, vw_ref, gate_ref,
                    qm_ref, km_ref, vm_ref):
    xb = x_ref[...].astype(BF16)

    def seg(c0, width):
        return _dot(xb, w_ref[:, c0:c0 + width])

    for i in range(N_NSA_HEADS):
        qn_ref[0, i] = (seg(_C_NQ + i * LANES, LANES) * (NSA_D ** -0.5)).astype(BF16)
    kc_ref[...] = seg(_C_KC, LANES)
    vc_ref[...] = seg(_C_VC, LANES)
    ks_ref[...] = seg(_C_KS, LANES).astype(BF16)
    vs_ref[...] = seg(_C_VS, LANES).astype(BF16)
    kw_ref[...] = seg(_C_KW, LANES).astype(BF16)
    vw_ref[...] = seg(_C_VW, LANES).astype(BF16)
    gate_ref[...] = jax.nn.sigmoid(seg(_C_GL, LANES))

    cqn = _rms_norm(seg(_C_CQ, MLA_Q_LORA), gq_ref[...]).astype(BF16)
    cq_t = cq_ref[...]
    sq_t = sq_ref[...]
    for h in range(N_MLA_HEADS):
        a = _dot(cqn, wqa_ref[:, h * LANES:(h + 1) * LANES])
        b = _dot(cqn, wqb_ref[:, h * LANES:(h + 1) * LANES])
        qm_ref[:, h * LANES:(h + 1) * LANES] = (a * cq_t + b * sq_t).astype(BF16)

    kvn = _rms_norm(seg(_C_CKV, MLA_KV_LORA), gkv_ref[...]).astype(BF16)
    kpe = seg(_C_KPA, LANES) * ck_ref[...] + seg(_C_KPB, LANES) * sk_ref[...]
    for h in range(N_MLA_HEADS):
        ka = _dot(kvn, wka_ref[:, h * LANES:(h + 1) * LANES])
        km_ref[:, h * LANES:(h + 1) * LANES] = (ka + kpe).astype(BF16)
    vm_ref[...] = _dot(kvn, wv_ref[...]).astype(BF16)


def _in_proj(x2, w_all, wqa, wqb, wka, wv, gq, gkv, cq_t, sq_t, ck_t, sk_t, B, T):
    N = B * T
    tm = TM_PROJ
    tpb = T // tm
    full = lambda a: pl.BlockSpec(a.shape, lambda i: (0,) * a.ndim)
    rows = lambda w: pl.BlockSpec((tm, w), lambda i: (i, 0))
    tab = pl.BlockSpec((tm, LANES), lambda i: (i % tpb, 0))
    out_shape = (
        jax.ShapeDtypeStruct((B, N_NSA_HEADS, T, LANES), BF16),
        jax.ShapeDtypeStruct((N, LANES), F32),
        jax.ShapeDtypeStruct((N, LANES), F32),
        jax.ShapeDtypeStruct((N, LANES), BF16),
        jax.ShapeDtypeStruct((N, LANES), BF16),
        jax.ShapeDtypeStruct((N, LANES), BF16),
        jax.ShapeDtypeStruct((N, LANES), BF16),
        jax.ShapeDtypeStruct((N, LANES), F32),
        jax.ShapeDtypeStruct((N, N_MLA_HEADS * LANES), BF16),
        jax.ShapeDtypeStruct((N, N_MLA_HEADS * LANES), BF16),
        jax.ShapeDtypeStruct((N, N_MLA_HEADS * MLA_V), BF16),
    )
    out_specs = (
        pl.BlockSpec((1, N_NSA_HEADS, tm, LANES), lambda i: (i // tpb, 0, i % tpb, 0)),
        rows(LANES), rows(LANES), rows(LANES), rows(LANES), rows(LANES), rows(LANES), rows(LANES),
        rows(N_MLA_HEADS * LANES), rows(N_MLA_HEADS * LANES), rows(N_MLA_HEADS * MLA_V),
    )
    return pl.pallas_call(
        _in_proj_kernel,
        grid=(N // tm,),
        in_specs=[rows(D_MODEL), full(w_all), full(wqa), full(wqb), full(wka), full(wv),
                  full(gq), full(gkv), tab, tab, tab, tab],
        out_specs=out_specs,
        out_shape=out_shape,
        compiler_params=pltpu.CompilerParams(dimension_semantics=("arbitrary",),
                                             vmem_limit_bytes=VMEM_LIMIT),
        name="in_proj",
    )(x2, w_all, wqa, wqb, wka, wv, gq, gkv, cq_t, sq_t, ck_t, sk_t)


def _compress_kernel(kc_ref, vc_ref, pk_ref, pv_ref, wk_lo_ref, wk_hi_ref, wk2_ref,
                     wv_lo_ref, wv_hi_ref, wv2_ref, ko_ref, vo_ref):
    nsub = kc_ref.shape[1]

    def one(src_ref, pos_ref, lo_ref, hi_ref, w2_ref, dst_ref):
        a = src_ref[0]
        pos = pos_ref[...]
        p0 = _dot((a + pos[0:1]).astype(BF16), lo_ref[...])
        p1 = _dot((a + pos[1:2]).astype(BF16), hi_ref[...])
        hid = p0 + pltpu.roll(p1, nsub - 1, 0)
        out = _dot(jax.nn.gelu(hid).astype(BF16), w2_ref[...])
        row = lax.broadcasted_iota(jnp.int32, out.shape, 0)
        dst_ref[0] = jnp.where(row < nsub - 1, out, 0.0).astype(BF16)

    one(kc_ref, pk_ref, wk_lo_ref, wk_hi_ref, wk2_ref, ko_ref)
    one(vc_ref, pv_ref, wv_lo_ref, wv_hi_ref, wv2_ref, vo_ref)


def _compress(kc3, vc3, pk, pv, wk_lo, wk_hi, wk2, wv_lo, wv_hi, wv2):
    B, nsub, width = kc3.shape
    full = lambda a: pl.BlockSpec(a.shape, lambda b: (0,) * a.ndim)
    per_b = pl.BlockSpec((1, nsub, width), lambda b: (b, 0, 0))
    out_b = pl.BlockSpec((1, nsub, LANES), lambda b: (b, 0, 0))
    return pl.pallas_call(
        _compress_kernel,
        grid=(B,),
        in_specs=[per_b, per_b, full(pk), full(pv), full(wk_lo), full(wk_hi), full(wk2),
                  full(wv_lo), full(wv_hi), full(wv2)],
        out_specs=(out_b, out_b),
        out_shape=(jax.ShapeDtypeStruct((B, nsub, LANES), BF16),) * 2,
        compiler_params=pltpu.CompilerParams(dimension_semantics=("arbitrary",),
                                             vmem_limit_bytes=VMEM_LIMIT),
        name="nsa_compress",
    )(kc3, vc3, pk, pv, wk_lo, wk_hi, wk2, wv_lo, wv_hi, wv2)


def _softmax_rows(s, mask):
    m = jnp.max(s, axis=-1, keepdims=True)
    e = jnp.where(mask, jnp.exp(s - m), 0.0)
    den = jnp.sum(e, axis=-1, keepdims=True)
    inv = jnp.where(den > 0.0, 1.0 / den, 0.0)
    return e * inv


def _nsa_kernel(qn_ref, kcmp_ref, vcmp_ref, ks_ref, vs_ref, kw_ref, vw_ref, ek_ref, mt_ref,
                gate_ref, o_ref, qaug_ref, m_ref, l_ref, acc_ref, *, n_top):
    qb = pl.program_id(1)
    q0 = qb * TQ_NSA
    R = NSA_HPG * TQ_NSA
    ncmp = kcmp_ref.shape[1]

    row = lax.broadcasted_iota(jnp.int32, (R, 1), 0)
    hrow = row >> 7
    tl = row & (TQ_NSA - 1)
    t_col = q0 + tl
    lane = lax.broadcasted_iota(jnp.int32, (R, LANES), 1)
    gates = gate_ref[...]

    group_out = []
    for g in range(NSA_GROUPS):
        slope = jnp.zeros((R, 1), F32)
        for h in range(NSA_HPG):
            slope = jnp.where(hrow == h, 2.0 ** -(g * NSA_HPG + h + 1), slope)
        q = qn_ref[0, g * NSA_HPG:(g + 1) * NSA_HPG].reshape(R, LANES)

        cend = lax.broadcasted_iota(jnp.int32, (1, ncmp), 1) * CMP_STRIDE + (CMP_LEN - 1)
        cmask = cend <= t_col
        s = _nt_dot(q, kcmp_ref[0]) - slope * (t_col - cend).astype(F32)
        p_cmp = _softmax_rows(jnp.where(cmask, s, NEG_INF), cmask)
        o_cmp = _dot(p_cmp.astype(BF16), vcmp_ref[0])

        p4 = p_cmp[0:TQ_NSA]
        for h in range(1, NSA_HPG):
            p4 = p4 + p_cmp[h * TQ_NSA:(h + 1) * TQ_NSA]
        hi = p4.astype(BF16)
        r1 = p4 - hi.astype(F32)
        mid = r1.astype(BF16)
        lo = (r1 - mid.astype(F32)).astype(BF16)
        mt = mt_ref[...]
        imp = _nt_dot(mt, hi) + _nt_dot(mt, mid) + _nt_dot(mt, lo)
        nb = imp.shape[0]
        jb = lax.broadcasted_iota(jnp.int32, (nb, TQ_NSA), 0)
        cur = (q0 + lax.broadcasted_iota(jnp.int32, (nb, TQ_NSA), 1)) >> 6
        forced = (jb == 0) | (jb == cur) | (jb == cur - 1)
        future = jb > cur
        imp = jnp.where(forced, FORCE_SCORE, imp)
        imp = jnp.where(future, -1.0, imp)
        jf = jb.astype(F32)
        sel = jnp.zeros((nb, TQ_NSA), F32)
        for _ in range(n_top):
            mx = jnp.max(imp, axis=0, keepdims=True)
            first = jnp.min(jnp.where(imp == mx, jf, float(nb)), axis=0, keepdims=True)
            hit = jf == first
            sel = jnp.where(hit, 1.0, sel)
            imp = jnp.where(hit, REMOVED_SCORE, imp)
        selneg = jnp.where((sel > 0.0) & jnp.logical_not(future), 0.0, NEG_INF)
        selneg_t = selneg.T.astype(BF16)

        alibi = jnp.where(lane == 0, slope, 0.0)
        alibi = jnp.where(lane == 1, slope, alibi)
        alibi = jnp.where(lane == 2, -slope * (qb * TQ_NSA).astype(F32), alibi)
        alibi = jnp.where(lane == 3, -slope * tl.astype(F32), alibi)
        alibi = alibi.astype(BF16)
        qaug_ref[:, 0:LANES] = q
        for h in range(NSA_HPG):
            qaug_ref[h * TQ_NSA:(h + 1) * TQ_NSA, LANES:2 * LANES] = selneg_t
        qaug_ref[:, 2 * LANES:3 * LANES] = alibi

        m_ref[...] = jnp.full((R, 1), NEG_INF, F32)
        l_ref[...] = jnp.zeros((R, 1), F32)
        acc_ref[...] = jnp.zeros((R, LANES), F32)

        def slc_chunk(c, causal):
            k0 = pl.multiple_of(c * TK_SLC, TK_SLC)
            k_aug = jnp.concatenate([ks_ref[0, pl.ds(k0, TK_SLC), :],
                                     ek_ref[pl.ds(k0, TK_SLC), 0:2 * LANES]], axis=1)
            sc = _nt_dot(qaug_ref[...], k_aug)
            if causal:
                kpos = k0 + lax.broadcasted_iota(jnp.int32, (1, TK_SLC), 1)
                sc = jnp.where(kpos <= t_col, sc, NEG_INF)
            m_old = m_ref[...]
            m_new = jnp.maximum(m_old, jnp.max(sc, axis=-1, keepdims=True))
            a = jnp.exp(m_old - m_new)
            p = jnp.exp(sc - m_new)
            l_ref[...] = a * l_ref[...] + jnp.sum(p, axis=-1, keepdims=True)
            acc_ref[...] = a * acc_ref[...] + _dot(p.astype(BF16), vs_ref[0, pl.ds(k0, TK_SLC), :])
            m_ref[...] = m_new

        n_full = qb // (TK_SLC // TQ_NSA)

        def body(c, carry):
            slc_chunk(c, False)
            return carry

        lax.fori_loop(0, n_full, body, 0)
        slc_chunk(n_full, True)
        o_slc = acc_ref[...] / l_ref[...]

        w0 = pl.multiple_of(q0, TQ_NSA)
        kl = lax.broadcasted_iota(jnp.int32, (1, WIN_KEYS), 1)
        q_win = jnp.concatenate([qaug_ref[:, 0:LANES], qaug_ref[:, 2 * LANES:3 * LANES]], axis=1)
        k_win = jnp.concatenate([kw_ref[0, pl.ds(w0, WIN_KEYS), :],
                                 ek_ref[pl.ds(w0, WIN_KEYS), 2 * LANES:3 * LANES]], axis=1)
        dist = tl + WINDOW - kl
        wmask = (dist >= 0) & (dist < WINDOW) & (kl + q0 >= WINDOW)
        sw = jnp.where(wmask, _nt_dot(q_win, k_win), NEG_INF)
        p_win = _softmax_rows(sw, wmask)
        o_win = _dot(p_win.astype(BF16), vw_ref[0, pl.ds(w0, WIN_KEYS), :])

        def gate_col(branch):
            cols = [gates[:, (g * NSA_HPG + h) * 3 + branch:(g * NSA_HPG + h) * 3 + branch + 1]
                    for h in range(NSA_HPG)]
            return jnp.concatenate(cols, axis=0)

        group_out.append(gate_col(0) * o_cmp + gate_col(1) * o_slc + gate_col(2) * o_win)

    lane_q = lax.broadcasted_iota(jnp.int32, (TQ_NSA, LANES), 1)
    for h in range(NSA_HPG):
        blk = jnp.where(lane_q < NSA_D, group_out[0][h * TQ_NSA:(h + 1) * TQ_NSA],
                        group_out[1][h * TQ_NSA:(h + 1) * TQ_NSA])
        o_ref[:, h * LANES:(h + 1) * LANES] = blk.astype(BF16)


def _nsa(qn, kcmp, vcmp, ks3, vs3, kwp, vwp, ek, mt, gates, B, T):
    nq = T // TQ_NSA
    nb = T // SLC_BLOCK
    n_top = min(SLC_TOP_N, nb)
    R = NSA_HPG * TQ_NSA
    per_b = lambda a: pl.BlockSpec((1,) + a.shape[1:], lambda b, i: (b,) + (0,) * (a.ndim - 1))
    full = lambda a: pl.BlockSpec(a.shape, lambda b, i: (0,) * a.ndim)
    return pl.pallas_call(
        functools.partial(_nsa_kernel, n_top=n_top),
        grid=(B, nq),
        in_specs=[pl.BlockSpec((1, N_NSA_HEADS, TQ_NSA, LANES), lambda b, i: (b, 0, i, 0)),
                  per_b(kcmp), per_b(vcmp), per_b(ks3), per_b(vs3), per_b(kwp), per_b(vwp),
                  full(ek), full(mt),
                  pl.BlockSpec((TQ_NSA, LANES), lambda b, i: (b * nq + i, 0))],
        out_specs=pl.BlockSpec((TQ_NSA, NSA_HPG * LANES), lambda b, i: (b * nq + i, 0)),
        out_shape=jax.ShapeDtypeStruct((B * T, NSA_HPG * LANES), BF16),
        scratch_shapes=[pltpu.VMEM((R, 3 * LANES), BF16),
                        pltpu.VMEM((R, 1), F32), pltpu.VMEM((R, 1), F32),
                        pltpu.VMEM((R, LANES), F32)],
        compiler_params=pltpu.CompilerParams(dimension_semantics=("arbitrary", "arbitrary"),
                                             vmem_limit_bytes=VMEM_LIMIT),
        name="nsa_attention",
    )(qn, kcmp, vcmp, ks3, vs3, kwp, vwp, ek, mt, gates)


def _mla_kernel(q_ref, k_ref, v_ref, o_ref):
    qi = pl.program_id(2)
    q0 = qi * TQ_MLA
    tpos = q0 + lax.broadcasted_iota(jnp.int32, (TQ_MLA, 1), 0)
    qs = [q_ref[:, hh * LANES:(hh + 1) * LANES] for hh in range(2)]

    def chunk(c, carry, causal):
        k0 = pl.multiple_of(c * TK_MLA, TK_MLA)
        v = v_ref[0, pl.ds(k0, TK_MLA), :]
        new = []
        for hh in range(2):
            m_old, l_old, acc = carry[hh]
            s = _nt_dot(qs[hh], k_ref[0, pl.ds(k0, TK_MLA), hh * LANES:(hh + 1) * LANES])
            if causal:
                kpos = k0 + lax.broadcasted_iota(jnp.int32, (1, TK_MLA), 1)
                s = jnp.where(kpos <= tpos, s, NEG_INF)
            m_new = jnp.maximum(m_old, jnp.max(s, axis=-1, keepdims=True))
            a = jnp.exp(m_old - m_new)
            p = jnp.exp(s - m_new)
            l_new = a * l_old + jnp.sum(p, axis=-1, keepdims=True)
            acc = a * acc + _dot(p.astype(BF16), v)
            new.append((m_new, l_new, acc))
        return tuple(new)

    init = tuple((jnp.full((TQ_MLA, 1), NEG_INF, F32), jnp.zeros((TQ_MLA, 1), F32),
                  jnp.zeros((TQ_MLA, LANES), F32)) for _ in range(2))
    carry = lax.fori_loop(0, qi, lambda c, cr: chunk(c, cr, False), init)
    carry = chunk(qi, carry, True)
    lane = lax.broadcasted_iota(jnp.int32, (TQ_MLA, LANES), 1)
    o0 = carry[0][2] / carry[0][1]
    o1 = carry[1][2] / carry[1][1]
    o_ref[...] = jnp.where(lane < MLA_V, o0, o1).astype(BF16)


def _mla(qm, km3, vm3, B, T):
    nq = T // TQ_MLA
    npair = N_MLA_HEADS // 2
    return pl.pallas_call(
        _mla_kernel,
        grid=(B, npair, nq),
        in_specs=[pl.BlockSpec((TQ_MLA, 2 * LANES), lambda b, h, i: (b * nq + i, h)),
                  pl.BlockSpec((1, T, 2 * LANES), lambda b, h, i: (b, 0, h)),
                  pl.BlockSpec((1, T, LANES), lambda b, h, i: (b, 0, h))],
        out_specs=pl.BlockSpec((TQ_MLA, LANES), lambda b, h, i: (b * nq + i, h)),
        out_shape=jax.ShapeDtypeStruct((B * T, npair * LANES), BF16),
        compiler_params=pltpu.CompilerParams(
            dimension_semantics=("arbitrary", "arbitrary", "arbitrary"),
            vmem_limit_bytes=VMEM_LIMIT),
        name="mla_attention",
    )(qm, km3, vm3)


def _out_ln_kernel(on_ref, om_ref, x_ref, wn_ref, wm_ref, g_ref, b_ref, o_ref):
    mix = _dot(on_ref[...], wn_ref[...]) + _dot(om_ref[...], wm_ref[...])
    o_ref[...] = _layer_norm(ALPHA * x_ref[...] + mix, g_ref[...], b_ref[...])


def _out_ln(o_nsa, o_mla, x2, wn, wm, g, b):
    N = x2.shape[0]
    tm = TM_PROJ
    full = lambda a: pl.BlockSpec(a.shape, lambda i: (0,) * a.ndim)
    rows = lambda w: pl.BlockSpec((tm, w), lambda i: (i, 0))
    return pl.pallas_call(
        _out_ln_kernel,
        grid=(N // tm,),
        in_specs=[rows(o_nsa.shape[1]), rows(o_mla.shape[1]), rows(D_MODEL),
                  full(wn), full(wm), full(g), full(b)],
        out_specs=rows(D_MODEL),
        out_shape=jax.ShapeDtypeStruct((N, D_MODEL), F32),
        compiler_params=pltpu.CompilerParams(dimension_semantics=("arbitrary",),
                                             vmem_limit_bytes=VMEM_LIMIT),
        name="out_proj_ln",
    )(o_nsa, o_mla, x2, wn, wm, g, b)


def _ffn_ln_kernel(x_ref, wg_ref, wu_ref, wd_ref, g_ref, b_ref, o_ref, acc_ref):
    j = pl.program_id(1)

    @pl.when(j == 0)
    def _():
        acc_ref[...] = jnp.zeros_like(acc_ref)

    xb = x_ref[...].astype(BF16)
    gate = _dot(xb, wg_ref[...])
    up = _dot(xb, wu_ref[...])
    hid = (jax.nn.silu(gate) * up).astype(BF16)
    acc_ref[...] += _dot(hid, wd_ref[...])

    @pl.when(j == pl.num_programs(1) - 1)
    def _():
        o_ref[...] = _layer_norm(ALPHA * x_ref[...] + acc_ref[...], g_ref[...], b_ref[...])


def _ffn_ln(x1, w_up, w_down, g, b):
    N = x1.shape[0]
    tm = TM_PROJ
    nf = D_FF // TF_FFN
    vec = lambda a: pl.BlockSpec(a.shape, lambda i, j: (0,) * a.ndim)
    return pl.pallas_call(
        _ffn_ln_kernel,
        grid=(N // tm, nf),
        in_specs=[pl.BlockSpec((tm, D_MODEL), lambda i, j: (i, 0)),
                  pl.BlockSpec((D_MODEL, TF_FFN), lambda i, j: (0, j)),
                  pl.BlockSpec((D_MODEL, TF_FFN), lambda i, j: (0, j + nf)),
                  pl.BlockSpec((TF_FFN, D_MODEL), lambda i, j: (j, 0)),
                  vec(g), vec(b)],
        out_specs=pl.BlockSpec((tm, D_MODEL), lambda i, j: (i, 0)),
        out_shape=jax.ShapeDtypeStruct((N, D_MODEL), F32),
        scratch_shapes=[pltpu.VMEM((tm, D_MODEL), F32)],
        compiler_params=pltpu.CompilerParams(dimension_semantics=("arbitrary", "arbitrary"),
                                             vmem_limit_bytes=VMEM_LIMIT),
        name="ffn_ln",
    )(x1, w_up, w_up, w_down, g, b)


def _ple_ln_kernel(x_ref, p_ref, wg_ref, wp_ref, g_ref, b_ref, o_ref):
    xv = x_ref[...]
    gate = jax.nn.sigmoid(_dot(xv.astype(BF16), wg_ref[...]))
    ple = gate * _dot(p_ref[...].astype(BF16), wp_ref[...])
    o_ref[...] = _layer_norm(ALPHA * xv + ple, g_ref[...], b_ref[...])


def _ple_ln(x2, p2, wg, wp, g, b):
    N = x2.shape[0]
    tm = TM_PROJ
    full = lambda a: pl.BlockSpec(a.shape, lambda i: (0,) * a.ndim)
    rows = lambda w: pl.BlockSpec((tm, w), lambda i: (i, 0))
    return pl.pallas_call(
        _ple_ln_kernel,
        grid=(N // tm,),
        in_specs=[rows(D_MODEL), rows(D_PLE), full(wg), full(wp), full(g), full(b)],
        out_specs=rows(D_MODEL),
        out_shape=jax.ShapeDtypeStruct((N, D_MODEL), F32),
        compiler_params=pltpu.CompilerParams(dimension_semantics=("arbitrary",),
                                             vmem_limit_bytes=VMEM_LIMIT),
        name="ple_ln",
    )(x2, p2, wg, wp, g, b)


def _arrange_in_weights(w_in):
    splits = (N_NSA_HEADS * NSA_D,) + (NSA_GROUPS * NSA_D,) * 6 + (3 * N_NSA_HEADS, MLA_Q_LORA,
                                                                  MLA_KV_LORA, MLA_ROPE)
    offs = np.cumsum((0,) + splits)
    part = lambda i: w_in[:, offs[i]:offs[i + 1]]
    nq = part(0).reshape(D_MODEL, NSA_GROUPS, NSA_HPG, NSA_D)
    heads = []
    for g in range(NSA_GROUPS):
        for h in range(NSA_HPG):
            blk = jnp.zeros((D_MODEL, LANES), w_in.dtype)
            heads.append(blk.at[:, g * NSA_D:(g + 1) * NSA_D].set(nq[:, g, h]))
    gl = jnp.pad(part(7), ((0, 0), (0, LANES - 3 * N_NSA_HEADS)))
    kpe = part(10)
    half = MLA_ROPE // 2
    z = jnp.zeros((D_MODEL, LANES), w_in.dtype)
    kpa = z.at[:, MLA_NOPE:MLA_NOPE + MLA_ROPE].set(kpe)
    kpb = z.at[:, MLA_NOPE:MLA_NOPE + half].set(kpe[:, half:])
    kpb = kpb.at[:, MLA_NOPE + half:MLA_NOPE + MLA_ROPE].set(kpe[:, :half])
    cols = heads + [part(1), part(2), part(3), part(4), part(5), part(6), gl, part(8), part(9),
                    kpa, kpb]
    w_all = jnp.concatenate(cols, axis=1)
    assert w_all.shape[1] == _C_END
    return w_all.astype(BF16)


def _arrange_mla_weights(w_uq, w_ukv):
    half = MLA_ROPE // 2
    wq = w_uq.reshape(MLA_Q_LORA, N_MLA_HEADS, MLA_NOPE + MLA_ROPE)
    za = jnp.zeros((MLA_Q_LORA, N_MLA_HEADS, LANES), w_uq.dtype)
    wqa = za.at[:, :, :MLA_NOPE + MLA_ROPE].set(wq)
    wqb = za.at[:, :, MLA_NOPE:MLA_NOPE + half].set(wq[:, :, MLA_NOPE + half:])
    wqb = wqb.at[:, :, MLA_NOPE + half:MLA_NOPE + MLA_ROPE].set(wq[:, :, MLA_NOPE:MLA_NOPE + half])
    wkv = w_ukv.reshape(MLA_KV_LORA, N_MLA_HEADS, MLA_NOPE + MLA_V)
    wka = jnp.zeros((MLA_KV_LORA, N_MLA_HEADS, LANES), w_ukv.dtype).at[:, :, :MLA_NOPE].set(
        wkv[:, :, :MLA_NOPE])
    wv = wkv[:, :, MLA_NOPE:]
    flat = lambda a: a.reshape(a.shape[0], -1).astype(BF16)
    return flat(wqa), flat(wqb), flat(wka), flat(wv)


def _rope_tables(T):
    half = MLA_ROPE // 2
    pos = jnp.arange(T, dtype=F32)
    inv_freq = ROPE_THETA ** (-jnp.arange(half, dtype=F32) / half)
    ang = pos[:, None] * inv_freq[None, :]
    cos, sin = jnp.cos(ang), jnp.sin(ang)
    scale = (MLA_NOPE + MLA_ROPE) ** -0.5
    z = jnp.zeros((T, LANES), F32)
    lo, mid, hi = MLA_NOPE, MLA_NOPE + half, MLA_NOPE + MLA_ROPE
    cq = z.at[:, :lo].set(scale).at[:, lo:mid].set(cos * scale).at[:, mid:hi].set(cos * scale)
    sq = z.at[:, lo:mid].set(-sin * scale).at[:, mid:hi].set(sin * scale)
    ck = z.at[:, lo:mid].set(cos).at[:, mid:hi].set(cos)
    sk = z.at[:, lo:mid].set(-sin).at[:, mid:hi].set(sin)
    return cq, sq, ck, sk


def _compress_weights(w1, w2, pos):
    G, D = NSA_GROUPS, NSA_D
    w1r = w1.reshape(CMP_LEN, D, CMP_HIDDEN)
    halves = []
    for j in range(CMP_LEN // CMP_STRIDE):
        part = w1r[j * CMP_STRIDE:(j + 1) * CMP_STRIDE]
        wide = jnp.zeros((CMP_STRIDE, G, D, G, CMP_HIDDEN), w1.dtype)
        for g in range(G):
            wide = wide.at[:, g, :, g, :].set(part)
        halves.append(wide.reshape(CMP_STRIDE * G * D, G * CMP_HIDDEN).astype(BF16))
    w2bd = jnp.zeros((G, CMP_HIDDEN, G, D), w2.dtype)
    for g in range(G):
        w2bd = w2bd.at[g, :, g, :].set(w2)
    w2bd = w2bd.reshape(G * CMP_HIDDEN, G * D).astype(BF16)
    posr = pos.reshape(CMP_LEN // CMP_STRIDE, CMP_STRIDE, 1, D)
    posw = jnp.broadcast_to(posr, (CMP_LEN // CMP_STRIDE, CMP_STRIDE, G, D)).reshape(
        CMP_LEN // CMP_STRIDE, CMP_STRIDE * G * D)
    return halves[0], halves[1], w2bd, posw


def _selection_tables(T):
    nb = T // SLC_BLOCK
    kpos = np.arange(T)
    onehot = np.zeros((T, LANES), np.float32)
    onehot[kpos, kpos // SLC_BLOCK] = 1.0
    feat = np.zeros((T, LANES), np.float32)
    feat[:, 0] = (kpos // TQ_NSA) * TQ_NSA
    feat[:, 1] = kpos % TQ_NSA
    feat[:, 2] = 1.0
    feat[:, 3] = 1.0
    pad = np.zeros((WINDOW, LANES), np.float32)
    ek = np.concatenate([np.concatenate([onehot, pad]), np.concatenate([feat, pad]),
                         np.concatenate([pad, feat])], axis=1)
    nsub = T // CMP_STRIDE
    ratio = SLC_BLOCK // CMP_STRIDE
    c = np.arange(nsub)
    mt = np.zeros((LANES, nsub), np.float32)
    valid = c < nsub - 1
    for j in range(CMP_LEN // CMP_STRIDE):
        np.add.at(mt, ((c[valid] + j) // ratio, c[valid]), 1.0)
    return jnp.asarray(ek, BF16), jnp.asarray(mt, BF16)


def kernel(x, p, w_in, w_ck1, w_ck2, pos_ck, w_cv1, w_cv2, pos_cv, mla_q_norm, w_uq, mla_kv_norm,
           w_ukv, w_out, ln1_g, ln1_b, w_up, w_down, ln2_g, ln2_b, w_ple_gate, w_ple, ln3_g, ln3_b):
    B, T, _ = x.shape
    N = B * T
    assert T % TQ_MLA == 0 and T % TM_PROJ == 0 and T // SLC_BLOCK <= LANES
    cq_t, sq_t, ck_t, sk_t = _rope_tables(T)
    ek_full, mt = _selection_tables(T)
    row2 = lambda v: v.reshape(1, -1)
    xc = x.reshape(N, D_MODEL)
    for i in range(DEPTH):
        w_all = _arrange_in_weights(w_in[i])
        wqa, wqb, wka, wv = _arrange_mla_weights(w_uq[i], w_ukv[i])
        (qn, kc, vc, ks, vs, kw, vw, gates, qm, km, vm) = _in_proj(
            xc, w_all, wqa, wqb, wka, wv, row2(mla_q_norm[i]), row2(mla_kv_norm[i]),
            cq_t, sq_t, ck_t, sk_t, B, T)
        wk_lo, wk_hi, wk2, pk = _compress_weights(w_ck1[i], w_ck2[i], pos_ck[i])
        wv_lo, wv_hi, wv2, pv = _compress_weights(w_cv1[i], w_cv2[i], pos_cv[i])
        nsub = T // CMP_STRIDE
        kcmp, vcmp = _compress(kc.reshape(B, nsub, CMP_STRIDE * LANES),
                               vc.reshape(B, nsub, CMP_STRIDE * LANES),
                               pk, pv, wk_lo, wk_hi, wk2, wv_lo, wv_hi, wv2)
        pad_w = lambda a: jnp.pad(a.reshape(B, T, LANES), ((0, 0), (WINDOW, 0), (0, 0)))
        o_nsa = _nsa(qn, kcmp, vcmp, ks.reshape(B, T, LANES), vs.reshape(B, T, LANES),
                     pad_w(kw), pad_w(vw), ek_full, mt, gates, B, T)
        o_mla = _mla(qm, km.reshape(B, T, -1), vm.reshape(B, T, -1), B, T)
        wo = w_out[i]
        nsa_w = N_NSA_HEADS * NSA_D
        wn = wo[:nsa_w].reshape(NSA_GROUPS, NSA_HPG, NSA_D, D_MODEL).transpose(1, 0, 2, 3)
        wn = wn.reshape(nsa_w, D_MODEL).astype(BF16)
        wm = wo[nsa_w:].astype(BF16)
        xc = _out_ln(o_nsa, o_mla, xc, wn, wm, row2(ln1_g[i]), row2(ln1_b[i]))
        xc = _ffn_ln(xc, w_up[i].astype(BF16), w_down[i].astype(BF16), row2(ln2_g[i]), row2(ln2_b[i]))
        xc = _ple_ln(xc, p[i].reshape(N, D_PLE), w_ple_gate[i].astype(BF16), w_ple[i].astype(BF16),
                     row2(ln3_g[i]), row2(ln3_b[i]))
    return xc.reshape(B, T, D_MODEL)
```

```python
import functools

import jax
import jax.numpy as jnp
import numpy as np
from jax import lax
from jax.experimental import pallas as pl
from jax.experimental.pallas import tpu as pltpu

F32 = jnp.float32
BF16 = jnp.bfloat16

D_MODEL = 1024
N_NSA_HEADS = 8
NSA_GROUPS = 2
NSA_HPG = N_NSA_HEADS // NSA_GROUPS
NSA_D = 64
CMP_LEN = 32
CMP_STRIDE = 16
CMP_HIDDEN = 256
SLC_BLOCK = 64
SLC_TOP_N = 16
WINDOW = 512
N_MLA_HEADS = 8
MLA_NOPE = 64
MLA_ROPE = 32
MLA_V = 64
MLA_Q_LORA = 256
MLA_KV_LORA = 128
ROPE_THETA = 10000.0
D_FF = -(-8 * D_MODEL // (3 * 256)) * 256
D_PLE = 256
DEPTH = 1
ALPHA = (2 * DEPTH) ** 0.25
LN_EPS = 1e-5
RMS_EPS = 1e-6
NEG_INF = -1e30
FORCE_SCORE = 1e6
REMOVED_SCORE = -3e38
LOG2_E = 1.4426950408889634

LANES = 128
SUBLANES = 8
TQ_NSA = 128
R_NSA = NSA_HPG * TQ_NSA
TK_SLC = 512
BLOCKS_PER_CHUNK = TK_SLC // SLC_BLOCK
N_SLC_CHUNKS = LANES // BLOCKS_PER_CHUNK
SLC_UNROLL = 2
WIN_KEYS = WINDOW + TQ_NSA
N_FEAT = 5
TQ_MLA = 512
TK_MLA = 512
MLA_UNROLL = 4
MLA_DEPTH = 2
TM_PROJ = 512
TF_FFN = D_FF // 2
VMEM_LIMIT = 56 * 1024 * 1024

_C_NQ = 0
_C_KC = _C_NQ + N_NSA_HEADS * LANES
_C_VC = _C_KC + LANES
_C_KS = _C_VC + LANES
_C_KW = _C_KS + NSA_GROUPS * LANES
_C_GL = _C_KW + NSA_GROUPS * LANES
_C_CQ = _C_GL + LANES
_C_CKV = _C_CQ + MLA_Q_LORA
_C_KPA = _C_CKV + MLA_KV_LORA
_C_KPB = _C_KPA + LANES
_C_END = _C_KPB + LANES


def _feat_base(g):
    return NSA_D * (1 - g)


def _nt_dot(a, b):
    return lax.dot_general(a, b, (((1,), (1,)), ((), ())), preferred_element_type=F32)


def _dot(a, b):
    return jnp.dot(a, b, preferred_element_type=F32)


def _layer_norm(v, g, b):
    mu = jnp.mean(v, axis=-1, keepdims=True)
    d = v - mu
    var = jnp.mean(d * d, axis=-1, keepdims=True)
    return d * lax.rsqrt(var + LN_EPS) * g + b


def _rms_norm(v, g):
    return v * lax.rsqrt(jnp.mean(v * v, axis=-1, keepdims=True) + RMS_EPS) * g


def _in_proj_kernel(x_ref, w_ref, wvst_ref, wvwt_ref, wqa_ref, wqb_ref, wka_ref, wv_ref,
                    gq_ref, gkv_ref, ft0_ref, ft1_ref, cq_ref, sq_ref, ck_ref, sk_ref,
                    qn_ref, kc_ref, vc_ref, ks0_ref, ks1_ref, vst_ref, kw0_ref, kw1_ref, vwt_ref,
                    gate_ref, qm_ref, km_ref, vm_ref):
    xb = x_ref[...].astype(BF16)
    h_q = _dot(xb, w_ref[:, _C_NQ:_C_KC])
    h_rest = _dot(xb, w_ref[:, _C_KC:_C_END])

    def seg(c0, width):
        return h_rest[:, c0 - _C_KC:c0 - _C_KC + width]

    for i in range(N_NSA_HEADS):
        qn_ref[0, i] = (h_q[:, i * LANES:(i + 1) * LANES] * (NSA_D ** -0.5)).astype(BF16)
    kc_ref[...] = seg(_C_KC, LANES)
    vc_ref[...] = seg(_C_VC, LANES)
    ks0_ref[...] = (seg(_C_KS, LANES) + ft0_ref[...]).astype(BF16)
    ks1_ref[...] = (seg(_C_KS + LANES, LANES) + ft1_ref[...]).astype(BF16)
    kw0_ref[...] = (seg(_C_KW, LANES) + ft0_ref[...]).astype(BF16)
    kw1_ref[...] = (seg(_C_KW + LANES, LANES) + ft1_ref[...]).astype(BF16)
    def ones_rows(n):
        r = lax.broadcasted_iota(jnp.int32, (n, 1), 0) & (2 * LANES - 1)
        return jnp.where((r == NSA_D) | (r == LANES), 1.0, 0.0)

    vst_ref[0] = (_nt_dot(wvst_ref[...], xb) + ones_rows(NSA_GROUPS * LANES)).astype(BF16)
    vwt_ref[0] = (_nt_dot(wvwt_ref[...], xb) + ones_rows(NSA_GROUPS * LANES)).astype(BF16)
    gate_ref[...] = jax.nn.sigmoid(seg(_C_GL, LANES))

    cqn = _rms_norm(seg(_C_CQ, MLA_Q_LORA), gq_ref[...]).astype(BF16)
    cq_t = cq_ref[...]
    sq_t = sq_ref[...]
    qa = _dot(cqn, wqa_ref[...])
    qb = _dot(cqn, wqb_ref[...])
    for h in range(N_MLA_HEADS):
        hs = slice(h * LANES, (h + 1) * LANES)
        qm_ref[:, hs] = (qa[:, hs] * cq_t + qb[:, hs] * sq_t).astype(BF16)

    kvn = _rms_norm(seg(_C_CKV, MLA_KV_LORA), gkv_ref[...]).astype(BF16)
    kpe = seg(_C_KPA, LANES) * ck_ref[...] + seg(_C_KPB, LANES) * sk_ref[...]
    ka = _dot(kvn, wka_ref[...])
    for h in range(N_MLA_HEADS):
        hs = slice(h * LANES, (h + 1) * LANES)
        km_ref[:, hs] = (ka[:, hs] + kpe).astype(BF16)
    vm_ref[0] = (_nt_dot(wv_ref[...], kvn) + ones_rows(N_MLA_HEADS * LANES)).astype(BF16)


def _in_proj(x2, w_all, wvst, wvwt, wqa, wqb, wka, wv, gq, gkv, ft0, ft1, cq_t, sq_t, ck_t, sk_t,
             B, T):
    N = B * T
    tm = TM_PROJ
    tpb = T // tm
    full = lambda a: pl.BlockSpec(a.shape, lambda i: (0,) * a.ndim)
    rows = lambda w: pl.BlockSpec((tm, w), lambda i: (i, 0))
    tab = pl.BlockSpec((tm, LANES), lambda i: (i % tpb, 0))
    cols_t = lambda w: pl.BlockSpec((1, w, tm), lambda i: (i // tpb, 0, i % tpb))
    tok = lambda dt: jax.ShapeDtypeStruct((N, LANES), dt)
    out_shape = (
        jax.ShapeDtypeStruct((B, N_NSA_HEADS, T, LANES), BF16),
        tok(F32), tok(F32),
        tok(BF16), tok(BF16),
        jax.ShapeDtypeStruct((B, NSA_GROUPS * LANES, T), BF16),
        tok(BF16), tok(BF16),
        jax.ShapeDtypeStruct((B, NSA_GROUPS * LANES, T), BF16),
        tok(F32),
        jax.ShapeDtypeStruct((N, N_MLA_HEADS * LANES), BF16),
        jax.ShapeDtypeStruct((N, N_MLA_HEADS * LANES), BF16),
        jax.ShapeDtypeStruct((B, N_MLA_HEADS * LANES, T), BF16),
    )
    out_specs = (
        pl.BlockSpec((1, N_NSA_HEADS, tm, LANES), lambda i: (i // tpb, 0, i % tpb, 0)),
        rows(LANES), rows(LANES), rows(LANES), rows(LANES), cols_t(NSA_GROUPS * LANES),
        rows(LANES), rows(LANES), cols_t(NSA_GROUPS * LANES), rows(LANES),
        rows(N_MLA_HEADS * LANES), rows(N_MLA_HEADS * LANES), cols_t(N_MLA_HEADS * LANES),
    )
    return pl.pallas_call(
        _in_proj_kernel,
        grid=(N // tm,),
        in_specs=[rows(D_MODEL), full(w_all), full(wvst), full(wvwt), full(wqa), full(wqb),
                  full(wka), full(wv), full(gq), full(gkv), tab, tab, tab, tab, tab, tab],
        out_specs=out_specs,
        out_shape=out_shape,
        compiler_params=pltpu.CompilerParams(dimension_semantics=("arbitrary",),
                                             vmem_limit_bytes=VMEM_LIMIT),
        name="in_proj",
    )(x2, w_all, wvst, wvwt, wqa, wqb, wka, wv, gq, gkv, ft0, ft1, cq_t, sq_t, ck_t, sk_t)


def _compress_kernel(kc_ref, vc_ref, pk_ref, pv_ref, wk_lo_ref, wk_hi_ref, wk2_ref,
                     wv_lo_ref, wv_hi_ref, wv2t_ref, ko_ref, vo_ref):
    nsub = kc_ref.shape[1]

    def hidden(src_ref, pos_ref, lo_ref, hi_ref):
        a = src_ref[0]
        pos = pos_ref[...]
        p0 = _dot((a + pos[0:1]).astype(BF16), lo_ref[...])
        p1 = _dot((a + pos[1:2]).astype(BF16), hi_ref[...])
        hid = p0 + pltpu.roll(p1, nsub - 1, 0)
        return jax.nn.gelu(hid).astype(BF16)

    k_out = _dot(hidden(kc_ref, pk_ref, wk_lo_ref, wk_hi_ref), wk2_ref[...])
    row = lax.broadcasted_iota(jnp.int32, k_out.shape, 0)
    ko_ref[0] = jnp.where(row < nsub - 1, k_out, 0.0).astype(BF16)
    v_out = _nt_dot(wv2t_ref[...], hidden(vc_ref, pv_ref, wv_lo_ref, wv_hi_ref))
    col = lax.broadcasted_iota(jnp.int32, v_out.shape, 1)
    vo_ref[0] = jnp.where(col < nsub - 1, v_out, 0.0).astype(BF16)


def _compress(kc3, vc3, pk, pv, wk_lo, wk_hi, wk2, wv_lo, wv_hi, wv2t):
    B, nsub, width = kc3.shape
    full = lambda a: pl.BlockSpec(a.shape, lambda b: (0,) * a.ndim)
    per_b = pl.BlockSpec((1, nsub, width), lambda b: (b, 0, 0))
    return pl.pallas_call(
        _compress_kernel,
        grid=(B,),
        in_specs=[per_b, per_b, full(pk), full(pv), full(wk_lo), full(wk_hi), full(wk2),
                  full(wv_lo), full(wv_hi), full(wv2t)],
        out_specs=(pl.BlockSpec((1, nsub, LANES), lambda b: (b, 0, 0)),
                   pl.BlockSpec((1, LANES, nsub), lambda b: (b, 0, 0))),
        out_shape=(jax.ShapeDtypeStruct((B, nsub, LANES), BF16),
                   jax.ShapeDtypeStruct((B, LANES, nsub), BF16)),
        compiler_params=pltpu.CompilerParams(dimension_semantics=("arbitrary",),
                                             vmem_limit_bytes=VMEM_LIMIT),
        name="nsa_compress",
    )(kc3, vc3, pk, pv, wk_lo, wk_hi, wk2, wv_lo, wv_hi, wv2t)


def _nsa_kernel(qn_ref, kcmp_ref, vcmpt_ref, cfeat_ref, ks0_ref, ks1_ref, vst_ref,
                kw0_ref, kw1_ref, vwt_ref, onehot_ref, band_ref, mt_ref, qf_ref, gate_ref,
                o_ref, qaug_ref, lst_ref, m_ref, acc_ref, *, n_top):
    qb = pl.program_id(1)
    q0 = qb * TQ_NSA
    R = R_NSA
    ncmp = kcmp_ref.shape[1]

    col = lax.broadcasted_iota(jnp.int32, (1, R), 1)
    t_row = q0 + (col & (TQ_NSA - 1))
    gates_t = gate_ref[...].T
    c_diag = qb // (TK_SLC // TQ_NSA)
    kl_col = lax.broadcasted_iota(jnp.int32, (TK_SLC, 1), 0)
    w0 = pl.multiple_of(q0, TQ_NSA)
    qb_f = qb.astype(F32)

    q_feats, o_wins, o_cmps, imps = [], [], [], []
    for g in range(NSA_GROUPS):
        kw_ref = (kw0_ref, kw1_ref)[g]
        q = qn_ref[0, g * NSA_HPG:(g + 1) * NSA_HPG].reshape(R, LANES)
        q_feat = (q.astype(F32) + qf_ref[g, 0] + qb_f * qf_ref[g, 1]).astype(BF16)
        q_feats.append(q_feat)

        v_rows = slice(g * LANES, (g + 1) * LANES)
        sum_row = NSA_D * (1 - g)

        sw = _nt_dot(kw_ref[0, pl.ds(w0, WIN_KEYS), :], q_feat) + band_ref[...]
        pw = jnp.exp((sw - jnp.max(sw, axis=0, keepdims=True)).astype(BF16))
        o_win = _dot(vwt_ref[0, v_rows, pl.ds(w0, WIN_KEYS)], pw)
        o_wins.append(o_win / o_win[sum_row:sum_row + 1, :])

        kc_aug = jnp.concatenate([kcmp_ref[0], cfeat_ref[...]], axis=1)
        q_cmp = jnp.concatenate([q, (qf_ref[g, 2] + qb_f * qf_ref[g, 3]).astype(BF16)], axis=1)
        cend = lax.broadcasted_iota(jnp.int32, (ncmp, 1), 0) * CMP_STRIDE + (CMP_LEN - 1)
        cmask = cend <= t_row
        sc = jnp.where(cmask, _nt_dot(kc_aug, q_cmp), NEG_INF)
        e = jnp.exp(sc - jnp.maximum(jnp.max(sc, axis=0, keepdims=True), 0.1 * NEG_INF))
        den = jnp.sum(e, axis=0, keepdims=True)
        p_cmp = e * jnp.where(den > 0.0, 1.0 / den, 0.0)
        o_cmps.append(_dot(vcmpt_ref[0], p_cmp.astype(BF16)))

        p4 = p_cmp[:, 0:TQ_NSA]
        for h in range(1, NSA_HPG):
            p4 = p4 + p_cmp[:, h * TQ_NSA:(h + 1) * TQ_NSA]
        hi = p4.astype(BF16)
        r1 = p4 - hi.astype(F32)
        mid = r1.astype(BF16)
        lo = (r1 - mid.astype(F32)).astype(BF16)
        mt = mt_ref[...]
        imps.append(_dot(mt, hi) + _dot(mt, mid) + _dot(mt, lo))

    imp = jnp.concatenate(imps, axis=1)
    width = NSA_GROUPS * TQ_NSA
    jb = lax.broadcasted_iota(jnp.int32, (LANES, width), 0)
    tl2 = lax.broadcasted_iota(jnp.int32, (LANES, width), 1) & (TQ_NSA - 1)
    cur = (q0 + tl2) >> 6
    forced = (jb == 0) | (jb == cur) | (jb == cur - 1)
    future = jb > cur
    jf = jb.astype(F32)
    rest = jnp.where(forced, REMOVED_SCORE, jnp.where(future, -1.0, imp))
    picked = forced
    for _ in range(max(n_top - 3, 0)):
        mx = jnp.max(rest, axis=0, keepdims=True)
        first = jnp.min(jnp.where(rest == mx, jf, float(LANES)), axis=0, keepdims=True)
        hit = jf == first
        picked = picked | hit
        rest = jnp.where(hit, REMOVED_SCORE, rest)
    allowed2 = jnp.where(picked & jnp.logical_not(future), 1.0, 0.0)

    group_out = []
    for g in range(NSA_GROUPS):
        ks_ref = (ks0_ref, ks1_ref)[g]
        v_rows = slice(g * LANES, (g + 1) * LANES)
        sum_row = NSA_D * (1 - g)
        o_cmp, o_win = o_cmps[g], o_wins[g]
        allowed = allowed2[:, g * TQ_NSA:(g + 1) * TQ_NSA]
        selneg_t = jnp.where(allowed > 0.0, 0.0, NEG_INF).T.astype(BF16)

        qaug_ref[:, 0:LANES] = q_feats[g]
        for h in range(NSA_HPG):
            qaug_ref[h * TQ_NSA:(h + 1) * TQ_NSA, LANES:2 * LANES] = selneg_t

        count = jnp.int32(0)
        for c in range(N_SLC_CHUNKS):
            used = jnp.max(allowed[c * BLOCKS_PER_CHUNK:(c + 1) * BLOCKS_PER_CHUNK, :]) > 0.0
            lst_ref[count] = jnp.int32(c)
            count = count + (used & (c < c_diag)).astype(jnp.int32)

        def slc_scores(c, causal, ks_ref=ks_ref):
            k0 = pl.multiple_of(c * TK_SLC, TK_SLC)
            k_aug = jnp.concatenate([ks_ref[0, pl.ds(k0, TK_SLC), :],
                                     onehot_ref[pl.ds(k0, TK_SLC), :]], axis=1)
            st = _nt_dot(k_aug, qaug_ref[...])
            if causal:
                st = jnp.where(k0 + kl_col <= t_row, st, NEG_INF)
            return st, jnp.max(st, axis=0, keepdims=True)

        def absorb(tile, c, v_rows=v_rows):
            st, mx = tile
            k0 = pl.multiple_of(c * TK_SLC, TK_SLC)
            m_old = m_ref[...]
            m_new = jnp.maximum(m_old, mx)
            p = jnp.exp((st - m_new).astype(BF16))
            pv = _dot(vst_ref[0, v_rows, pl.ds(k0, TK_SLC)], p)
            acc_ref[...] = jnp.exp(m_old - m_new) * acc_ref[...] + pv
            m_ref[...] = m_new

        def steps(chunks, causal=False, slc_scores=slc_scores, absorb=absorb):
            tiles = [slc_scores(c, causal) for c in chunks]
            for c, tile in zip(chunks, tiles):
                absorb(tile, c)

        m_ref[...] = jnp.full((1, R), NEG_INF, F32)
        acc_ref[...] = jnp.zeros((LANES, R), F32)
        steps([c_diag], causal=True)

        def body(i, carry, steps=steps):
            steps([lst_ref[SLC_UNROLL * i + j] for j in range(SLC_UNROLL)])
            return carry

        n_loop = count // SLC_UNROLL
        lax.fori_loop(0, n_loop, body, 0)
        for j in range(SLC_UNROLL - 1):
            @pl.when(count - n_loop * SLC_UNROLL > j)
            def _(j=j, steps=steps):
                steps([lst_ref[n_loop * SLC_UNROLL + j]])
        o_slc = acc_ref[...] / acc_ref[sum_row:sum_row + 1, :]

        def gate_row(branch):
            rows = [gates_t[(g * NSA_HPG + h) * 3 + branch:(g * NSA_HPG + h) * 3 + branch + 1, :]
                    for h in range(NSA_HPG)]
            return jnp.concatenate(rows, axis=1)

        group_out.append(gate_row(0) * o_cmp + gate_row(1) * o_slc + gate_row(2) * o_win)

    sub = lax.broadcasted_iota(jnp.int32, (LANES, R), 0)
    out_t = jnp.where(sub < NSA_D, group_out[0], group_out[1])
    for h in range(NSA_HPG):
        o_ref[:, h * LANES:(h + 1) * LANES] = out_t[:, h * TQ_NSA:(h + 1) * TQ_NSA].T.astype(BF16)


def _nsa(qn, kcmp, vcmpt, cfeat, ks0, ks1, vst, kw0, kw1, vwt, onehot, band, mt, qf, gates, B, T):
    nq = T // TQ_NSA
    n_top = min(SLC_TOP_N, T // SLC_BLOCK)
    per_b = lambda a: pl.BlockSpec((1,) + a.shape[1:], lambda b, i: (b,) + (0,) * (a.ndim - 1))
    full = lambda a: pl.BlockSpec(a.shape, lambda b, i: (0,) * a.ndim)
    return pl.pallas_call(
        functools.partial(_nsa_kernel, n_top=n_top),
        grid=(B, nq),
        in_specs=[pl.BlockSpec((1, N_NSA_HEADS, TQ_NSA, LANES), lambda b, i: (b, 0, i, 0)),
                  per_b(kcmp), per_b(vcmpt), full(cfeat), per_b(ks0), per_b(ks1), per_b(vst),
                  per_b(kw0), per_b(kw1), per_b(vwt), full(onehot), full(band), full(mt), full(qf),
                  pl.BlockSpec((TQ_NSA, LANES), lambda b, i: (b * nq + i, 0))],
        out_specs=pl.BlockSpec((TQ_NSA, NSA_HPG * LANES), lambda b, i: (b * nq + i, 0)),
        out_shape=jax.ShapeDtypeStruct((B * T, NSA_HPG * LANES), BF16),
        scratch_shapes=[pltpu.VMEM((R_NSA, 2 * LANES), BF16),
                        pltpu.SMEM((N_SLC_CHUNKS + 1,), jnp.int32),
                        pltpu.VMEM((1, R_NSA), F32), pltpu.VMEM((LANES, R_NSA), F32)],
        compiler_params=pltpu.CompilerParams(dimension_semantics=("arbitrary", "arbitrary"),
                                             vmem_limit_bytes=VMEM_LIMIT),
        name="nsa_attention",
    )(qn, kcmp, vcmpt, cfeat, ks0, ks1, vst, kw0, kw1, vwt, onehot, band, mt, qf, gates)


def _mla_kernel(q_ref, k_ref, vt_ref, o_ref, m_ref, acc_ref):
    qi = pl.program_id(2)
    q0 = qi * TQ_MLA
    tpos = q0 + lax.broadcasted_iota(jnp.int32, (1, TQ_MLA), 1)
    kl = lax.broadcasted_iota(jnp.int32, (TK_MLA, 1), 0)
    qs = [q_ref[:, hh * LANES:(hh + 1) * LANES] for hh in range(2)]

    def qk(hh, c, causal):
        k0 = pl.multiple_of(c * TK_MLA, TK_MLA)
        st = _nt_dot(k_ref[0, pl.ds(k0, TK_MLA), hh * LANES:(hh + 1) * LANES], qs[hh])
        if causal:
            st = jnp.where(k0 + kl <= tpos, st, NEG_INF)
        return st, jnp.max(st, axis=0, keepdims=True)

    def absorb(hh, tile, c):
        st, mx = tile
        k0 = pl.multiple_of(c * TK_MLA, TK_MLA)
        m_old = m_ref[hh]
        m_new = jnp.maximum(m_old, mx)
        p = jnp.exp2((st - m_new).astype(BF16))
        pv = _dot(vt_ref[0, hh * LANES:(hh + 1) * LANES, pl.ds(k0, TK_MLA)], p)
        acc_ref[hh] = jnp.exp2(m_old - m_new) * acc_ref[hh] + pv
        m_ref[hh] = m_new

    def steps(chunks, causal=False):
        work = [(hh, c) for c in chunks for hh in range(2)]
        tiles = [qk(hh, c, causal) for hh, c in work[:MLA_DEPTH]]
        for i, (hh, c) in enumerate(work):
            absorb(hh, tiles[i], c)
            if i + MLA_DEPTH < len(work):
                nh, nc = work[i + MLA_DEPTH]
                tiles.append(qk(nh, nc, causal))

    m_ref[...] = jnp.full(m_ref.shape, NEG_INF, F32)
    acc_ref[...] = jnp.zeros(acc_ref.shape, F32)
    c_diag = qi // (TK_MLA // TQ_MLA)
    steps([c_diag], causal=True)

    def body(i, carry):
        steps([MLA_UNROLL * i + j for j in range(MLA_UNROLL)])
        return carry

    n_loop = c_diag // MLA_UNROLL
    lax.fori_loop(0, n_loop, body, 0)
    for j in range(MLA_UNROLL - 1):
        @pl.when(c_diag - n_loop * MLA_UNROLL > j)
        def _(j=j):
            steps([n_loop * MLA_UNROLL + j])

    o_even = acc_ref[0] / acc_ref[0, MLA_V:MLA_V + 1, :]
    o_odd = acc_ref[1] / acc_ref[1, 0:1, :]
    sub = lax.broadcasted_iota(jnp.int32, (LANES, TQ_MLA), 0)
    o_ref[...] = jnp.where(sub < MLA_V, o_even, o_odd).T.astype(BF16)


def _mla(qm, km3, vmt, B, T):
    nq = T // TQ_MLA
    npair = N_MLA_HEADS // 2
    return pl.pallas_call(
        _mla_kernel,
        grid=(B, npair, nq),
        in_specs=[pl.BlockSpec((TQ_MLA, 2 * LANES), lambda b, h, i: (b * nq + i, h)),
                  pl.BlockSpec((1, T, 2 * LANES), lambda b, h, i: (b, 0, h)),
                  pl.BlockSpec((1, 2 * LANES, T), lambda b, h, i: (b, h, 0))],
        out_specs=pl.BlockSpec((TQ_MLA, LANES), lambda b, h, i: (b * nq + i, h)),
        out_shape=jax.ShapeDtypeStruct((B * T, npair * LANES), BF16),
        scratch_shapes=[pltpu.VMEM((2, 1, TQ_MLA), F32), pltpu.VMEM((2, LANES, TQ_MLA), F32)],
        compiler_params=pltpu.CompilerParams(
            dimension_semantics=("arbitrary", "arbitrary", "arbitrary"),
            vmem_limit_bytes=VMEM_LIMIT,
            ),
        name="mla_attention",
    )(qm, km3, vmt)


def _post_kernel(on_ref, om_ref, x_ref, p_ref, wn_ref, wm_ref, wg_ref, wu_ref, wd_ref,
                 wpg_ref, wp_ref, ln_ref, o_ref, x1_ref, x1b_ref, acc_ref):
    j = pl.program_id(1)

    @pl.when(j == 0)
    def _():
        mix = _dot(on_ref[...], wn_ref[...]) + _dot(om_ref[...], wm_ref[...])
        x1 = _layer_norm(ALPHA * x_ref[...] + mix, ln_ref[0:1, :], ln_ref[1:2, :])
        x1_ref[...] = x1
        x1b_ref[...] = x1.astype(BF16)
        acc_ref[...] = jnp.zeros_like(acc_ref)

    xb = x1b_ref[...]
    hid = (jax.nn.silu(_dot(xb, wg_ref[...])) * _dot(xb, wu_ref[...])).astype(BF16)
    acc_ref[...] += _dot(hid, wd_ref[...])

    @pl.when(j == pl.num_programs(1) - 1)
    def _():
        x2 = _layer_norm(ALPHA * x1_ref[...] + acc_ref[...], ln_ref[2:3, :], ln_ref[3:4, :])
        gate = jax.nn.sigmoid(_dot(x2.astype(BF16), wpg_ref[...]))
        ple = gate * _dot(p_ref[...].astype(BF16), wp_ref[...])
        o_ref[...] = _layer_norm(ALPHA * x2 + ple, ln_ref[4:5, :], ln_ref[5:6, :])


def _post(o_nsa, o_mla, x2, p2, wn, wm, w_up, w_down, wpg, wp, ln):
    N = x2.shape[0]
    tm = TM_PROJ
    nf = D_FF // TF_FFN
    rows = lambda w: pl.BlockSpec((tm, w), lambda i, j: (i, 0))
    full = lambda a: pl.BlockSpec(a.shape, lambda i, j: (0,) * a.ndim)
    return pl.pallas_call(
        _post_kernel,
        grid=(N // tm, nf),
        in_specs=[rows(o_nsa.shape[1]), rows(o_mla.shape[1]), rows(D_MODEL), rows(D_PLE),
                  full(wn), full(wm),
                  pl.BlockSpec((D_MODEL, TF_FFN), lambda i, j: (0, j)),
                  pl.BlockSpec((D_MODEL, TF_FFN), lambda i, j: (0, j + nf)),
                  pl.BlockSpec((TF_FFN, D_MODEL), lambda i, j: (j, 0)),
                  full(wpg), full(wp), full(ln)],
        out_specs=rows(D_MODEL),
        out_shape=jax.ShapeDtypeStruct((N, D_MODEL), F32),
        scratch_shapes=[pltpu.VMEM((tm, D_MODEL), F32), pltpu.VMEM((tm, D_MODEL), BF16),
                        pltpu.VMEM((tm, D_MODEL), F32)],
        compiler_params=pltpu.CompilerParams(dimension_semantics=("arbitrary", "arbitrary"),
                                             vmem_limit_bytes=VMEM_LIMIT),
        name="post_attention",
    )(o_nsa, o_mla, x2, p2, wn, wm, w_up, w_up, w_down, wpg, wp, ln)


def _arrange_in_weights(w_in):
    splits = (N_NSA_HEADS * NSA_D,) + (NSA_GROUPS * NSA_D,) * 6 + (3 * N_NSA_HEADS, MLA_Q_LORA,
                                                                  MLA_KV_LORA, MLA_ROPE)
    offs = np.cumsum((0,) + splits)
    part = lambda i: w_in[:, offs[i]:offs[i + 1]]

    def place(w, lane0):
        return jnp.pad(w, ((0, 0), (lane0, LANES - lane0 - w.shape[1])))

    def group_only(w, g):
        return place(w[:, g * NSA_D:(g + 1) * NSA_D], g * NSA_D)

    nq = part(0).reshape(D_MODEL, NSA_GROUPS, NSA_HPG, NSA_D)
    heads = [place(nq[:, g, h], g * NSA_D) for g in range(NSA_GROUPS) for h in range(NSA_HPG)]
    gl = place(part(7), 0)
    kpe = part(10)
    half = MLA_ROPE // 2
    kpa = place(kpe, MLA_NOPE)
    kpb = place(jnp.concatenate([kpe[:, half:], kpe[:, :half]], axis=1), MLA_NOPE)
    cols = heads + [part(1), part(2), group_only(part(3), 0), group_only(part(3), 1),
                    group_only(part(5), 0), group_only(part(5), 1), gl, part(8), part(9), kpa, kpb]
    w_all = jnp.concatenate(cols, axis=1)
    assert w_all.shape[1] == _C_END

    def value_rows(w):
        return jnp.concatenate([group_only(w, g) for g in range(NSA_GROUPS)], axis=1).T

    return w_all.astype(BF16), value_rows(part(4)).astype(BF16), value_rows(part(6)).astype(BF16)


def _arrange_mla_weights(w_uq, w_ukv):
    half = MLA_ROPE // 2
    wq = w_uq.reshape(MLA_Q_LORA, N_MLA_HEADS, MLA_NOPE + MLA_ROPE)

    def place(w, lane0):
        return jnp.pad(w, ((0, 0), (0, 0), (lane0, LANES - lane0 - w.shape[2])))

    wqa = place(wq, 0)
    wqb = place(jnp.concatenate([wq[:, :, MLA_NOPE + half:], wq[:, :, MLA_NOPE:MLA_NOPE + half]],
                                axis=2), MLA_NOPE)
    wkv = w_ukv.reshape(MLA_KV_LORA, N_MLA_HEADS, MLA_NOPE + MLA_V)
    wka = place(wkv[:, :, :MLA_NOPE], 0)
    wv_pair = wkv[:, :, MLA_NOPE:].reshape(MLA_KV_LORA, N_MLA_HEADS // 2, 2, MLA_V)
    wv = jnp.stack([place(wv_pair[:, :, 0], 0), place(wv_pair[:, :, 1], LANES - MLA_V)], axis=2)
    flat = lambda a: a.reshape(a.shape[0], -1).astype(BF16)
    return flat(wqa), flat(wqb), flat(wka), flat(wv).T


def _rope_tables(T):
    half = MLA_ROPE // 2
    pos = jnp.arange(T, dtype=F32)
    inv_freq = ROPE_THETA ** (-jnp.arange(half, dtype=F32) / half)
    ang = pos[:, None] * inv_freq[None, :]
    cos, sin = jnp.cos(ang), jnp.sin(ang)
    scale = (MLA_NOPE + MLA_ROPE) ** -0.5 * LOG2_E
    ones = jnp.ones((T, MLA_NOPE), F32)
    tail = jnp.zeros((T, LANES - MLA_NOPE - MLA_ROPE), F32)
    cos_row = lambda head: jnp.concatenate([head, cos, cos, tail], axis=1)
    sin_row = jnp.concatenate([0.0 * ones, -sin, sin, tail], axis=1)
    return cos_row(ones) * scale, sin_row * scale, cos_row(0.0 * ones), sin_row


def _compress_weights(w1, w2, pos):
    G, D = NSA_GROUPS, NSA_D
    w1r = w1.reshape(CMP_LEN, D, CMP_HIDDEN)
    eye = jnp.eye(G, dtype=bool)
    halves = []
    for j in range(CMP_LEN // CMP_STRIDE):
        part = w1r[j * CMP_STRIDE:(j + 1) * CMP_STRIDE]
        wide = jnp.where(eye[None, :, None, :, None], part[:, None, :, None, :], 0.0)
        halves.append(wide.reshape(CMP_STRIDE * G * D, G * CMP_HIDDEN).astype(BF16))
    w2bd = jnp.where(eye[:, None, :, None], w2[None, :, None, :], 0.0)
    w2bd = w2bd.reshape(G * CMP_HIDDEN, G * D).astype(BF16)
    posr = pos.reshape(CMP_LEN // CMP_STRIDE, CMP_STRIDE, 1, D)
    posw = jnp.broadcast_to(posr, (CMP_LEN // CMP_STRIDE, CMP_STRIDE, G, D)).reshape(
        CMP_LEN // CMP_STRIDE, CMP_STRIDE * G * D)
    return halves[0], halves[1], w2bd, posw


def _position_features(pos):
    return np.stack([(pos // LANES) * LANES, pos % LANES, np.ones_like(pos), np.ones_like(pos)],
                    axis=1).astype(np.float32)


def _nsa_tables(T):
    kpos = np.arange(T)
    onehot = np.zeros((T, LANES), np.float32)
    onehot[kpos, kpos // SLC_BLOCK] = 1.0
    key_feats, pad_rows = [], []
    for g in range(NSA_GROUPS):
        base = _feat_base(g)
        ft = np.zeros((T, LANES), np.float32)
        ft[:, base:base + 4] = _position_features(kpos)
        key_feats.append(jnp.asarray(ft))
        pad = np.zeros((WINDOW, LANES), np.float32)
        pad[:, base + 4] = 1.0
        pad_rows.append(jnp.asarray(pad, BF16))
    nsub = T // CMP_STRIDE
    c = np.arange(nsub)
    cfeat = np.zeros((nsub, LANES), np.float32)
    cfeat[:, 0:4] = _position_features(c * CMP_STRIDE + CMP_LEN - 1)
    kl = np.arange(WIN_KEYS)[:, None]
    tl = np.arange(TQ_NSA)[None, :]
    band = np.where((kl > tl) & (kl <= tl + WINDOW), 0.0, NEG_INF).astype(np.float32)
    band = np.tile(band, (1, NSA_HPG))
    ratio = SLC_BLOCK // CMP_STRIDE
    mt = np.zeros((LANES, nsub), np.float32)
    valid = c < nsub - 1
    for j in range(CMP_LEN // CMP_STRIDE):
        np.add.at(mt, ((c[valid] + j) // ratio, c[valid]), 1.0)
    qf = np.zeros((NSA_GROUPS, 4, R_NSA, LANES), np.float32)
    tl_rows = np.tile(np.arange(TQ_NSA), NSA_HPG).astype(np.float32)
    for g in range(NSA_GROUPS):
        slope = np.repeat(2.0 ** -(g * NSA_HPG + np.arange(NSA_HPG) + 1.0), TQ_NSA).astype(np.float32)
        for k, base in enumerate((_feat_base(g), 0)):
            qf[g, 2 * k, :, base] = slope
            qf[g, 2 * k, :, base + 1] = slope
            qf[g, 2 * k, :, base + 3] = -slope * tl_rows
            qf[g, 2 * k, :, base + 4] = NEG_INF
            qf[g, 2 * k + 1, :, base + 2] = -slope * TQ_NSA
    return (jnp.asarray(onehot, BF16), key_feats, pad_rows, jnp.asarray(cfeat, BF16),
            jnp.asarray(band), jnp.asarray(mt, BF16), jnp.asarray(qf))


def kernel(x, p, w_in, w_ck1, w_ck2, pos_ck, w_cv1, w_cv2, pos_cv, mla_q_norm, w_uq, mla_kv_norm,
           w_ukv, w_out, ln1_g, ln1_b, w_up, w_down, ln2_g, ln2_b, w_ple_gate, w_ple, ln3_g, ln3_b):
    B, T, _ = x.shape
    N = B * T
    assert T % TQ_MLA == 0 and T % TM_PROJ == 0 and T // SLC_BLOCK <= LANES
    cq_t, sq_t, ck_t, sk_t = _rope_tables(T)
    onehot, key_feats, pad_rows, cfeat, band, mt, qf = _nsa_tables(T)
    row2 = lambda v: v.reshape(1, -1)
    xc = x.reshape(N, D_MODEL)
    for i in range(DEPTH):
        w_all, wvst, wvwt = _arrange_in_weights(w_in[i])
        wqa, wqb, wka, wv = _arrange_mla_weights(w_uq[i], w_ukv[i])
        (qn, kc, vc, ks0, ks1, vst, kw0, kw1, vwt, gates, qm, km, vmt) = _in_proj(
            xc, w_all, wvst, wvwt, wqa, wqb, wka, wv, row2(mla_q_norm[i]), row2(mla_kv_norm[i]),
            key_feats[0], key_feats[1], cq_t, sq_t, ck_t, sk_t, B, T)
        wk_lo, wk_hi, wk2, pk = _compress_weights(w_ck1[i], w_ck2[i], pos_ck[i])
        wv_lo, wv_hi, wv2, pv = _compress_weights(w_cv1[i], w_cv2[i], pos_cv[i])
        nsub = T // CMP_STRIDE
        kcmp, vcmpt = _compress(kc.reshape(B, nsub, CMP_STRIDE * LANES),
                                vc.reshape(B, nsub, CMP_STRIDE * LANES),
                                pk, pv, wk_lo, wk_hi, wk2, wv_lo, wv_hi, wv2.T)
        pad_k = lambda a, g: jnp.concatenate(
            [jnp.broadcast_to(pad_rows[g], (B, WINDOW, LANES)), a.reshape(B, T, LANES)], axis=1)
        o_nsa = _nsa(qn, kcmp, vcmpt, cfeat, ks0.reshape(B, T, LANES), ks1.reshape(B, T, LANES), vst,
                     pad_k(kw0, 0), pad_k(kw1, 1), jnp.pad(vwt, ((0, 0), (0, 0), (WINDOW, 0))),
                     onehot, band, mt, qf, gates, B, T)
        o_mla = _mla(qm, km.reshape(B, T, -1), vmt, B, T)
        wo = w_out[i]
        nsa_w = N_NSA_HEADS * NSA_D
        wn = wo[:nsa_w].reshape(NSA_GROUPS, NSA_HPG, NSA_D, D_MODEL).transpose(1, 0, 2, 3)
        wn = wn.reshape(nsa_w, D_MODEL).astype(BF16)
        wm = wo[nsa_w:].astype(BF16)
        ln = jnp.stack([ln1_g[i], ln1_b[i], ln2_g[i], ln2_b[i], ln3_g[i], ln3_b[i]])
        xc = _post(o_nsa, o_mla, xc, p[i].reshape(N, D_PLE), wn, wm, w_up[i].astype(BF16),
                   w_down[i].astype(BF16), w_ple_gate[i].astype(BF16), w_ple[i].astype(BF16), ln)
    return xc.reshape(B, T, D_MODEL)
```

```python
import functools

import jax
import jax.numpy as jnp
import numpy as np
from jax import lax
from jax.experimental import pallas as pl
from jax.experimental.pallas import tpu as pltpu

F32 = jnp.float32
BF16 = jnp.bfloat16

D_MODEL = 1024
N_NSA_HEADS = 8
NSA_GROUPS = 2
NSA_HPG = N_NSA_HEADS // NSA_GROUPS
NSA_D = 64
CMP_LEN = 32
CMP_STRIDE = 16
CMP_HIDDEN = 256
SLC_BLOCK = 64
SLC_TOP_N = 16
WINDOW = 512
N_MLA_HEADS = 8
MLA_NOPE = 64
MLA_ROPE = 32
MLA_V = 64
MLA_Q_LORA = 256
MLA_KV_LORA = 128
ROPE_THETA = 10000.0
D_FF = -(-8 * D_MODEL // (3 * 256)) * 256
D_PLE = 256
DEPTH = 1
ALPHA = (2 * DEPTH) ** 0.25
LN_EPS = 1e-5
RMS_EPS = 1e-6
NEG_INF = -1e30
FORCE_SCORE = 1e6
REMOVED_SCORE = -3e38
LOG2_E = 1.4426950408889634

LANES = 128
SUBLANES = 8
V_ROWS = 80
TQ_NSA = 128
R_NSA = NSA_HPG * TQ_NSA
TK_SLC = 512
BLOCKS_PER_CHUNK = TK_SLC // SLC_BLOCK
N_SLC_CHUNKS = LANES // BLOCKS_PER_CHUNK
SLC_UNROLL = 2
WIN_KEYS = WINDOW + TQ_NSA
N_FEAT = 5
TQ_MLA = 512
TK_MLA = 512
MLA_UNROLL = 4
MLA_DEPTH = 2
TM_PROJ = 512
TF_FFN = 256
VMEM_LIMIT = 56 * 1024 * 1024

_C_NQ = 0
_C_KC = _C_NQ + N_NSA_HEADS * LANES
_C_VC = _C_KC + LANES
_C_KS = _C_VC + LANES
_C_KW = _C_KS + NSA_GROUPS * LANES
_C_GL = _C_KW + NSA_GROUPS * LANES
_C_CQ = _C_GL + LANES
_C_CKV = _C_CQ + MLA_Q_LORA
_C_KPA = _C_CKV + MLA_KV_LORA
_C_KPB = _C_KPA + LANES
_C_END = _C_KPB + LANES


def _feat_base(g):
    return NSA_D * (1 - g)


def _nt_dot(a, b):
    return lax.dot_general(a, b, (((1,), (1,)), ((), ())), preferred_element_type=F32)


def _dot(a, b):
    return jnp.dot(a, b, preferred_element_type=F32)


def _layer_norm(v, g, b):
    mu = jnp.mean(v, axis=-1, keepdims=True)
    d = v - mu
    var = jnp.mean(d * d, axis=-1, keepdims=True)
    return d * lax.rsqrt(var + LN_EPS) * g + b


def _rms_norm(v, g):
    return v * lax.rsqrt(jnp.mean(v * v, axis=-1, keepdims=True) + RMS_EPS) * g


def _in_proj_kernel(x_ref, w_ref, wvst_ref, wvwt_ref, wqa_ref, wqb_ref, wka_ref, wv_ref,
                    gq_ref, gkv_ref, ft0_ref, ft1_ref, cq_ref, sq_ref, ck_ref, sk_ref,
                    qn_ref, kc_ref, vc_ref, ks0_ref, ks1_ref, vst_ref, kw0_ref, kw1_ref, vwt_ref,
                    gate_ref, qm_ref, km_ref, vm_ref):
    xb = x_ref[...].astype(BF16)
    h_q = _dot(xb, w_ref[:, _C_NQ:_C_KC])
    h_rest = _dot(xb, w_ref[:, _C_KC:_C_END])

    def seg(c0, width):
        return h_rest[:, c0 - _C_KC:c0 - _C_KC + width]

    for i in range(N_NSA_HEADS):
        qn_ref[0, i] = (h_q[:, i * LANES:(i + 1) * LANES] * (NSA_D ** -0.5)).astype(BF16)
    kc_ref[...] = seg(_C_KC, LANES)
    vc_ref[...] = seg(_C_VC, LANES)
    ks0_ref[...] = (seg(_C_KS, LANES) + ft0_ref[...]).astype(BF16)
    ks1_ref[...] = (seg(_C_KS + LANES, LANES) + ft1_ref[...]).astype(BF16)
    kw0_ref[...] = (seg(_C_KW, LANES) + ft0_ref[...]).astype(BF16)
    kw1_ref[...] = (seg(_C_KW + LANES, LANES) + ft1_ref[...]).astype(BF16)
    def ones_rows(n):
        r = lax.broadcasted_iota(jnp.int32, (n, 1), 0) & (LANES - 1)
        return jnp.where(r == NSA_D, 1.0, 0.0)

    vst_ref[0] = (_nt_dot(wvst_ref[...], xb) + ones_rows(NSA_GROUPS * LANES)).astype(BF16)
    vwt_ref[0] = (_nt_dot(wvwt_ref[...], xb) + ones_rows(NSA_GROUPS * LANES)).astype(BF16)
    gate_ref[...] = jax.nn.sigmoid(seg(_C_GL, LANES))

    cqn = _rms_norm(seg(_C_CQ, MLA_Q_LORA), gq_ref[...]).astype(BF16)
    cq_t = cq_ref[...]
    sq_t = sq_ref[...]
    qa = _dot(cqn, wqa_ref[...])
    qb = _dot(cqn, wqb_ref[...])
    for h in range(N_MLA_HEADS):
        hs = slice(h * LANES, (h + 1) * LANES)
        qm_ref[:, hs] = (qa[:, hs] * cq_t + qb[:, hs] * sq_t).astype(BF16)

    kvn = _rms_norm(seg(_C_CKV, MLA_KV_LORA), gkv_ref[...]).astype(BF16)
    kpe = seg(_C_KPA, LANES) * ck_ref[...] + seg(_C_KPB, LANES) * sk_ref[...]
    ka = _dot(kvn, wka_ref[...])
    for h in range(N_MLA_HEADS):
        hs = slice(h * LANES, (h + 1) * LANES)
        km_ref[:, hs] = (ka[:, hs] + kpe).astype(BF16)
    vm_ref[0] = (_nt_dot(wv_ref[...], kvn) + ones_rows(N_MLA_HEADS * LANES)).astype(BF16)


def _in_proj(x2, w_all, wvst, wvwt, wqa, wqb, wka, wv, gq, gkv, ft0, ft1, cq_t, sq_t, ck_t, sk_t,
             B, T):
    N = B * T
    tm = TM_PROJ
    tpb = T // tm
    full = lambda a: pl.BlockSpec(a.shape, lambda i: (0,) * a.ndim)
    rows = lambda w: pl.BlockSpec((tm, w), lambda i: (i, 0))
    tab = pl.BlockSpec((tm, LANES), lambda i: (i % tpb, 0))
    cols_t = lambda w: pl.BlockSpec((1, w, tm), lambda i: (i // tpb, 0, i % tpb))
    tok = lambda dt: jax.ShapeDtypeStruct((N, LANES), dt)
    out_shape = (
        jax.ShapeDtypeStruct((B, N_NSA_HEADS, T, LANES), BF16),
        tok(F32), tok(F32),
        tok(BF16), tok(BF16),
        jax.ShapeDtypeStruct((B, NSA_GROUPS * LANES, T), BF16),
        tok(BF16), tok(BF16),
        jax.ShapeDtypeStruct((B, NSA_GROUPS * LANES, T), BF16),
        tok(F32),
        jax.ShapeDtypeStruct((N, N_MLA_HEADS * LANES), BF16),
        jax.ShapeDtypeStruct((N, N_MLA_HEADS * LANES), BF16),
        jax.ShapeDtypeStruct((B, N_MLA_HEADS * LANES, T), BF16),
    )
    out_specs = (
        pl.BlockSpec((1, N_NSA_HEADS, tm, LANES), lambda i: (i // tpb, 0, i % tpb, 0)),
        rows(LANES), rows(LANES), rows(LANES), rows(LANES), cols_t(NSA_GROUPS * LANES),
        rows(LANES), rows(LANES), cols_t(NSA_GROUPS * LANES), rows(LANES),
        rows(N_MLA_HEADS * LANES), rows(N_MLA_HEADS * LANES), cols_t(N_MLA_HEADS * LANES),
    )
    return pl.pallas_call(
        _in_proj_kernel,
        grid=(N // tm,),
        in_specs=[rows(D_MODEL), full(w_all), full(wvst), full(wvwt), full(wqa), full(wqb),
                  full(wka), full(wv), full(gq), full(gkv), tab, tab, tab, tab, tab, tab],
        out_specs=out_specs,
        out_shape=out_shape,
        compiler_params=pltpu.CompilerParams(dimension_semantics=("arbitrary",),
                                             vmem_limit_bytes=VMEM_LIMIT),
        name="in_proj",
    )(x2, w_all, wvst, wvwt, wqa, wqb, wka, wv, gq, gkv, ft0, ft1, cq_t, sq_t, ck_t, sk_t)


def _compress_kernel(kc_ref, vc_ref, pk_ref, pv_ref, wk_lo_ref, wk_hi_ref, wk2_ref,
                     wv_lo_ref, wv_hi_ref, wv2t_ref, ko_ref, vo_ref):
    nsub = kc_ref.shape[1]

    def hidden(src_ref, pos_ref, lo_ref, hi_ref):
        a = src_ref[0]
        pos = pos_ref[...]
        p0 = _dot((a + pos[0:1]).astype(BF16), lo_ref[...])
        p1 = _dot((a + pos[1:2]).astype(BF16), hi_ref[...])
        hid = p0 + pltpu.roll(p1, nsub - 1, 0)
        return jax.nn.gelu(hid).astype(BF16)

    k_out = _dot(hidden(kc_ref, pk_ref, wk_lo_ref, wk_hi_ref), wk2_ref[...])
    row = lax.broadcasted_iota(jnp.int32, k_out.shape, 0)
    ko_ref[0] = jnp.where(row < nsub - 1, k_out, 0.0).astype(BF16)
    v_out = _nt_dot(wv2t_ref[...], hidden(vc_ref, pv_ref, wv_lo_ref, wv_hi_ref))
    col = lax.broadcasted_iota(jnp.int32, v_out.shape, 1)
    vo_ref[0] = jnp.where(col < nsub - 1, v_out, 0.0).astype(BF16)


def _compress(kc3, vc3, pk, pv, wk_lo, wk_hi, wk2, wv_lo, wv_hi, wv2t):
    B, nsub, width = kc3.shape
    full = lambda a: pl.BlockSpec(a.shape, lambda b: (0,) * a.ndim)
    per_b = pl.BlockSpec((1, nsub, width), lambda b: (b, 0, 0))
    return pl.pallas_call(
        _compress_kernel,
        grid=(B,),
        in_specs=[per_b, per_b, full(pk), full(pv), full(wk_lo), full(wk_hi), full(wk2),
                  full(wv_lo), full(wv_hi), full(wv2t)],
        out_specs=(pl.BlockSpec((1, nsub, LANES), lambda b: (b, 0, 0)),
                   pl.BlockSpec((1, LANES, nsub), lambda b: (b, 0, 0))),
        out_shape=(jax.ShapeDtypeStruct((B, nsub, LANES), BF16),
                   jax.ShapeDtypeStruct((B, LANES, nsub), BF16)),
        compiler_params=pltpu.CompilerParams(dimension_semantics=("arbitrary",),
                                             vmem_limit_bytes=VMEM_LIMIT),
        name="nsa_compress",
    )(kc3, vc3, pk, pv, wk_lo, wk_hi, wk2, wv_lo, wv_hi, wv2t)


def _nsa_kernel(qn_ref, kcmp_ref, vcmpt_ref, cfeat_ref, ks0_ref, ks1_ref, vst_ref,
                kw0_ref, kw1_ref, vwt_ref, onehot_ref, band_ref, mt_ref, qf_ref, gate_ref,
                o_ref, qaug_ref, lst_ref, m_ref, acc_ref, *, n_top):
    qb = pl.program_id(1)
    q0 = qb * TQ_NSA
    R = R_NSA
    ncmp = kcmp_ref.shape[1]

    col = lax.broadcasted_iota(jnp.int32, (1, R), 1)
    t_row = q0 + (col & (TQ_NSA - 1))
    gates_t = gate_ref[...].T
    c_diag = qb // (TK_SLC // TQ_NSA)
    kl_col = lax.broadcasted_iota(jnp.int32, (TK_SLC, 1), 0)
    w0 = pl.multiple_of(q0, TQ_NSA)
    qb_f = qb.astype(F32)

    q_feats, o_wins, o_cmps, imps = [], [], [], []
    for g in range(NSA_GROUPS):
        kw_ref = (kw0_ref, kw1_ref)[g]
        q = qn_ref[0, g * NSA_HPG:(g + 1) * NSA_HPG].reshape(R, LANES)
        q_feat = (q.astype(F32) + qf_ref[g, 0] + qb_f * qf_ref[g, 1]).astype(BF16)
        q_feats.append(q_feat)

        v_rows = slice(g * LANES, g * LANES + V_ROWS)

        sw = _nt_dot(kw_ref[0, pl.ds(w0, WIN_KEYS), :], q_feat) + band_ref[...]
        pw = jnp.exp((sw - jnp.max(sw, axis=0, keepdims=True)).astype(BF16))
        o_win = _dot(vwt_ref[0, v_rows, pl.ds(w0, WIN_KEYS)], pw)
        o_wins.append(o_win[0:NSA_D, :] / o_win[NSA_D:NSA_D + 1, :])

        kc_aug = jnp.concatenate([kcmp_ref[0], cfeat_ref[...]], axis=1)
        q_cmp = jnp.concatenate([q, (qf_ref[g, 2] + qb_f * qf_ref[g, 3]).astype(BF16)], axis=1)
        cend = lax.broadcasted_iota(jnp.int32, (ncmp, 1), 0) * CMP_STRIDE + (CMP_LEN - 1)
        cmask = cend <= t_row
        sc = jnp.where(cmask, _nt_dot(kc_aug, q_cmp), NEG_INF)
        e = jnp.exp(sc - jnp.maximum(jnp.max(sc, axis=0, keepdims=True), 0.1 * NEG_INF))
        den = jnp.sum(e, axis=0, keepdims=True)
        p_cmp = e * jnp.where(den > 0.0, 1.0 / den, 0.0)
        o_cmps.append(_dot(vcmpt_ref[0, g * NSA_D:(g + 1) * NSA_D, :], p_cmp.astype(BF16)))

        p4 = p_cmp[:, 0:TQ_NSA]
        for h in range(1, NSA_HPG):
            p4 = p4 + p_cmp[:, h * TQ_NSA:(h + 1) * TQ_NSA]
        hi = p4.astype(BF16)
        r1 = p4 - hi.astype(F32)
        mid = r1.astype(BF16)
        lo = (r1 - mid.astype(F32)).astype(BF16)
        mt = mt_ref[...]
        imps.append(_dot(mt, hi) + _dot(mt, mid) + _dot(mt, lo))

    imp = jnp.concatenate(imps, axis=1)
    width = NSA_GROUPS * TQ_NSA
    jb = lax.broadcasted_iota(jnp.int32, (LANES, width), 0)
    tl2 = lax.broadcasted_iota(jnp.int32, (LANES, width), 1) & (TQ_NSA - 1)
    cur = (q0 + tl2) >> 6
    forced = (jb == 0) | (jb == cur) | (jb == cur - 1)
    future = jb > cur
    jf = jb.astype(F32)
    rest = jnp.where(forced, REMOVED_SCORE, jnp.where(future, -1.0, imp))
    picked = forced
    for _ in range(max(n_top - 3, 0)):
        mx = jnp.max(rest, axis=0, keepdims=True)
        first = jnp.min(jnp.where(rest == mx, jf, float(LANES)), axis=0, keepdims=True)
        hit = jf == first
        picked = picked | hit
        rest = jnp.where(hit, REMOVED_SCORE, rest)
    allowed2 = jnp.where(picked & jnp.logical_not(future), 1.0, 0.0)

    group_out = []
    for g in range(NSA_GROUPS):
        ks_ref = (ks0_ref, ks1_ref)[g]
        v_rows = slice(g * LANES, g * LANES + V_ROWS)
        o_cmp, o_win = o_cmps[g], o_wins[g]
        allowed = allowed2[:, g * TQ_NSA:(g + 1) * TQ_NSA]
        selneg_t = jnp.where(allowed > 0.0, 0.0, NEG_INF).T.astype(BF16)

        qaug_ref[:, 0:LANES] = q_feats[g]
        for h in range(NSA_HPG):
            qaug_ref[h * TQ_NSA:(h + 1) * TQ_NSA, LANES:2 * LANES] = selneg_t

        count = jnp.int32(0)
        for c in range(N_SLC_CHUNKS):
            used = jnp.max(allowed[c * BLOCKS_PER_CHUNK:(c + 1) * BLOCKS_PER_CHUNK, :]) > 0.0
            lst_ref[count] = jnp.int32(c)
            count = count + (used & (c < c_diag)).astype(jnp.int32)

        def slc_scores(c, causal, ks_ref=ks_ref):
            k0 = pl.multiple_of(c * TK_SLC, TK_SLC)
            k_aug = jnp.concatenate([ks_ref[0, pl.ds(k0, TK_SLC), :],
                                     onehot_ref[pl.ds(k0, TK_SLC), :]], axis=1)
            st = _nt_dot(k_aug, qaug_ref[...])
            if causal:
                st = jnp.where(k0 + kl_col <= t_row, st, NEG_INF)
            return st, jnp.max(st, axis=0, keepdims=True)

        def absorb(tile, c, v_rows=v_rows):
            st, mx = tile
            k0 = pl.multiple_of(c * TK_SLC, TK_SLC)
            m_old = m_ref[...]
            m_new = jnp.maximum(m_old, mx)
            p = jnp.exp((st - m_new).astype(BF16))
            pv = _dot(vst_ref[0, v_rows, pl.ds(k0, TK_SLC)], p)
            acc_ref[...] = jnp.exp(m_old - m_new) * acc_ref[...] + pv
            m_ref[...] = m_new

        def steps(chunks, causal=False, slc_scores=slc_scores, absorb=absorb):
            tiles = [slc_scores(c, causal) for c in chunks]
            for c, tile in zip(chunks, tiles):
                absorb(tile, c)

        m_ref[...] = jnp.full((1, R), NEG_INF, F32)
        acc_ref[...] = jnp.zeros((V_ROWS, R), F32)
        steps([c_diag], causal=True)

        def body(i, carry, steps=steps):
            steps([lst_ref[SLC_UNROLL * i + j] for j in range(SLC_UNROLL)])
            return carry

        n_loop = count // SLC_UNROLL
        lax.fori_loop(0, n_loop, body, 0)
        for j in range(SLC_UNROLL - 1):
            @pl.when(count - n_loop * SLC_UNROLL > j)
            def _(j=j, steps=steps):
                steps([lst_ref[n_loop * SLC_UNROLL + j]])
        o_slc = acc_ref[0:NSA_D, :] / acc_ref[NSA_D:NSA_D + 1, :]

        def gate_row(branch):
            rows = [gates_t[(g * NSA_HPG + h) * 3 + branch:(g * NSA_HPG + h) * 3 + branch + 1, :]
                    for h in range(NSA_HPG)]
            return jnp.concatenate(rows, axis=1)

        group_out.append(gate_row(0) * o_cmp + gate_row(1) * o_slc + gate_row(2) * o_win)

    out_t = jnp.concatenate(group_out, axis=0)
    for h in range(NSA_HPG):
        o_ref[:, h * LANES:(h + 1) * LANES] = out_t[:, h * TQ_NSA:(h + 1) * TQ_NSA].T.astype(BF16)


def _nsa(qn, kcmp, vcmpt, cfeat, ks0, ks1, vst, kw0, kw1, vwt, onehot, band, mt, qf, gates, B, T):
    nq = T // TQ_NSA
    n_top = min(SLC_TOP_N, T // SLC_BLOCK)
    per_b = lambda a: pl.BlockSpec((1,) + a.shape[1:], lambda b, i: (b,) + (0,) * (a.ndim - 1))
    full = lambda a: pl.BlockSpec(a.shape, lambda b, i: (0,) * a.ndim)
    return pl.pallas_call(
        functools.partial(_nsa_kernel, n_top=n_top),
        grid=(B, nq),
        in_specs=[pl.BlockSpec((1, N_NSA_HEADS, TQ_NSA, LANES), lambda b, i: (b, 0, i, 0)),
                  per_b(kcmp), per_b(vcmpt), full(cfeat), per_b(ks0), per_b(ks1), per_b(vst),
                  per_b(kw0), per_b(kw1), per_b(vwt), full(onehot), full(band), full(mt), full(qf),
                  pl.BlockSpec((TQ_NSA, LANES), lambda b, i: (b * nq + i, 0))],
        out_specs=pl.BlockSpec((TQ_NSA, NSA_HPG * LANES), lambda b, i: (b * nq + i, 0)),
        out_shape=jax.ShapeDtypeStruct((B * T, NSA_HPG * LANES), BF16),
        scratch_shapes=[pltpu.VMEM((R_NSA, 2 * LANES), BF16),
                        pltpu.SMEM((N_SLC_CHUNKS + 1,), jnp.int32),
                        pltpu.VMEM((1, R_NSA), F32), pltpu.VMEM((V_ROWS, R_NSA), F32)],
        compiler_params=pltpu.CompilerParams(dimension_semantics=("arbitrary", "arbitrary"),
                                             vmem_limit_bytes=VMEM_LIMIT),
        name="nsa_attention",
    )(qn, kcmp, vcmpt, cfeat, ks0, ks1, vst, kw0, kw1, vwt, onehot, band, mt, qf, gates)


def _mla_kernel(q_ref, k_ref, vt_ref, o_ref, m_ref, acc_ref):
    qi = pl.program_id(2)
    q0 = qi * TQ_MLA
    tpos = q0 + lax.broadcasted_iota(jnp.int32, (1, TQ_MLA), 1)
    kl = lax.broadcasted_iota(jnp.int32, (TK_MLA, 1), 0)
    qs = [q_ref[:, hh * LANES:(hh + 1) * LANES] for hh in range(2)]

    def qk(hh, c, causal):
        k0 = pl.multiple_of(c * TK_MLA, TK_MLA)
        st = _nt_dot(k_ref[0, pl.ds(k0, TK_MLA), hh * LANES:(hh + 1) * LANES], qs[hh])
        if causal:
            st = jnp.where(k0 + kl <= tpos, st, NEG_INF)
        return st, jnp.max(st, axis=0, keepdims=True)

    def absorb(hh, tile, c):
        st, mx = tile
        k0 = pl.multiple_of(c * TK_MLA, TK_MLA)
        m_old = m_ref[hh]
        m_new = jnp.maximum(m_old, mx)
        p = jnp.exp2((st - m_new).astype(BF16))
        pv = _dot(vt_ref[0, hh * LANES:hh * LANES + V_ROWS, pl.ds(k0, TK_MLA)], p)
        acc_ref[hh] = jnp.exp2(m_old - m_new) * acc_ref[hh] + pv
        m_ref[hh] = m_new

    def steps(chunks, causal=False):
        work = [(hh, c) for c in chunks for hh in range(2)]
        tiles = [qk(hh, c, causal) for hh, c in work[:MLA_DEPTH]]
        for i, (hh, c) in enumerate(work):
            absorb(hh, tiles[i], c)
            if i + MLA_DEPTH < len(work):
                nh, nc = work[i + MLA_DEPTH]
                tiles.append(qk(nh, nc, causal))

    m_ref[...] = jnp.full(m_ref.shape, NEG_INF, F32)
    acc_ref[...] = jnp.zeros(acc_ref.shape, F32)
    c_diag = qi // (TK_MLA // TQ_MLA)
    steps([c_diag], causal=True)

    def body(i, carry):
        steps([MLA_UNROLL * i + j for j in range(MLA_UNROLL)])
        return carry

    n_loop = c_diag // MLA_UNROLL
    lax.fori_loop(0, n_loop, body, 0)
    for j in range(MLA_UNROLL - 1):
        @pl.when(c_diag - n_loop * MLA_UNROLL > j)
        def _(j=j):
            steps([n_loop * MLA_UNROLL + j])

    outs = [acc_ref[hh, 0:MLA_V, :] / acc_ref[hh, MLA_V:MLA_V + 1, :] for hh in range(2)]
    o_ref[...] = jnp.concatenate(outs, axis=0).T.astype(BF16)


def _mla(qm, km3, vmt, B, T):
    nq = T // TQ_MLA
    npair = N_MLA_HEADS // 2
    return pl.pallas_call(
        _mla_kernel,
        grid=(B, npair, nq),
        in_specs=[pl.BlockSpec((TQ_MLA, 2 * LANES), lambda b, h, i: (b * nq + i, h)),
                  pl.BlockSpec((1, T, 2 * LANES), lambda b, h, i: (b, 0, h)),
                  pl.BlockSpec((1, 2 * LANES, T), lambda b, h, i: (b, h, 0))],
        out_specs=pl.BlockSpec((TQ_MLA, LANES), lambda b, h, i: (b * nq + i, h)),
        out_shape=jax.ShapeDtypeStruct((B * T, npair * LANES), BF16),
        scratch_shapes=[pltpu.VMEM((2, 1, TQ_MLA), F32), pltpu.VMEM((2, V_ROWS, TQ_MLA), F32)],
        compiler_params=pltpu.CompilerParams(
            dimension_semantics=("arbitrary", "arbitrary", "arbitrary"),
            vmem_limit_bytes=VMEM_LIMIT,
            ),
        name="mla_attention",
    )(qm, km3, vmt)


def _post_kernel(on_ref, om_ref, x_ref, p_ref, wn_ref, wm_ref, wup_ref, wd_ref,
                 wpg_ref, wp_ref, ln_ref, o_ref):
    mix = _dot(on_ref[...], wn_ref[...]) + _dot(om_ref[...], wm_ref[...])
    x1 = _layer_norm(ALPHA * x_ref[...] + mix, ln_ref[0:1, :], ln_ref[1:2, :])
    xb = x1.astype(BF16)
    ffn = jnp.zeros_like(x1)
    for c in range(D_FF // TF_FFN):
        cols = slice(c * TF_FFN, (c + 1) * TF_FFN)
        up_cols = slice(D_FF + c * TF_FFN, D_FF + (c + 1) * TF_FFN)
        hid = jax.nn.silu(_dot(xb, wup_ref[:, cols])) * _dot(xb, wup_ref[:, up_cols])
        ffn = ffn + _dot(hid.astype(BF16), wd_ref[cols, :])
    x2 = _layer_norm(ALPHA * x1 + ffn, ln_ref[2:3, :], ln_ref[3:4, :])
    gate = jax.nn.sigmoid(_dot(x2.astype(BF16), wpg_ref[...]))
    ple = gate * _dot(p_ref[...].astype(BF16), wp_ref[...])
    o_ref[...] = _layer_norm(ALPHA * x2 + ple, ln_ref[4:5, :], ln_ref[5:6, :])


def _post(o_nsa, o_mla, x2, p2, wn, wm, w_up, w_down, wpg, wp, ln):
    N = x2.shape[0]
    tm = TM_PROJ
    rows = lambda w: pl.BlockSpec((tm, w), lambda i: (i, 0))
    full = lambda a: pl.BlockSpec(a.shape, lambda i: (0,) * a.ndim)
    return pl.pallas_call(
        _post_kernel,
        grid=(N // tm,),
        in_specs=[rows(o_nsa.shape[1]), rows(o_mla.shape[1]), rows(D_MODEL), rows(D_PLE),
                  full(wn), full(wm), full(w_up), full(w_down), full(wpg), full(wp), full(ln)],
        out_specs=rows(D_MODEL),
        out_shape=jax.ShapeDtypeStruct((N, D_MODEL), F32),
        compiler_params=pltpu.CompilerParams(dimension_semantics=("arbitrary",),
                                             vmem_limit_bytes=VMEM_LIMIT),
        name="post_attention",
    )(o_nsa, o_mla, x2, p2, wn, wm, w_up, w_down, wpg, wp, ln)


def _arrange_in_weights(w_in):
    splits = (N_NSA_HEADS * NSA_D,) + (NSA_GROUPS * NSA_D,) * 6 + (3 * N_NSA_HEADS, MLA_Q_LORA,
                                                                  MLA_KV_LORA, MLA_ROPE)
    offs = np.cumsum((0,) + splits)
    part = lambda i: w_in[:, offs[i]:offs[i + 1]]

    def place(w, lane0):
        return jnp.pad(w, ((0, 0), (lane0, LANES - lane0 - w.shape[1])))

    def group_only(w, g):
        return place(w[:, g * NSA_D:(g + 1) * NSA_D], g * NSA_D)

    nq = part(0).reshape(D_MODEL, NSA_GROUPS, NSA_HPG, NSA_D)
    heads = [place(nq[:, g, h], g * NSA_D) for g in range(NSA_GROUPS) for h in range(NSA_HPG)]
    gl = place(part(7), 0)
    kpe = part(10)
    half = MLA_ROPE // 2
    kpa = place(kpe, MLA_NOPE)
    kpb = place(jnp.concatenate([kpe[:, half:], kpe[:, :half]], axis=1), MLA_NOPE)
    cols = heads + [part(1), part(2), group_only(part(3), 0), group_only(part(3), 1),
                    group_only(part(5), 0), group_only(part(5), 1), gl, part(8), part(9), kpa, kpb]
    w_all = jnp.concatenate(cols, axis=1)
    assert w_all.shape[1] == _C_END

    def value_rows(w):
        return jnp.concatenate([place(w[:, g * NSA_D:(g + 1) * NSA_D], 0)
                                for g in range(NSA_GROUPS)], axis=1).T

    return w_all.astype(BF16), value_rows(part(4)).astype(BF16), value_rows(part(6)).astype(BF16)


def _arrange_mla_weights(w_uq, w_ukv):
    half = MLA_ROPE // 2
    wq = w_uq.reshape(MLA_Q_LORA, N_MLA_HEADS, MLA_NOPE + MLA_ROPE)

    def place(w, lane0):
        return jnp.pad(w, ((0, 0), (0, 0), (lane0, LANES - lane0 - w.shape[2])))

    wqa = place(wq, 0)
    wqb = place(jnp.concatenate([wq[:, :, MLA_NOPE + half:], wq[:, :, MLA_NOPE:MLA_NOPE + half]],
                                axis=2), MLA_NOPE)
    wkv = w_ukv.reshape(MLA_KV_LORA, N_MLA_HEADS, MLA_NOPE + MLA_V)
    wka = place(wkv[:, :, :MLA_NOPE], 0)
    wv = place(wkv[:, :, MLA_NOPE:], 0)
    flat = lambda a: a.reshape(a.shape[0], -1).astype(BF16)
    return flat(wqa), flat(wqb), flat(wka), flat(wv).T


def _rope_tables(T):
    half = MLA_ROPE // 2
    pos = jnp.arange(T, dtype=F32)
    inv_freq = ROPE_THETA ** (-jnp.arange(half, dtype=F32) / half)
    ang = pos[:, None] * inv_freq[None, :]
    cos, sin = jnp.cos(ang), jnp.sin(ang)
    scale = (MLA_NOPE + MLA_ROPE) ** -0.5 * LOG2_E
    ones = jnp.ones((T, MLA_NOPE), F32)
    tail = jnp.zeros((T, LANES - MLA_NOPE - MLA_ROPE), F32)
    cos_row = lambda head: jnp.concatenate([head, cos, cos, tail], axis=1)
    sin_row = jnp.concatenate([0.0 * ones, -sin, sin, tail], axis=1)
    return cos_row(ones) * scale, sin_row * scale, cos_row(0.0 * ones), sin_row


def _compress_weights(w1, w2, pos):
    G, D = NSA_GROUPS, NSA_D
    w1r = w1.reshape(CMP_LEN, D, CMP_HIDDEN)
    eye = jnp.eye(G, dtype=bool)
    halves = []
    for j in range(CMP_LEN // CMP_STRIDE):
        part = w1r[j * CMP_STRIDE:(j + 1) * CMP_STRIDE]
        wide = jnp.where(eye[None, :, None, :, None], part[:, None, :, None, :], 0.0)
        halves.append(wide.reshape(CMP_STRIDE * G * D, G * CMP_HIDDEN).astype(BF16))
    w2bd = jnp.where(eye[:, None, :, None], w2[None, :, None, :], 0.0)
    w2bd = w2bd.reshape(G * CMP_HIDDEN, G * D).astype(BF16)
    posr = pos.reshape(CMP_LEN // CMP_STRIDE, CMP_STRIDE, 1, D)
    posw = jnp.broadcast_to(posr, (CMP_LEN // CMP_STRIDE, CMP_STRIDE, G, D)).reshape(
        CMP_LEN // CMP_STRIDE, CMP_STRIDE * G * D)
    return halves[0], halves[1], w2bd, posw


def _position_features(pos):
    return np.stack([(pos // LANES) * LANES, pos % LANES, np.ones_like(pos), np.ones_like(pos)],
                    axis=1).astype(np.float32)


def _nsa_tables(T):
    kpos = np.arange(T)
    onehot = np.zeros((T, LANES), np.float32)
    onehot[kpos, kpos // SLC_BLOCK] = 1.0
    key_feats, pad_rows = [], []
    for g in range(NSA_GROUPS):
        base = _feat_base(g)
        ft = np.zeros((T, LANES), np.float32)
        ft[:, base:base + 4] = _position_features(kpos)
        key_feats.append(jnp.asarray(ft))
        pad = np.zeros((WINDOW, LANES), np.float32)
        pad[:, base + 4] = 1.0
        pad_rows.append(jnp.asarray(pad, BF16))
    nsub = T // CMP_STRIDE
    c = np.arange(nsub)
    cfeat = np.zeros((nsub, LANES), np.float32)
    cfeat[:, 0:4] = _position_features(c * CMP_STRIDE + CMP_LEN - 1)
    kl = np.arange(WIN_KEYS)[:, None]
    tl = np.arange(TQ_NSA)[None, :]
    band = np.where((kl > tl) & (kl <= tl + WINDOW), 0.0, NEG_INF).astype(np.float32)
    band = np.tile(band, (1, NSA_HPG))
    ratio = SLC_BLOCK // CMP_STRIDE
    mt = np.zeros((LANES, nsub), np.float32)
    valid = c < nsub - 1
    for j in range(CMP_LEN // CMP_STRIDE):
        np.add.at(mt, ((c[valid] + j) // ratio, c[valid]), 1.0)
    qf = np.zeros((NSA_GROUPS, 4, R_NSA, LANES), np.float32)
    tl_rows = np.tile(np.arange(TQ_NSA), NSA_HPG).astype(np.float32)
    for g in range(NSA_GROUPS):
        slope = np.repeat(2.0 ** -(g * NSA_HPG + np.arange(NSA_HPG) + 1.0), TQ_NSA).astype(np.float32)
        for k, base in enumerate((_feat_base(g), 0)):
            qf[g, 2 * k, :, base] = slope
            qf[g, 2 * k, :, base + 1] = slope
            qf[g, 2 * k, :, base + 3] = -slope * tl_rows
            qf[g, 2 * k, :, base + 4] = NEG_INF
            qf[g, 2 * k + 1, :, base + 2] = -slope * TQ_NSA
    return (jnp.asarray(onehot, BF16), key_feats, pad_rows, jnp.asarray(cfeat, BF16),
            jnp.asarray(band), jnp.asarray(mt, BF16), jnp.asarray(qf))


def kernel(x, p, w_in, w_ck1, w_ck2, pos_ck, w_cv1, w_cv2, pos_cv, mla_q_norm, w_uq, mla_kv_norm,
           w_ukv, w_out, ln1_g, ln1_b, w_up, w_down, ln2_g, ln2_b, w_ple_gate, w_ple, ln3_g, ln3_b):
    B, T, _ = x.shape
    N = B * T
    assert T % TQ_MLA == 0 and T % TM_PROJ == 0 and T // SLC_BLOCK <= LANES
    cq_t, sq_t, ck_t, sk_t = _rope_tables(T)
    onehot, key_feats, pad_rows, cfeat, band, mt, qf = _nsa_tables(T)
    row2 = lambda v: v.reshape(1, -1)
    xc = x.reshape(N, D_MODEL)
    for i in range(DEPTH):
        w_all, wvst, wvwt = _arrange_in_weights(w_in[i])
        wqa, wqb, wka, wv = _arrange_mla_weights(w_uq[i], w_ukv[i])
        (qn, kc, vc, ks0, ks1, vst, kw0, kw1, vwt, gates, qm, km, vmt) = _in_proj(
            xc, w_all, wvst, wvwt, wqa, wqb, wka, wv, row2(mla_q_norm[i]), row2(mla_kv_norm[i]),
            key_feats[0], key_feats[1], cq_t, sq_t, ck_t, sk_t, B, T)
        wk_lo, wk_hi, wk2, pk = _compress_weights(w_ck1[i], w_ck2[i], pos_ck[i])
        wv_lo, wv_hi, wv2, pv = _compress_weights(w_cv1[i], w_cv2[i], pos_cv[i])
        nsub = T // CMP_STRIDE
        kcmp, vcmpt = _compress(kc.reshape(B, nsub, CMP_STRIDE * LANES),
                                vc.reshape(B, nsub, CMP_STRIDE * LANES),
                                pk, pv, wk_lo, wk_hi, wk2, wv_lo, wv_hi, wv2.T)
        pad_k = lambda a, g: jnp.concatenate(
            [jnp.broadcast_to(pad_rows[g], (B, WINDOW, LANES)), a.reshape(B, T, LANES)], axis=1)
        o_nsa = _nsa(qn, kcmp, vcmpt, cfeat, ks0.reshape(B, T, LANES), ks1.reshape(B, T, LANES), vst,
                     pad_k(kw0, 0), pad_k(kw1, 1), jnp.pad(vwt, ((0, 0), (0, 0), (WINDOW, 0))),
                     onehot, band, mt, qf, gates, B, T)
        o_mla = _mla(qm, km.reshape(B, T, -1), vmt, B, T)
        wo = w_out[i]
        nsa_w = N_NSA_HEADS * NSA_D
        wn = wo[:nsa_w].reshape(NSA_GROUPS, NSA_HPG, NSA_D, D_MODEL).transpose(1, 0, 2, 3)
        wn = wn.reshape(nsa_w, D_MODEL).astype(BF16)
        wm = wo[nsa_w:].astype(BF16)
        ln = jnp.stack([ln1_g[i], ln1_b[i], ln2_g[i], ln2_b[i], ln3_g[i], ln3_b[i]])
        xc = _post(o_nsa, o_mla, xc, p[i].reshape(N, D_PLE), wn, wm, w_up[i].astype(BF16),
                   w_down[i].astype(BF16), w_ple_gate[i].astype(BF16), w_ple[i].astype(BF16), ln)
    return xc.reshape(B, T, D_MODEL)
```

```python
import functools

import jax
import jax.numpy as jnp
import numpy as np
from jax import lax
from jax.experimental import pallas as pl
from jax.experimental.pallas import tpu as pltpu

F32 = jnp.float32
BF16 = jnp.bfloat16

D_MODEL = 1024
N_NSA_HEADS = 8
NSA_GROUPS = 2
NSA_HPG = N_NSA_HEADS // NSA_GROUPS
NSA_D = 64
CMP_LEN = 32
CMP_STRIDE = 16
CMP_HIDDEN = 256
SLC_BLOCK = 64
SLC_TOP_N = 16
WINDOW = 512
N_MLA_HEADS = 8
MLA_NOPE = 64
MLA_ROPE = 32
MLA_V = 64
MLA_Q_LORA = 256
MLA_KV_LORA = 128
ROPE_THETA = 10000.0
D_FF = -(-8 * D_MODEL // (3 * 256)) * 256
D_PLE = 256
DEPTH = 1
ALPHA = (2 * DEPTH) ** 0.25
LN_EPS = 1e-5
RMS_EPS = 1e-6
NEG_INF = -1e30
FORCE_SCORE = 1e6
REMOVED_SCORE = -3e38
LOG2_E = 1.4426950408889634

LANES = 128
SUBLANES = 8
V_ROWS = 80
TQ_NSA = 128
R_NSA = NSA_HPG * TQ_NSA
TK_SLC = 512
BLOCKS_PER_CHUNK = TK_SLC // SLC_BLOCK
N_SLC_CHUNKS = LANES // BLOCKS_PER_CHUNK
SLC_UNROLL = 2
WIN_KEYS = WINDOW + TQ_NSA
N_FEAT = 5
TQ_MLA = 512
TK_MLA = 512
MLA_UNROLL = 4
MLA_DEPTH = 2
TM_PROJ = 512
TF_FFN = 256
VMEM_LIMIT = 56 * 1024 * 1024

_C_NQ = 0
_C_KC = _C_NQ + N_NSA_HEADS * LANES
_C_VC = _C_KC + LANES
_C_KS = _C_VC + LANES
_C_KW = _C_KS + NSA_GROUPS * LANES
_C_GL = _C_KW + NSA_GROUPS * LANES
_C_CQ = _C_GL + LANES
_C_CKV = _C_CQ + MLA_Q_LORA
_C_KPA = _C_CKV + MLA_KV_LORA
_C_KPB = _C_KPA + LANES
_C_END = _C_KPB + LANES


def _feat_base(g):
    return NSA_D * (1 - g)


def _nt_dot(a, b):
    return lax.dot_general(a, b, (((1,), (1,)), ((), ())), preferred_element_type=F32)


def _dot(a, b):
    return jnp.dot(a, b, preferred_element_type=F32)


def _layer_norm(v, g, b):
    mu = jnp.mean(v, axis=-1, keepdims=True)
    d = v - mu
    var = jnp.mean(d * d, axis=-1, keepdims=True)
    return d * lax.rsqrt(var + LN_EPS) * g + b


def _rms_norm(v, g):
    return v * lax.rsqrt(jnp.mean(v * v, axis=-1, keepdims=True) + RMS_EPS) * g


def _in_proj_kernel(x_ref, w_ref, wvst_ref, wvwt_ref, wqa_ref, wqb_ref, wka_ref, wv_ref,
                    gq_ref, gkv_ref, ft0_ref, ft1_ref, cq_ref, sq_ref, ck_ref, sk_ref,
                    qn_ref, kc_ref, vc_ref, ks0_ref, ks1_ref, vst_ref, kw0_ref, kw1_ref, vwt_ref,
                    gate_ref, qm_ref, km_ref, vm_ref):
    xb = x_ref[...].astype(BF16)
    h_q = _dot(xb, w_ref[:, _C_NQ:_C_KC])
    h_rest = _dot(xb, w_ref[:, _C_KC:_C_END])

    def seg(c0, width):
        return h_rest[:, c0 - _C_KC:c0 - _C_KC + width]

    for i in range(N_NSA_HEADS):
        qn_ref[0, i] = (h_q[:, i * LANES:(i + 1) * LANES] * (NSA_D ** -0.5)).astype(BF16)
    kc_ref[...] = seg(_C_KC, LANES)
    vc_ref[...] = seg(_C_VC, LANES)
    ks0_ref[...] = (seg(_C_KS, LANES) + ft0_ref[...]).astype(BF16)
    ks1_ref[...] = (seg(_C_KS + LANES, LANES) + ft1_ref[...]).astype(BF16)
    kw0_ref[...] = (seg(_C_KW, LANES) + ft0_ref[...]).astype(BF16)
    kw1_ref[...] = (seg(_C_KW + LANES, LANES) + ft1_ref[...]).astype(BF16)
    def ones_rows(n):
        r = lax.broadcasted_iota(jnp.int32, (n, 1), 0) & (LANES - 1)
        return jnp.where(r == NSA_D, 1.0, 0.0)

    vst_ref[0] = (_nt_dot(wvst_ref[...], xb) + ones_rows(NSA_GROUPS * LANES)).astype(BF16)
    vwt_ref[0] = (_nt_dot(wvwt_ref[...], xb) + ones_rows(NSA_GROUPS * LANES)).astype(BF16)
    gate_ref[...] = jax.nn.sigmoid(seg(_C_GL, LANES))

    cqn = _rms_norm(seg(_C_CQ, MLA_Q_LORA), gq_ref[...]).astype(BF16)
    cq_t = cq_ref[...]
    sq_t = sq_ref[...]
    qa = _dot(cqn, wqa_ref[...])
    qb = _dot(cqn, wqb_ref[...])
    for h in range(N_MLA_HEADS):
        hs = slice(h * LANES, (h + 1) * LANES)
        qm_ref[:, hs] = (qa[:, hs] * cq_t + qb[:, hs] * sq_t).astype(BF16)

    kvn = _rms_norm(seg(_C_CKV, MLA_KV_LORA), gkv_ref[...]).astype(BF16)
    kpe = seg(_C_KPA, LANES) * ck_ref[...] + seg(_C_KPB, LANES) * sk_ref[...]
    ka = _dot(kvn, wka_ref[...])
    for h in range(N_MLA_HEADS):
        hs = slice(h * LANES, (h + 1) * LANES)
        km_ref[:, hs] = (ka[:, hs] + kpe).astype(BF16)
    vm_ref[0] = (_nt_dot(wv_ref[...], kvn) + ones_rows(N_MLA_HEADS * LANES)).astype(BF16)


def _in_proj(x2, w_all, wvst, wvwt, wqa, wqb, wka, wv, gq, gkv, ft0, ft1, cq_t, sq_t, ck_t, sk_t,
             B, T):
    N = B * T
    tm = TM_PROJ
    tpb = T // tm
    full = lambda a: pl.BlockSpec(a.shape, lambda i: (0,) * a.ndim)
    rows = lambda w: pl.BlockSpec((tm, w), lambda i: (i, 0))
    tab = pl.BlockSpec((tm, LANES), lambda i: (i % tpb, 0))
    cols_t = lambda w: pl.BlockSpec((1, w, tm), lambda i: (i // tpb, 0, i % tpb))
    tok = lambda dt: jax.ShapeDtypeStruct((N, LANES), dt)
    out_shape = (
        jax.ShapeDtypeStruct((B, N_NSA_HEADS, T, LANES), BF16),
        tok(F32), tok(F32),
        tok(BF16), tok(BF16),
        jax.ShapeDtypeStruct((B, NSA_GROUPS * LANES, T), BF16),
        tok(BF16), tok(BF16),
        jax.ShapeDtypeStruct((B, NSA_GROUPS * LANES, T), BF16),
        tok(F32),
        jax.ShapeDtypeStruct((N, N_MLA_HEADS * LANES), BF16),
        jax.ShapeDtypeStruct((N, N_MLA_HEADS * LANES), BF16),
        jax.ShapeDtypeStruct((B, N_MLA_HEADS * LANES, T), BF16),
    )
    out_specs = (
        pl.BlockSpec((1, N_NSA_HEADS, tm, LANES), lambda i: (i // tpb, 0, i % tpb, 0)),
        rows(LANES), rows(LANES), rows(LANES), rows(LANES), cols_t(NSA_GROUPS * LANES),
        rows(LANES), rows(LANES), cols_t(NSA_GROUPS * LANES), rows(LANES),
        rows(N_MLA_HEADS * LANES), rows(N_MLA_HEADS * LANES), cols_t(N_MLA_HEADS * LANES),
    )
    return pl.pallas_call(
        _in_proj_kernel,
        grid=(N // tm,),
        in_specs=[rows(D_MODEL), full(w_all), full(wvst), full(wvwt), full(wqa), full(wqb),
                  full(wka), full(wv), full(gq), full(gkv), tab, tab, tab, tab, tab, tab],
        out_specs=out_specs,
        out_shape=out_shape,
        compiler_params=pltpu.CompilerParams(dimension_semantics=("arbitrary",),
                                             vmem_limit_bytes=VMEM_LIMIT),
        name="in_proj",
    )(x2, w_all, wvst, wvwt, wqa, wqb, wka, wv, gq, gkv, ft0, ft1, cq_t, sq_t, ck_t, sk_t)


def _compress_kernel(kc_ref, vc_ref, pk_ref, pv_ref, wk_lo_ref, wk_hi_ref, wk2_ref,
                     wv_lo_ref, wv_hi_ref, wv2t_ref, ko_ref, vo_ref):
    nsub = kc_ref.shape[1]

    def hidden(src_ref, pos_ref, lo_ref, hi_ref):
        a = src_ref[0]
        pos = pos_ref[...]
        p0 = _dot((a + pos[0:1]).astype(BF16), lo_ref[...])
        p1 = _dot((a + pos[1:2]).astype(BF16), hi_ref[...])
        hid = p0 + pltpu.roll(p1, nsub - 1, 0)
        return jax.nn.gelu(hid).astype(BF16)

    k_out = _dot(hidden(kc_ref, pk_ref, wk_lo_ref, wk_hi_ref), wk2_ref[...])
    row = lax.broadcasted_iota(jnp.int32, k_out.shape, 0)
    ko_ref[0] = jnp.where(row < nsub - 1, k_out, 0.0).astype(BF16)
    v_out = _nt_dot(wv2t_ref[...], hidden(vc_ref, pv_ref, wv_lo_ref, wv_hi_ref))
    col = lax.broadcasted_iota(jnp.int32, v_out.shape, 1)
    vo_ref[0] = jnp.where(col < nsub - 1, v_out, 0.0).astype(BF16)


def _compress(kc3, vc3, pk, pv, wk_lo, wk_hi, wk2, wv_lo, wv_hi, wv2t):
    B, nsub, width = kc3.shape
    full = lambda a: pl.BlockSpec(a.shape, lambda b: (0,) * a.ndim)
    per_b = pl.BlockSpec((1, nsub, width), lambda b: (b, 0, 0))
    return pl.pallas_call(
        _compress_kernel,
        grid=(B,),
        in_specs=[per_b, per_b, full(pk), full(pv), full(wk_lo), full(wk_hi), full(wk2),
                  full(wv_lo), full(wv_hi), full(wv2t)],
        out_specs=(pl.BlockSpec((1, nsub, LANES), lambda b: (b, 0, 0)),
                   pl.BlockSpec((1, LANES, nsub), lambda b: (b, 0, 0))),
        out_shape=(jax.ShapeDtypeStruct((B, nsub, LANES), BF16),
                   jax.ShapeDtypeStruct((B, LANES, nsub), BF16)),
        compiler_params=pltpu.CompilerParams(dimension_semantics=("arbitrary",),
                                             vmem_limit_bytes=VMEM_LIMIT),
        name="nsa_compress",
    )(kc3, vc3, pk, pv, wk_lo, wk_hi, wk2, wv_lo, wv_hi, wv2t)


def _nsa_kernel(qn_ref, kcmp_ref, vcmpt_ref, cfeat_ref, ks0_ref, ks1_ref, vst_ref,
                kw0_ref, kw1_ref, vwt_ref, onehot_ref, band_ref, mt_ref, qf_ref, gate_ref,
                o_ref, qaug_ref, lst_ref, m_ref, acc_ref, *, n_top):
    qb = pl.program_id(1)
    q0 = qb * TQ_NSA
    R = R_NSA
    ncmp = kcmp_ref.shape[1]

    col = lax.broadcasted_iota(jnp.int32, (1, R), 1)
    t_row = q0 + (col & (TQ_NSA - 1))
    gates_t = gate_ref[...].T
    tiles_per_chunk = TK_SLC // TQ_NSA
    near_tile0 = jnp.maximum(qb - (tiles_per_chunk - 1), 0)
    near_start = pl.multiple_of(near_tile0 * TQ_NSA, TQ_NSA)
    near_block0 = near_tile0 * (TQ_NSA // SLC_BLOCK)
    n_far = (near_tile0 + tiles_per_chunk - 1) // tiles_per_chunk
    jb1 = lax.broadcasted_iota(jnp.int32, (LANES, TQ_NSA), 0)
    kl_col = lax.broadcasted_iota(jnp.int32, (TK_SLC, 1), 0)
    w0 = pl.multiple_of(q0, TQ_NSA)
    qb_f = qb.astype(F32)

    q_feats, o_wins, o_cmps, imps = [], [], [], []
    for g in range(NSA_GROUPS):
        kw_ref = (kw0_ref, kw1_ref)[g]
        q = qn_ref[0, g * NSA_HPG:(g + 1) * NSA_HPG].reshape(R, LANES)
        q_feat = (q.astype(F32) + qf_ref[g, 0] + qb_f * qf_ref[g, 1]).astype(BF16)
        q_feats.append(q_feat)

        v_rows = slice(g * LANES, g * LANES + V_ROWS)

        sw = _nt_dot(kw_ref[0, pl.ds(w0, WIN_KEYS), :], q_feat) + band_ref[...]
        pw = jnp.exp((sw - jnp.max(sw, axis=0, keepdims=True)).astype(BF16))
        o_win = _dot(vwt_ref[0, v_rows, pl.ds(w0, WIN_KEYS)], pw)
        o_wins.append(o_win[0:NSA_D, :] / o_win[NSA_D:NSA_D + 1, :])

        kc_aug = jnp.concatenate([kcmp_ref[0], cfeat_ref[...]], axis=1)
        q_cmp = jnp.concatenate([q, (qf_ref[g, 2] + qb_f * qf_ref[g, 3]).astype(BF16)], axis=1)
        cend = lax.broadcasted_iota(jnp.int32, (ncmp, 1), 0) * CMP_STRIDE + (CMP_LEN - 1)
        cmask = cend <= t_row
        sc = jnp.where(cmask, _nt_dot(kc_aug, q_cmp), NEG_INF)
        e = jnp.exp(sc - jnp.maximum(jnp.max(sc, axis=0, keepdims=True), 0.1 * NEG_INF))
        den = jnp.sum(e, axis=0, keepdims=True)
        p_cmp = e * jnp.where(den > 0.0, 1.0 / den, 0.0)
        o_cmps.append(_dot(vcmpt_ref[0, g * NSA_D:(g + 1) * NSA_D, :], p_cmp.astype(BF16)))

        p4 = p_cmp[:, 0:TQ_NSA]
        for h in range(1, NSA_HPG):
            p4 = p4 + p_cmp[:, h * TQ_NSA:(h + 1) * TQ_NSA]
        hi = p4.astype(BF16)
        r1 = p4 - hi.astype(F32)
        mid = r1.astype(BF16)
        lo = (r1 - mid.astype(F32)).astype(BF16)
        mt = mt_ref[...]
        imps.append(_dot(mt, hi) + _dot(mt, mid) + _dot(mt, lo))

    imp = jnp.concatenate(imps, axis=1)
    width = NSA_GROUPS * TQ_NSA
    jb = lax.broadcasted_iota(jnp.int32, (LANES, width), 0)
    tl2 = lax.broadcasted_iota(jnp.int32, (LANES, width), 1) & (TQ_NSA - 1)
    cur = (q0 + tl2) >> 6
    forced = (jb == 0) | (jb == cur) | (jb == cur - 1)
    future = jb > cur
    jf = jb.astype(F32)
    rest = jnp.where(forced, REMOVED_SCORE, jnp.where(future, -1.0, imp))
    picked = forced
    for _ in range(max(n_top - 3, 0)):
        mx = jnp.max(rest, axis=0, keepdims=True)
        first = jnp.min(jnp.where(rest == mx, jf, float(LANES)), axis=0, keepdims=True)
        hit = jf == first
        picked = picked | hit
        rest = jnp.where(hit, REMOVED_SCORE, rest)
    allowed2 = jnp.where(picked & jnp.logical_not(future), 1.0, 0.0)

    group_out = []
    for g in range(NSA_GROUPS):
        ks_ref = (ks0_ref, ks1_ref)[g]
        v_rows = slice(g * LANES, g * LANES + V_ROWS)
        o_cmp, o_win = o_cmps[g], o_wins[g]
        allowed = allowed2[:, g * TQ_NSA:(g + 1) * TQ_NSA]
        far = jb1 < near_block0
        selneg_t = jnp.where(allowed > 0.0, 0.0, NEG_INF).T.astype(BF16)
        selneg_far_t = jnp.where((allowed > 0.0) & far, 0.0, NEG_INF).T.astype(BF16)

        qaug_ref[:, 0:LANES] = q_feats[g]
        for h in range(NSA_HPG):
            qaug_ref[h * TQ_NSA:(h + 1) * TQ_NSA, LANES:2 * LANES] = selneg_t
            qaug_ref[h * TQ_NSA:(h + 1) * TQ_NSA, 2 * LANES:3 * LANES] = selneg_far_t

        allowed_far = jnp.where(far, allowed, 0.0)
        count = jnp.int32(0)
        for c in range(N_SLC_CHUNKS):
            used = jnp.max(allowed_far[c * BLOCKS_PER_CHUNK:(c + 1) * BLOCKS_PER_CHUNK, :]) > 0.0
            lst_ref[count] = jnp.int32(c)
            count = count + (used & (c < n_far)).astype(jnp.int32)

        def slc_scores(k0, near, ks_ref=ks_ref):
            k_aug = jnp.concatenate([ks_ref[0, pl.ds(k0, TK_SLC), :],
                                     onehot_ref[pl.ds(k0, TK_SLC), :]], axis=1)
            if near:
                st = _nt_dot(k_aug, qaug_ref[:, 0:2 * LANES])
                st = jnp.where(k0 + kl_col <= t_row, st, NEG_INF)
            else:
                q_far = jnp.concatenate([qaug_ref[:, 0:LANES], qaug_ref[:, 2 * LANES:3 * LANES]],
                                        axis=1)
                st = _nt_dot(k_aug, q_far)
            return st, jnp.max(st, axis=0, keepdims=True)

        def absorb(tile, k0, v_rows=v_rows):
            st, mx = tile
            m_old = m_ref[...]
            m_new = jnp.maximum(m_old, mx)
            p = jnp.exp((st - m_new).astype(BF16))
            pv = _dot(vst_ref[0, v_rows, pl.ds(k0, TK_SLC)], p)
            acc_ref[...] = jnp.exp(m_old - m_new) * acc_ref[...] + pv
            m_ref[...] = m_new

        def steps(chunks, slc_scores=slc_scores, absorb=absorb):
            starts = [pl.multiple_of(c * TK_SLC, TK_SLC) for c in chunks]
            tiles = [slc_scores(k0, False) for k0 in starts]
            for k0, tile in zip(starts, tiles):
                absorb(tile, k0)

        m_ref[...] = jnp.full((1, R), NEG_INF, F32)
        acc_ref[...] = jnp.zeros((V_ROWS, R), F32)
        absorb(slc_scores(near_start, True), near_start)

        def body(i, carry, steps=steps):
            steps([lst_ref[SLC_UNROLL * i + j] for j in range(SLC_UNROLL)])
            return carry

        n_loop = count // SLC_UNROLL
        lax.fori_loop(0, n_loop, body, 0)
        for j in range(SLC_UNROLL - 1):
            @pl.when(count - n_loop * SLC_UNROLL > j)
            def _(j=j, steps=steps):
                steps([lst_ref[n_loop * SLC_UNROLL + j]])
        o_slc = acc_ref[0:NSA_D, :] / acc_ref[NSA_D:NSA_D + 1, :]

        def gate_row(branch):
            rows = [gates_t[(g * NSA_HPG + h) * 3 + branch:(g * NSA_HPG + h) * 3 + branch + 1, :]
                    for h in range(NSA_HPG)]
            return jnp.concatenate(rows, axis=1)

        group_out.append(gate_row(0) * o_cmp + gate_row(1) * o_slc + gate_row(2) * o_win)

    out_t = jnp.concatenate(group_out, axis=0)
    for h in range(NSA_HPG):
        o_ref[:, h * LANES:(h + 1) * LANES] = out_t[:, h * TQ_NSA:(h + 1) * TQ_NSA].T.astype(BF16)


def _nsa(qn, kcmp, vcmpt, cfeat, ks0, ks1, vst, kw0, kw1, vwt, onehot, band, mt, qf, gates, B, T):
    nq = T // TQ_NSA
    n_top = min(SLC_TOP_N, T // SLC_BLOCK)
    per_b = lambda a: pl.BlockSpec((1,) + a.shape[1:], lambda b, i: (b,) + (0,) * (a.ndim - 1))
    full = lambda a: pl.BlockSpec(a.shape, lambda b, i: (0,) * a.ndim)
    return pl.pallas_call(
        functools.partial(_nsa_kernel, n_top=n_top),
        grid=(B, nq),
        in_specs=[pl.BlockSpec((1, N_NSA_HEADS, TQ_NSA, LANES), lambda b, i: (b, 0, i, 0)),
                  per_b(kcmp), per_b(vcmpt), full(cfeat), per_b(ks0), per_b(ks1), per_b(vst),
                  per_b(kw0), per_b(kw1), per_b(vwt), full(onehot), full(band), full(mt), full(qf),
                  pl.BlockSpec((TQ_NSA, LANES), lambda b, i: (b * nq + i, 0))],
        out_specs=pl.BlockSpec((TQ_NSA, NSA_HPG * LANES), lambda b, i: (b * nq + i, 0)),
        out_shape=jax.ShapeDtypeStruct((B * T, NSA_HPG * LANES), BF16),
        scratch_shapes=[pltpu.VMEM((R_NSA, 3 * LANES), BF16),
                        pltpu.SMEM((N_SLC_CHUNKS + 1,), jnp.int32),
                        pltpu.VMEM((1, R_NSA), F32), pltpu.VMEM((V_ROWS, R_NSA), F32)],
        compiler_params=pltpu.CompilerParams(dimension_semantics=("arbitrary", "arbitrary"),
                                             vmem_limit_bytes=VMEM_LIMIT),
        name="nsa_attention",
    )(qn, kcmp, vcmpt, cfeat, ks0, ks1, vst, kw0, kw1, vwt, onehot, band, mt, qf, gates)


def _mla_kernel(q_ref, k_ref, vt_ref, o_ref, m_ref, acc_ref):
    qi = pl.program_id(2)
    q0 = qi * TQ_MLA
    tpos = q0 + lax.broadcasted_iota(jnp.int32, (1, TQ_MLA), 1)
    kl = lax.broadcasted_iota(jnp.int32, (TK_MLA, 1), 0)
    qs = [q_ref[:, hh * LANES:(hh + 1) * LANES] for hh in range(2)]

    def qk(hh, c, causal):
        k0 = pl.multiple_of(c * TK_MLA, TK_MLA)
        st = _nt_dot(k_ref[0, pl.ds(k0, TK_MLA), hh * LANES:(hh + 1) * LANES], qs[hh])
        if causal:
            st = jnp.where(k0 + kl <= tpos, st, NEG_INF)
        return st, jnp.max(st, axis=0, keepdims=True)

    def absorb(hh, tile, c):
        st, mx = tile
        k0 = pl.multiple_of(c * TK_MLA, TK_MLA)
        m_old = m_ref[hh]
        m_new = jnp.maximum(m_old, mx)
        p = jnp.exp2((st - m_new).astype(BF16))
        pv = _dot(vt_ref[0, hh * LANES:(hh + 1) * LANES, pl.ds(k0, TK_MLA)], p)
        acc_ref[hh] = jnp.exp2(m_old - m_new) * acc_ref[hh] + pv
        m_ref[hh] = m_new

    def steps(chunks, causal=False):
        work = [(hh, c) for c in chunks for hh in range(2)]
        tiles = [qk(hh, c, causal) for hh, c in work[:MLA_DEPTH]]
        for i, (hh, c) in enumerate(work):
            absorb(hh, tiles[i], c)
            if i + MLA_DEPTH < len(work):
                nh, nc = work[i + MLA_DEPTH]
                tiles.append(qk(nh, nc, causal))

    m_ref[...] = jnp.full(m_ref.shape, NEG_INF, F32)
    acc_ref[...] = jnp.zeros(acc_ref.shape, F32)
    c_diag = qi // (TK_MLA // TQ_MLA)
    steps([c_diag], causal=True)

    def body(i, carry):
        steps([MLA_UNROLL * i + j for j in range(MLA_UNROLL)])
        return carry

    n_loop = c_diag // MLA_UNROLL
    lax.fori_loop(0, n_loop, body, 0)
    for j in range(MLA_UNROLL - 1):
        @pl.when(c_diag - n_loop * MLA_UNROLL > j)
        def _(j=j):
            steps([n_loop * MLA_UNROLL + j])

    outs = [acc_ref[hh, 0:MLA_V, :] / acc_ref[hh, MLA_V:MLA_V + 1, :] for hh in range(2)]
    o_ref[...] = jnp.concatenate(outs, axis=0).T.astype(BF16)


def _mla(qm, km3, vmt, B, T):
    nq = T // TQ_MLA
    npair = N_MLA_HEADS // 2
    return pl.pallas_call(
        _mla_kernel,
        grid=(B, npair, nq),
        in_specs=[pl.BlockSpec((TQ_MLA, 2 * LANES), lambda b, h, i: (b * nq + i, h)),
                  pl.BlockSpec((1, T, 2 * LANES), lambda b, h, i: (b, 0, h)),
                  pl.BlockSpec((1, 2 * LANES, T), lambda b, h, i: (b, h, 0))],
        out_specs=pl.BlockSpec((TQ_MLA, LANES), lambda b, h, i: (b * nq + i, h)),
        out_shape=jax.ShapeDtypeStruct((B * T, npair * LANES), BF16),
        scratch_shapes=[pltpu.VMEM((2, 1, TQ_MLA), F32), pltpu.VMEM((2, LANES, TQ_MLA), F32)],
        compiler_params=pltpu.CompilerParams(
            dimension_semantics=("arbitrary", "arbitrary", "arbitrary"),
            vmem_limit_bytes=VMEM_LIMIT,
            ),
        name="mla_attention",
    )(qm, km3, vmt)


def _post_kernel(on_ref, om_ref, x_ref, p_ref, wn_ref, wm_ref, wup_ref, wd_ref,
                 wpg_ref, wp_ref, ln_ref, o_ref):
    mix = _dot(on_ref[...], wn_ref[...]) + _dot(om_ref[...], wm_ref[...])
    x1 = _layer_norm(ALPHA * x_ref[...] + mix, ln_ref[0:1, :], ln_ref[1:2, :])
    xb = x1.astype(BF16)
    ffn = jnp.zeros_like(x1)
    for c in range(D_FF // TF_FFN):
        cols = slice(c * TF_FFN, (c + 1) * TF_FFN)
        up_cols = slice(D_FF + c * TF_FFN, D_FF + (c + 1) * TF_FFN)
        hid = jax.nn.silu(_dot(xb, wup_ref[:, cols])) * _dot(xb, wup_ref[:, up_cols])
        ffn = ffn + _dot(hid.astype(BF16), wd_ref[cols, :])
    x2 = _layer_norm(ALPHA * x1 + ffn, ln_ref[2:3, :], ln_ref[3:4, :])
    gate = jax.nn.sigmoid(_dot(x2.astype(BF16), wpg_ref[...]))
    ple = gate * _dot(p_ref[...].astype(BF16), wp_ref[...])
    o_ref[...] = _layer_norm(ALPHA * x2 + ple, ln_ref[4:5, :], ln_ref[5:6, :])


def _post(o_nsa, o_mla, x2, p2, wn, wm, w_up, w_down, wpg, wp, ln):
    N = x2.shape[0]
    tm = TM_PROJ
    rows = lambda w: pl.BlockSpec((tm, w), lambda i: (i, 0))
    full = lambda a: pl.BlockSpec(a.shape, lambda i: (0,) * a.ndim)
    return pl.pallas_call(
        _post_kernel,
        grid=(N // tm,),
        in_specs=[rows(o_nsa.shape[1]), rows(o_mla.shape[1]), rows(D_MODEL), rows(D_PLE),
                  full(wn), full(wm), full(w_up), full(w_down), full(wpg), full(wp), full(ln)],
        out_specs=rows(D_MODEL),
        out_shape=jax.ShapeDtypeStruct((N, D_MODEL), F32),
        compiler_params=pltpu.CompilerParams(dimension_semantics=("arbitrary",),
                                             vmem_limit_bytes=VMEM_LIMIT),
        name="post_attention",
    )(o_nsa, o_mla, x2, p2, wn, wm, w_up, w_down, wpg, wp, ln)


def _arrange_in_weights(w_in):
    splits = (N_NSA_HEADS * NSA_D,) + (NSA_GROUPS * NSA_D,) * 6 + (3 * N_NSA_HEADS, MLA_Q_LORA,
                                                                  MLA_KV_LORA, MLA_ROPE)
    offs = np.cumsum((0,) + splits)
    part = lambda i: w_in[:, offs[i]:offs[i + 1]]

    def place(w, lane0):
        return jnp.pad(w, ((0, 0), (lane0, LANES - lane0 - w.shape[1])))

    def group_only(w, g):
        return place(w[:, g * NSA_D:(g + 1) * NSA_D], g * NSA_D)

    nq = part(0).reshape(D_MODEL, NSA_GROUPS, NSA_HPG, NSA_D)
    heads = [place(nq[:, g, h], g * NSA_D) for g in range(NSA_GROUPS) for h in range(NSA_HPG)]
    gl = place(part(7), 0)
    kpe = part(10)
    half = MLA_ROPE // 2
    kpa = place(kpe, MLA_NOPE)
    kpb = place(jnp.concatenate([kpe[:, half:], kpe[:, :half]], axis=1), MLA_NOPE)
    cols = heads + [part(1), part(2), group_only(part(3), 0), group_only(part(3), 1),
                    group_only(part(5), 0), group_only(part(5), 1), gl, part(8), part(9), kpa, kpb]
    w_all = jnp.concatenate(cols, axis=1)
    assert w_all.shape[1] == _C_END

    def value_rows(w):
        return jnp.concatenate([place(w[:, g * NSA_D:(g + 1) * NSA_D], 0)
                                for g in range(NSA_GROUPS)], axis=1).T

    return w_all.astype(BF16), value_rows(part(4)).astype(BF16), value_rows(part(6)).astype(BF16)


def _arrange_mla_weights(w_uq, w_ukv):
    half = MLA_ROPE // 2
    wq = w_uq.reshape(MLA_Q_LORA, N_MLA_HEADS, MLA_NOPE + MLA_ROPE)

    def place(w, lane0):
        return jnp.pad(w, ((0, 0), (0, 0), (lane0, LANES - lane0 - w.shape[2])))

    wqa = place(wq, 0)
    wqb = place(jnp.concatenate([wq[:, :, MLA_NOPE + half:], wq[:, :, MLA_NOPE:MLA_NOPE + half]],
                                axis=2), MLA_NOPE)
    wkv = w_ukv.reshape(MLA_KV_LORA, N_MLA_HEADS, MLA_NOPE + MLA_V)
    wka = place(wkv[:, :, :MLA_NOPE], 0)
    wv = place(wkv[:, :, MLA_NOPE:], 0)
    flat = lambda a: a.reshape(a.shape[0], -1).astype(BF16)
    return flat(wqa), flat(wqb), flat(wka), flat(wv).T


def _rope_tables(T):
    half = MLA_ROPE // 2
    pos = jnp.arange(T, dtype=F32)
    inv_freq = ROPE_THETA ** (-jnp.arange(half, dtype=F32) / half)
    ang = pos[:, None] * inv_freq[None, :]
    cos, sin = jnp.cos(ang), jnp.sin(ang)
    scale = (MLA_NOPE + MLA_ROPE) ** -0.5 * LOG2_E
    ones = jnp.ones((T, MLA_NOPE), F32)
    tail = jnp.zeros((T, LANES - MLA_NOPE - MLA_ROPE), F32)
    cos_row = lambda head: jnp.concatenate([head, cos, cos, tail], axis=1)
    sin_row = jnp.concatenate([0.0 * ones, -sin, sin, tail], axis=1)
    return cos_row(ones) * scale, sin_row * scale, cos_row(0.0 * ones), sin_row


def _compress_weights(w1, w2, pos):
    G, D = NSA_GROUPS, NSA_D
    w1r = w1.reshape(CMP_LEN, D, CMP_HIDDEN)
    eye = jnp.eye(G, dtype=bool)
    halves = []
    for j in range(CMP_LEN // CMP_STRIDE):
        part = w1r[j * CMP_STRIDE:(j + 1) * CMP_STRIDE]
        wide = jnp.where(eye[None, :, None, :, None], part[:, None, :, None, :], 0.0)
        halves.append(wide.reshape(CMP_STRIDE * G * D, G * CMP_HIDDEN).astype(BF16))
    w2bd = jnp.where(eye[:, None, :, None], w2[None, :, None, :], 0.0)
    w2bd = w2bd.reshape(G * CMP_HIDDEN, G * D).astype(BF16)
    posr = pos.reshape(CMP_LEN // CMP_STRIDE, CMP_STRIDE, 1, D)
    posw = jnp.broadcast_to(posr, (CMP_LEN // CMP_STRIDE, CMP_STRIDE, G, D)).reshape(
        CMP_LEN // CMP_STRIDE, CMP_STRIDE * G * D)
    return halves[0], halves[1], w2bd, posw


def _position_features(pos):
    return np.stack([(pos // LANES) * LANES, pos % LANES, np.ones_like(pos), np.ones_like(pos)],
                    axis=1).astype(np.float32)


def _nsa_tables(T):
    kpos = np.arange(T)
    onehot = np.zeros((T, LANES), np.float32)
    onehot[kpos, kpos // SLC_BLOCK] = 1.0
    key_feats, pad_rows = [], []
    for g in range(NSA_GROUPS):
        base = _feat_base(g)
        ft = np.zeros((T, LANES), np.float32)
        ft[:, base:base + 4] = _position_features(kpos)
        key_feats.append(jnp.asarray(ft))
        pad = np.zeros((WINDOW, LANES), np.float32)
        pad[:, base + 4] = 1.0
        pad_rows.append(jnp.asarray(pad, BF16))
    nsub = T // CMP_STRIDE
    c = np.arange(nsub)
    cfeat = np.zeros((nsub, LANES), np.float32)
    cfeat[:, 0:4] = _position_features(c * CMP_STRIDE + CMP_LEN - 1)
    kl = np.arange(WIN_KEYS)[:, None]
    tl = np.arange(TQ_NSA)[None, :]
    band = np.where((kl > tl) & (kl <= tl + WINDOW), 0.0, NEG_INF).astype(np.float32)
    band = np.tile(band, (1, NSA_HPG))
    ratio = SLC_BLOCK // CMP_STRIDE
    mt = np.zeros((LANES, nsub), np.float32)
    valid = c < nsub - 1
    for j in range(CMP_LEN // CMP_STRIDE):
        np.add.at(mt, ((c[valid] + j) // ratio, c[valid]), 1.0)
    qf = np.zeros((NSA_GROUPS, 4, R_NSA, LANES), np.float32)
    tl_rows = np.tile(np.arange(TQ_NSA), NSA_HPG).astype(np.float32)
    for g in range(NSA_GROUPS):
        slope = np.repeat(2.0 ** -(g * NSA_HPG + np.arange(NSA_HPG) + 1.0), TQ_NSA).astype(np.float32)
        for k, base in enumerate((_feat_base(g), 0)):
            qf[g, 2 * k, :, base] = slope
            qf[g, 2 * k, :, base + 1] = slope
            qf[g, 2 * k, :, base + 3] = -slope * tl_rows
            qf[g, 2 * k, :, base + 4] = NEG_INF
            qf[g, 2 * k + 1, :, base + 2] = -slope * TQ_NSA
    return (jnp.asarray(onehot, BF16), key_feats, pad_rows, jnp.asarray(cfeat, BF16),
            jnp.asarray(band), jnp.asarray(mt, BF16), jnp.asarray(qf))


def kernel(x, p, w_in, w_ck1, w_ck2, pos_ck, w_cv1, w_cv2, pos_cv, mla_q_norm, w_uq, mla_kv_norm,
           w_ukv, w_out, ln1_g, ln1_b, w_up, w_down, ln2_g, ln2_b, w_ple_gate, w_ple, ln3_g, ln3_b):
    B, T, _ = x.shape
    N = B * T
    assert T % TQ_MLA == 0 and T % TM_PROJ == 0 and T // SLC_BLOCK <= LANES
    cq_t, sq_t, ck_t, sk_t = _rope_tables(T)
    onehot, key_feats, pad_rows, cfeat, band, mt, qf = _nsa_tables(T)
    row2 = lambda v: v.reshape(1, -1)
    xc = x.reshape(N, D_MODEL)
    for i in range(DEPTH):
        w_all, wvst, wvwt = _arrange_in_weights(w_in[i])
        wqa, wqb, wka, wv = _arrange_mla_weights(w_uq[i], w_ukv[i])
        (qn, kc, vc, ks0, ks1, vst, kw0, kw1, vwt, gates, qm, km, vmt) = _in_proj(
            xc, w_all, wvst, wvwt, wqa, wqb, wka, wv, row2(mla_q_norm[i]), row2(mla_kv_norm[i]),
            key_feats[0], key_feats[1], cq_t, sq_t, ck_t, sk_t, B, T)
        wk_lo, wk_hi, wk2, pk = _compress_weights(w_ck1[i], w_ck2[i], pos_ck[i])
        wv_lo, wv_hi, wv2, pv = _compress_weights(w_cv1[i], w_cv2[i], pos_cv[i])
        nsub = T // CMP_STRIDE
        kcmp, vcmpt = _compress(kc.reshape(B, nsub, CMP_STRIDE * LANES),
                                vc.reshape(B, nsub, CMP_STRIDE * LANES),
                                pk, pv, wk_lo, wk_hi, wk2, wv_lo, wv_hi, wv2.T)
        pad_k = lambda a, g: jnp.concatenate(
            [jnp.broadcast_to(pad_rows[g], (B, WINDOW, LANES)), a.reshape(B, T, LANES)], axis=1)
        o_nsa = _nsa(qn, kcmp, vcmpt, cfeat, ks0.reshape(B, T, LANES), ks1.reshape(B, T, LANES), vst,
                     pad_k(kw0, 0), pad_k(kw1, 1), jnp.pad(vwt, ((0, 0), (0, 0), (WINDOW, 0))),
                     onehot, band, mt, qf, gates, B, T)
        o_mla = _mla(qm, km.reshape(B, T, -1), vmt, B, T)
        wo = w_out[i]
        nsa_w = N_NSA_HEADS * NSA_D
        wn = wo[:nsa_w].reshape(NSA_GROUPS, NSA_HPG, NSA_D, D_MODEL).transpose(1, 0, 2, 3)
        wn = wn.reshape(nsa_w, D_MODEL).astype(BF16)
        wm = wo[nsa_w:].astype(BF16)
        ln = jnp.stack([ln1_g[i], ln1_b[i], ln2_g[i], ln2_b[i], ln3_g[i], ln3_b[i]])
        xc = _post(o_nsa, o_mla, xc, p[i].reshape(N, D_PLE), wn, wm, w_up[i].astype(BF16),
                   w_down[i].astype(BF16), w_ple_gate[i].astype(BF16), w_ple[i].astype(BF16), ln)
    return xc.reshape(B, T, D_MODEL)
```

```python
import functools

import jax
import jax.numpy as jnp
import numpy as np
from jax import lax
from jax.experimental import pallas as pl
from jax.experimental.pallas import tpu as pltpu

F32 = jnp.float32
BF16 = jnp.bfloat16

D_MODEL = 1024
N_NSA_HEADS = 8
NSA_GROUPS = 2
NSA_HPG = N_NSA_HEADS // NSA_GROUPS
NSA_D = 64
CMP_LEN = 32
CMP_STRIDE = 16
CMP_HIDDEN = 256
SLC_BLOCK = 64
SLC_TOP_N = 16
WINDOW = 512
N_MLA_HEADS = 8
MLA_NOPE = 64
MLA_ROPE = 32
MLA_V = 64
MLA_Q_LORA = 256
MLA_KV_LORA = 128
ROPE_THETA = 10000.0
D_FF = -(-8 * D_MODEL // (3 * 256)) * 256
D_PLE = 256
DEPTH = 1
ALPHA = (2 * DEPTH) ** 0.25
LN_EPS = 1e-5
RMS_EPS = 1e-6
NEG_INF = -1e30
FORCE_SCORE = 1e6
REMOVED_SCORE = -3e38
LOG2_E = 1.4426950408889634

LANES = 128
SUBLANES = 8
V_ROWS = 80
TQ_NSA = 128
R_NSA = NSA_HPG * TQ_NSA
TK_SLC = 512
BLOCKS_PER_CHUNK = TK_SLC // SLC_BLOCK
N_SLC_CHUNKS = LANES // BLOCKS_PER_CHUNK
SLC_UNROLL = 2
WIN_KEYS = WINDOW + TQ_NSA
N_FEAT = 5
TQ_MLA = 512
TK_MLA = 512
MLA_UNROLL = 4
MLA_DEPTH = 2
TM_PROJ = 512
TF_FFN = 256
VMEM_LIMIT = 56 * 1024 * 1024

_C_NQ = 0
_C_KC = _C_NQ + N_NSA_HEADS * NSA_D
_C_VC = _C_KC + LANES
_C_KS = _C_VC + LANES
_C_KW = _C_KS + LANES
_C_GL = _C_KW + LANES
_C_CQ = _C_GL + LANES
_C_CKV = _C_CQ + MLA_Q_LORA
_C_KPA = _C_CKV + MLA_KV_LORA
_C_END = _C_KPA + LANES


def _feat_base(g):
    return NSA_D * (1 - g)


def _nt_dot(a, b):
    return lax.dot_general(a, b, (((1,), (1,)), ((), ())), preferred_element_type=F32)


def _dot(a, b):
    return jnp.dot(a, b, preferred_element_type=F32)


def _layer_norm(v, g, b):
    mu = jnp.mean(v, axis=-1, keepdims=True)
    d = v - mu
    var = jnp.mean(d * d, axis=-1, keepdims=True)
    return d * lax.rsqrt(var + LN_EPS) * g + b


def _rms_norm(v, g):
    return v * lax.rsqrt(jnp.mean(v * v, axis=-1, keepdims=True) + RMS_EPS) * g


def _in_proj_kernel(x_ref, w_ref, wvst_ref, wvwt_ref, wqa_ref, wka_ref, wv_ref,
                    gq_ref, gkv_ref, ft0_ref, ft1_ref, cq_ref, sq_ref, ck_ref, sk_ref,
                    qn_ref, kc_ref, vc_ref, ks0_ref, ks1_ref, vst_ref, kw0_ref, kw1_ref, vwt_ref,
                    gate_ref, qm_ref, km_ref, vm_ref):
    xb = x_ref[...].astype(BF16)
    h = _dot(xb, w_ref[...])
    tm = h.shape[0]
    lane = lax.broadcasted_iota(jnp.int32, (tm, LANES), 1)
    in_half = [lane < NSA_D, lane >= NSA_D]

    def seg(c0, width):
        return h[:, c0:c0 + width]

    def swap_rotary(v):
        half = MLA_ROPE // 2
        first = (lane >= MLA_NOPE) & (lane < MLA_NOPE + half)
        second = (lane >= MLA_NOPE + half) & (lane < MLA_NOPE + MLA_ROPE)
        return jnp.where(first, pltpu.roll(v, LANES - half, 1),
                         jnp.where(second, pltpu.roll(v, half, 1), 0.0))

    for i in range(N_NSA_HEADS):
        g = i // NSA_HPG
        src = seg(_C_NQ + (i // 2) * LANES, LANES)
        if i % 2 != g:
            src = pltpu.roll(src, NSA_D, 1)
        qn_ref[0, i] = (jnp.where(in_half[g], src, 0.0) * (NSA_D ** -0.5)).astype(BF16)
    kc_ref[...] = seg(_C_KC, LANES)
    vc_ref[...] = seg(_C_VC, LANES)
    h_ks, h_kw = seg(_C_KS, LANES), seg(_C_KW, LANES)
    ks0_ref[...] = (jnp.where(in_half[0], h_ks, 0.0) + ft0_ref[...]).astype(BF16)
    ks1_ref[...] = (jnp.where(in_half[1], h_ks, 0.0) + ft1_ref[...]).astype(BF16)
    kw0_ref[...] = (jnp.where(in_half[0], h_kw, 0.0) + ft0_ref[...]).astype(BF16)
    kw1_ref[...] = (jnp.where(in_half[1], h_kw, 0.0) + ft1_ref[...]).astype(BF16)
    def ones_rows(n):
        r = lax.broadcasted_iota(jnp.int32, (n, 1), 0) & (LANES - 1)
        return jnp.where(r == NSA_D, 1.0, 0.0)

    vst_ref[0] = (_nt_dot(wvst_ref[...], xb) + ones_rows(NSA_GROUPS * LANES)).astype(BF16)
    vwt_ref[0] = (_nt_dot(wvwt_ref[...], xb) + ones_rows(NSA_GROUPS * LANES)).astype(BF16)
    gate_ref[...] = jax.nn.sigmoid(seg(_C_GL, LANES))

    cqn = _rms_norm(seg(_C_CQ, MLA_Q_LORA), gq_ref[...]).astype(BF16)
    cq_t = cq_ref[...]
    sq_t = sq_ref[...]
    qa = _dot(cqn, wqa_ref[...])
    for hd in range(N_MLA_HEADS):
        hs = slice(hd * LANES, (hd + 1) * LANES)
        qm_ref[:, hs] = (qa[:, hs] * cq_t + swap_rotary(qa[:, hs]) * sq_t).astype(BF16)

    kvn = _rms_norm(seg(_C_CKV, MLA_KV_LORA), gkv_ref[...]).astype(BF16)
    kpa = seg(_C_KPA, LANES)
    kpe = kpa * ck_ref[...] + swap_rotary(kpa) * sk_ref[...]
    ka = _dot(kvn, wka_ref[...])
    for hd in range(N_MLA_HEADS):
        hs = slice(hd * LANES, (hd + 1) * LANES)
        km_ref[:, hs] = (ka[:, hs] + kpe).astype(BF16)
    vm_ref[0] = (_nt_dot(wv_ref[...], kvn) + ones_rows(N_MLA_HEADS * LANES)).astype(BF16)


def _in_proj(x2, w_all, wvst, wvwt, wqa, wka, wv, gq, gkv, ft0, ft1, cq_t, sq_t, ck_t, sk_t,
             B, T):
    N = B * T
    tm = TM_PROJ
    tpb = T // tm
    full = lambda a: pl.BlockSpec(a.shape, lambda i: (0,) * a.ndim)
    rows = lambda w: pl.BlockSpec((tm, w), lambda i: (i, 0))
    tab = pl.BlockSpec((tm, LANES), lambda i: (i % tpb, 0))
    cols_t = lambda w: pl.BlockSpec((1, w, tm), lambda i: (i // tpb, 0, i % tpb))
    tok = lambda dt: jax.ShapeDtypeStruct((N, LANES), dt)
    out_shape = (
        jax.ShapeDtypeStruct((B, N_NSA_HEADS, T, LANES), BF16),
        tok(F32), tok(F32),
        tok(BF16), tok(BF16),
        jax.ShapeDtypeStruct((B, NSA_GROUPS * LANES, T), BF16),
        tok(BF16), tok(BF16),
        jax.ShapeDtypeStruct((B, NSA_GROUPS * LANES, T), BF16),
        tok(F32),
        jax.ShapeDtypeStruct((N, N_MLA_HEADS * LANES), BF16),
        jax.ShapeDtypeStruct((N, N_MLA_HEADS * LANES), BF16),
        jax.ShapeDtypeStruct((B, N_MLA_HEADS * LANES, T), BF16),
    )
    out_specs = (
        pl.BlockSpec((1, N_NSA_HEADS, tm, LANES), lambda i: (i // tpb, 0, i % tpb, 0)),
        rows(LANES), rows(LANES), rows(LANES), rows(LANES), cols_t(NSA_GROUPS * LANES),
        rows(LANES), rows(LANES), cols_t(NSA_GROUPS * LANES), rows(LANES),
        rows(N_MLA_HEADS * LANES), rows(N_MLA_HEADS * LANES), cols_t(N_MLA_HEADS * LANES),
    )
    return pl.pallas_call(
        _in_proj_kernel,
        grid=(N // tm,),
        in_specs=[rows(D_MODEL), full(w_all), full(wvst), full(wvwt), full(wqa),
                  full(wka), full(wv), full(gq), full(gkv), tab, tab, tab, tab, tab, tab],
        out_specs=out_specs,
        out_shape=out_shape,
        compiler_params=pltpu.CompilerParams(dimension_semantics=("arbitrary",),
                                             vmem_limit_bytes=VMEM_LIMIT),
        name="in_proj",
    )(x2, w_all, wvst, wvwt, wqa, wka, wv, gq, gkv, ft0, ft1, cq_t, sq_t, ck_t, sk_t)


def _compress_kernel(kc_ref, vc_ref, pk_ref, pv_ref, wk_lo_ref, wk_hi_ref, wk2_ref,
                     wv_lo_ref, wv_hi_ref, wv2t_ref, ko_ref, vo_ref):
    nsub = kc_ref.shape[1]

    def hidden(src_ref, pos_ref, lo_ref, hi_ref):
        a = src_ref[0]
        pos = pos_ref[...]
        p0 = _dot((a + pos[0:1]).astype(BF16), lo_ref[...])
        p1 = _dot((a + pos[1:2]).astype(BF16), hi_ref[...])
        hid = p0 + pltpu.roll(p1, nsub - 1, 0)
        return jax.nn.gelu(hid).astype(BF16)

    k_out = _dot(hidden(kc_ref, pk_ref, wk_lo_ref, wk_hi_ref), wk2_ref[...])
    row = lax.broadcasted_iota(jnp.int32, k_out.shape, 0)
    ko_ref[0] = jnp.where(row < nsub - 1, k_out, 0.0).astype(BF16)
    v_out = _nt_dot(wv2t_ref[...], hidden(vc_ref, pv_ref, wv_lo_ref, wv_hi_ref))
    col = lax.broadcasted_iota(jnp.int32, v_out.shape, 1)
    vo_ref[0] = jnp.where(col < nsub - 1, v_out, 0.0).astype(BF16)


def _compress(kc3, vc3, pk, pv, wk_lo, wk_hi, wk2, wv_lo, wv_hi, wv2t):
    B, nsub, width = kc3.shape
    full = lambda a: pl.BlockSpec(a.shape, lambda b: (0,) * a.ndim)
    per_b = pl.BlockSpec((1, nsub, width), lambda b: (b, 0, 0))
    return pl.pallas_call(
        _compress_kernel,
        grid=(B,),
        in_specs=[per_b, per_b, full(pk), full(pv), full(wk_lo), full(wk_hi), full(wk2),
                  full(wv_lo), full(wv_hi), full(wv2t)],
        out_specs=(pl.BlockSpec((1, nsub, LANES), lambda b: (b, 0, 0)),
                   pl.BlockSpec((1, LANES, nsub), lambda b: (b, 0, 0))),
        out_shape=(jax.ShapeDtypeStruct((B, nsub, LANES), BF16),
                   jax.ShapeDtypeStruct((B, LANES, nsub), BF16)),
        compiler_params=pltpu.CompilerParams(dimension_semantics=("arbitrary",),
                                             vmem_limit_bytes=VMEM_LIMIT),
        name="nsa_compress",
    )(kc3, vc3, pk, pv, wk_lo, wk_hi, wk2, wv_lo, wv_hi, wv2t)


def _nsa_kernel(qn_ref, kcmp_ref, vcmpt_ref, cfeat_ref, ks0_ref, ks1_ref, vst_ref,
                kw0_ref, kw1_ref, vwt_ref, onehot_ref, band_ref, mt_ref, qf_ref, gate_ref,
                o_ref, qaug_ref, lst_ref, m_ref, acc_ref, *, n_top):
    qb = pl.program_id(1)
    q0 = qb * TQ_NSA
    R = R_NSA
    ncmp = kcmp_ref.shape[1]

    col = lax.broadcasted_iota(jnp.int32, (1, R), 1)
    t_row = q0 + (col & (TQ_NSA - 1))
    gates_t = gate_ref[...].T
    c_diag = qb // (TK_SLC // TQ_NSA)
    kl_col = lax.broadcasted_iota(jnp.int32, (TK_SLC, 1), 0)
    w0 = pl.multiple_of(q0, TQ_NSA)
    qb_f = qb.astype(F32)

    q_feats, o_wins, o_cmps, imps = [], [], [], []
    for g in range(NSA_GROUPS):
        kw_ref = (kw0_ref, kw1_ref)[g]
        q = qn_ref[0, g * NSA_HPG:(g + 1) * NSA_HPG].reshape(R, LANES)
        q_feat = (q.astype(F32) + qf_ref[g, 0] + qb_f * qf_ref[g, 1]).astype(BF16)
        q_feats.append(q_feat)

        v_rows = slice(g * LANES, g * LANES + V_ROWS)

        sw = _nt_dot(kw_ref[0, pl.ds(w0, WIN_KEYS), :], q_feat) + band_ref[...]
        pw = jnp.exp((sw - jnp.max(sw, axis=0, keepdims=True)).astype(BF16))
        o_win = _dot(vwt_ref[0, v_rows, pl.ds(w0, WIN_KEYS)], pw)
        o_wins.append(o_win[0:NSA_D, :] / o_win[NSA_D:NSA_D + 1, :])

        kc_aug = jnp.concatenate([kcmp_ref[0], cfeat_ref[...]], axis=1)
        q_cmp = jnp.concatenate([q, (qf_ref[g, 2] + qb_f * qf_ref[g, 3]).astype(BF16)], axis=1)
        cend = lax.broadcasted_iota(jnp.int32, (ncmp, 1), 0) * CMP_STRIDE + (CMP_LEN - 1)
        cmask = cend <= t_row
        sc = jnp.where(cmask, _nt_dot(kc_aug, q_cmp), NEG_INF)
        e = jnp.exp(sc - jnp.maximum(jnp.max(sc, axis=0, keepdims=True), 0.1 * NEG_INF))
        den = jnp.sum(e, axis=0, keepdims=True)
        p_cmp = e * jnp.where(den > 0.0, 1.0 / den, 0.0)
        o_cmps.append(_dot(vcmpt_ref[0, g * NSA_D:(g + 1) * NSA_D, :], p_cmp.astype(BF16)))

        p4 = p_cmp[:, 0:TQ_NSA]
        for h in range(1, NSA_HPG):
            p4 = p4 + p_cmp[:, h * TQ_NSA:(h + 1) * TQ_NSA]
        hi = p4.astype(BF16)
        r1 = p4 - hi.astype(F32)
        mid = r1.astype(BF16)
        lo = (r1 - mid.astype(F32)).astype(BF16)
        mt = mt_ref[...]
        imps.append(_dot(mt, hi) + _dot(mt, mid) + _dot(mt, lo))

    imp = jnp.concatenate(imps, axis=1)
    width = NSA_GROUPS * TQ_NSA
    jb = lax.broadcasted_iota(jnp.int32, (LANES, width), 0)
    tl2 = lax.broadcasted_iota(jnp.int32, (LANES, width), 1) & (TQ_NSA - 1)
    cur = (q0 + tl2) >> 6
    forced = (jb == 0) | (jb == cur) | (jb == cur - 1)
    future = jb > cur
    jf = jb.astype(F32)
    rest = jnp.where(forced, REMOVED_SCORE, jnp.where(future, -1.0, imp))
    picked = forced
    for _ in range(max(n_top - 3, 0)):
        mx = jnp.max(rest, axis=0, keepdims=True)
        first = jnp.min(jnp.where(rest == mx, jf, float(LANES)), axis=0, keepdims=True)
        hit = jf == first
        picked = picked | hit
        rest = jnp.where(hit, REMOVED_SCORE, rest)
    allowed2 = jnp.where(picked & jnp.logical_not(future), 1.0, 0.0)

    group_out = []
    for g in range(NSA_GROUPS):
        ks_ref = (ks0_ref, ks1_ref)[g]
        v_rows = slice(g * LANES, g * LANES + V_ROWS)
        o_cmp, o_win = o_cmps[g], o_wins[g]
        allowed = allowed2[:, g * TQ_NSA:(g + 1) * TQ_NSA]
        selneg_t = jnp.where(allowed > 0.0, 0.0, NEG_INF).T.astype(BF16)

        qaug_ref[:, 0:LANES] = q_feats[g]
        for h in range(NSA_HPG):
            qaug_ref[h * TQ_NSA:(h + 1) * TQ_NSA, LANES:2 * LANES] = selneg_t

        count = jnp.int32(0)
        for c in range(N_SLC_CHUNKS):
            used = jnp.max(allowed[c * BLOCKS_PER_CHUNK:(c + 1) * BLOCKS_PER_CHUNK, :]) > 0.0
            lst_ref[count] = jnp.int32(c)
            count = count + (used & (c < c_diag)).astype(jnp.int32)

        def slc_scores(c, causal, ks_ref=ks_ref):
            k0 = pl.multiple_of(c * TK_SLC, TK_SLC)
            k_aug = jnp.concatenate([ks_ref[0, pl.ds(k0, TK_SLC), :],
                                     onehot_ref[pl.ds(k0, TK_SLC), :]], axis=1)
            st = _nt_dot(k_aug, qaug_ref[...])
            if causal:
                st = jnp.where(k0 + kl_col <= t_row, st, NEG_INF)
            return st, jnp.max(st, axis=0, keepdims=True)

        def absorb(tile, c, v_rows=v_rows):
            st, mx = tile
            k0 = pl.multiple_of(c * TK_SLC, TK_SLC)
            m_old = m_ref[...]
            m_new = jnp.maximum(m_old, mx)
            p = jnp.exp((st - m_new).astype(BF16))
            pv = _dot(vst_ref[0, v_rows, pl.ds(k0, TK_SLC)], p)
            acc_ref[...] = jnp.exp(m_old - m_new) * acc_ref[...] + pv
            m_ref[...] = m_new

        def steps(chunks, causal=False, slc_scores=slc_scores, absorb=absorb):
            tiles = [slc_scores(c, causal) for c in chunks]
            for c, tile in zip(chunks, tiles):
                absorb(tile, c)

        m_ref[...] = jnp.full((1, R), NEG_INF, F32)
        acc_ref[...] = jnp.zeros((V_ROWS, R), F32)
        steps([c_diag], causal=True)

        def body(i, carry, steps=steps):
            steps([lst_ref[SLC_UNROLL * i + j] for j in range(SLC_UNROLL)])
            return carry

        n_loop = count // SLC_UNROLL
        lax.fori_loop(0, n_loop, body, 0)
        for j in range(SLC_UNROLL - 1):
            @pl.when(count - n_loop * SLC_UNROLL > j)
            def _(j=j, steps=steps):
                steps([lst_ref[n_loop * SLC_UNROLL + j]])
        o_slc = acc_ref[0:NSA_D, :] / acc_ref[NSA_D:NSA_D + 1, :]

        def gate_row(branch):
            rows = [gates_t[(g * NSA_HPG + h) * 3 + branch:(g * NSA_HPG + h) * 3 + branch + 1, :]
                    for h in range(NSA_HPG)]
            return jnp.concatenate(rows, axis=1)

        group_out.append(gate_row(0) * o_cmp + gate_row(1) * o_slc + gate_row(2) * o_win)

    out_t = jnp.concatenate(group_out, axis=0)
    for h in range(NSA_HPG):
        o_ref[:, h * LANES:(h + 1) * LANES] = out_t[:, h * TQ_NSA:(h + 1) * TQ_NSA].T.astype(BF16)


def _nsa(qn, kcmp, vcmpt, cfeat, ks0, ks1, vst, kw0, kw1, vwt, onehot, band, mt, qf, gates, B, T):
    nq = T // TQ_NSA
    n_top = min(SLC_TOP_N, T // SLC_BLOCK)
    per_b = lambda a: pl.BlockSpec((1,) + a.shape[1:], lambda b, i: (b,) + (0,) * (a.ndim - 1))
    full = lambda a: pl.BlockSpec(a.shape, lambda b, i: (0,) * a.ndim)
    return pl.pallas_call(
        functools.partial(_nsa_kernel, n_top=n_top),
        grid=(B, nq),
        in_specs=[pl.BlockSpec((1, N_NSA_HEADS, TQ_NSA, LANES), lambda b, i: (b, 0, i, 0)),
                  per_b(kcmp), per_b(vcmpt), full(cfeat), per_b(ks0), per_b(ks1), per_b(vst),
                  per_b(kw0), per_b(kw1), per_b(vwt), full(onehot), full(band), full(mt), full(qf),
                  pl.BlockSpec((TQ_NSA, LANES), lambda b, i: (b * nq + i, 0))],
        out_specs=pl.BlockSpec((TQ_NSA, NSA_HPG * LANES), lambda b, i: (b * nq + i, 0)),
        out_shape=jax.ShapeDtypeStruct((B * T, NSA_HPG * LANES), BF16),
        scratch_shapes=[pltpu.VMEM((R_NSA, 2 * LANES), BF16),
                        pltpu.SMEM((N_SLC_CHUNKS + 1,), jnp.int32),
                        pltpu.VMEM((1, R_NSA), F32), pltpu.VMEM((V_ROWS, R_NSA), F32)],
        compiler_params=pltpu.CompilerParams(dimension_semantics=("arbitrary", "arbitrary"),
                                             vmem_limit_bytes=VMEM_LIMIT),
        name="nsa_attention",
    )(qn, kcmp, vcmpt, cfeat, ks0, ks1, vst, kw0, kw1, vwt, onehot, band, mt, qf, gates)


def _mla_kernel(q_ref, k_ref, vt_ref, o_ref, m_ref, acc_ref):
    qi = pl.program_id(2)
    q0 = qi * TQ_MLA
    tpos = q0 + lax.broadcasted_iota(jnp.int32, (1, TQ_MLA), 1)
    kl = lax.broadcasted_iota(jnp.int32, (TK_MLA, 1), 0)
    qs = [q_ref[:, hh * LANES:(hh + 1) * LANES] for hh in range(2)]

    def qk(hh, c, causal):
        k0 = pl.multiple_of(c * TK_MLA, TK_MLA)
        st = _nt_dot(k_ref[0, pl.ds(k0, TK_MLA), hh * LANES:(hh + 1) * LANES], qs[hh])
        if causal:
            st = jnp.where(k0 + kl <= tpos, st, NEG_INF)
        return st, jnp.max(st, axis=0, keepdims=True)

    def absorb(hh, tile, c):
        st, mx = tile
        k0 = pl.multiple_of(c * TK_MLA, TK_MLA)
        m_old = m_ref[hh]
        m_new = jnp.maximum(m_old, mx)
        p = jnp.exp2((st - m_new).astype(BF16))
        pv = _dot(vt_ref[0, hh * LANES:(hh + 1) * LANES, pl.ds(k0, TK_MLA)], p)
        acc_ref[hh] = jnp.exp2(m_old - m_new) * acc_ref[hh] + pv
        m_ref[hh] = m_new

    def steps(chunks, causal=False):
        work = [(hh, c) for c in chunks for hh in range(2)]
        tiles = [qk(hh, c, causal) for hh, c in work[:MLA_DEPTH]]
        for i, (hh, c) in enumerate(work):
            absorb(hh, tiles[i], c)
            if i + MLA_DEPTH < len(work):
                nh, nc = work[i + MLA_DEPTH]
                tiles.append(qk(nh, nc, causal))

    m_ref[...] = jnp.full(m_ref.shape, NEG_INF, F32)
    acc_ref[...] = jnp.zeros(acc_ref.shape, F32)
    c_diag = qi // (TK_MLA // TQ_MLA)
    steps([c_diag], causal=True)

    def body(i, carry):
        steps([MLA_UNROLL * i + j for j in range(MLA_UNROLL)])
        return carry

    n_loop = c_diag // MLA_UNROLL
    lax.fori_loop(0, n_loop, body, 0)
    for j in range(MLA_UNROLL - 1):
        @pl.when(c_diag - n_loop * MLA_UNROLL > j)
        def _(j=j):
            steps([n_loop * MLA_UNROLL + j])

    outs = [acc_ref[hh, 0:MLA_V, :] / acc_ref[hh, MLA_V:MLA_V + 1, :] for hh in range(2)]
    o_ref[...] = jnp.concatenate(outs, axis=0).T.astype(BF16)


def _mla(qm, km3, vmt, B, T):
    nq = T // TQ_MLA
    npair = N_MLA_HEADS // 2
    return pl.pallas_call(
        _mla_kernel,
        grid=(B, npair, nq),
        in_specs=[pl.BlockSpec((TQ_MLA, 2 * LANES), lambda b, h, i: (b * nq + i, h)),
                  pl.BlockSpec((1, T, 2 * LANES), lambda b, h, i: (b, 0, h)),
                  pl.BlockSpec((1, 2 * LANES, T), lambda b, h, i: (b, h, 0))],
        out_specs=pl.BlockSpec((TQ_MLA, LANES), lambda b, h, i: (b * nq + i, h)),
        out_shape=jax.ShapeDtypeStruct((B * T, npair * LANES), BF16),
        scratch_shapes=[pltpu.VMEM((2, 1, TQ_MLA), F32), pltpu.VMEM((2, LANES, TQ_MLA), F32)],
        compiler_params=pltpu.CompilerParams(
            dimension_semantics=("arbitrary", "arbitrary", "arbitrary"),
            vmem_limit_bytes=VMEM_LIMIT,
            ),
        name="mla_attention",
    )(qm, km3, vmt)


def _post_kernel(on_ref, om_ref, x_ref, p_ref, wn_ref, wm_ref, wup_ref, wd_ref,
                 wpg_ref, wp_ref, ln_ref, o_ref):
    mix = _dot(on_ref[...], wn_ref[...]) + _dot(om_ref[...], wm_ref[...])
    x1 = _layer_norm(ALPHA * x_ref[...] + mix, ln_ref[0:1, :], ln_ref[1:2, :])
    xb = x1.astype(BF16)
    ffn = jnp.zeros_like(x1)
    for c in range(D_FF // TF_FFN):
        cols = slice(c * TF_FFN, (c + 1) * TF_FFN)
        up_cols = slice(D_FF + c * TF_FFN, D_FF + (c + 1) * TF_FFN)
        hid = jax.nn.silu(_dot(xb, wup_ref[:, cols])) * _dot(xb, wup_ref[:, up_cols])
        ffn = ffn + _dot(hid.astype(BF16), wd_ref[cols, :])
    x2 = _layer_norm(ALPHA * x1 + ffn, ln_ref[2:3, :], ln_ref[3:4, :])
    gate = jax.nn.sigmoid(_dot(x2.astype(BF16), wpg_ref[...]))
    ple = gate * _dot(p_ref[...].astype(BF16), wp_ref[...])
    o_ref[...] = _layer_norm(ALPHA * x2 + ple, ln_ref[4:5, :], ln_ref[5:6, :])


def _post(o_nsa, o_mla, x2, p2, wn, wm, w_up, w_down, wpg, wp, ln):
    N = x2.shape[0]
    tm = TM_PROJ
    rows = lambda w: pl.BlockSpec((tm, w), lambda i: (i, 0))
    full = lambda a: pl.BlockSpec(a.shape, lambda i: (0,) * a.ndim)
    return pl.pallas_call(
        _post_kernel,
        grid=(N // tm,),
        in_specs=[rows(o_nsa.shape[1]), rows(o_mla.shape[1]), rows(D_MODEL), rows(D_PLE),
                  full(wn), full(wm), full(w_up), full(w_down), full(wpg), full(wp), full(ln)],
        out_specs=rows(D_MODEL),
        out_shape=jax.ShapeDtypeStruct((N, D_MODEL), F32),
        compiler_params=pltpu.CompilerParams(dimension_semantics=("arbitrary",),
                                             vmem_limit_bytes=VMEM_LIMIT),
        name="post_attention",
    )(o_nsa, o_mla, x2, p2, wn, wm, w_up, w_down, wpg, wp, ln)


def _arrange_in_weights(w_in):
    splits = (N_NSA_HEADS * NSA_D,) + (NSA_GROUPS * NSA_D,) * 6 + (3 * N_NSA_HEADS, MLA_Q_LORA,
                                                                  MLA_KV_LORA, MLA_ROPE)
    offs = np.cumsum((0,) + splits)
    part = lambda i: w_in[:, offs[i]:offs[i + 1]]

    def place(w, lane0):
        return jnp.pad(w, ((0, 0), (lane0, LANES - lane0 - w.shape[1])))

    cols = [part(0), part(1), part(2), part(3), part(5), place(part(7), 0), part(8), part(9),
            place(part(10), MLA_NOPE)]
    w_all = jnp.concatenate(cols, axis=1)
    assert w_all.shape[1] == _C_END

    def value_rows(w):
        return jnp.concatenate([place(w[:, g * NSA_D:(g + 1) * NSA_D], 0)
                                for g in range(NSA_GROUPS)], axis=1).T

    return w_all.astype(BF16), value_rows(part(4)).astype(BF16), value_rows(part(6)).astype(BF16)


def _arrange_mla_weights(w_uq, w_ukv):
    wq = w_uq.reshape(MLA_Q_LORA, N_MLA_HEADS, MLA_NOPE + MLA_ROPE)

    def place(w, lane0):
        return jnp.pad(w, ((0, 0), (0, 0), (lane0, LANES - lane0 - w.shape[2])))

    wqa = place(wq, 0)
    wkv = w_ukv.reshape(MLA_KV_LORA, N_MLA_HEADS, MLA_NOPE + MLA_V)
    wka = place(wkv[:, :, :MLA_NOPE], 0)
    wv = place(wkv[:, :, MLA_NOPE:], 0)
    flat = lambda a: a.reshape(a.shape[0], -1).astype(BF16)
    return flat(wqa), flat(wka), flat(wv).T


def _rope_tables(T):
    half = MLA_ROPE // 2
    pos = jnp.arange(T, dtype=F32)
    inv_freq = ROPE_THETA ** (-jnp.arange(half, dtype=F32) / half)
    ang = pos[:, None] * inv_freq[None, :]
    cos, sin = jnp.cos(ang), jnp.sin(ang)
    scale = (MLA_NOPE + MLA_ROPE) ** -0.5 * LOG2_E
    ones = jnp.ones((T, MLA_NOPE), F32)
    tail = jnp.zeros((T, LANES - MLA_NOPE - MLA_ROPE), F32)
    cos_row = lambda head: jnp.concatenate([head, cos, cos, tail], axis=1)
    sin_row = jnp.concatenate([0.0 * ones, -sin, sin, tail], axis=1)
    return cos_row(ones) * scale, sin_row * scale, cos_row(0.0 * ones), sin_row


def _compress_weights(w1, w2, pos):
    G, D = NSA_GROUPS, NSA_D
    w1r = w1.reshape(CMP_LEN, D, CMP_HIDDEN)
    eye = jnp.eye(G, dtype=bool)
    halves = []
    for j in range(CMP_LEN // CMP_STRIDE):
        part = w1r[j * CMP_STRIDE:(j + 1) * CMP_STRIDE]
        wide = jnp.where(eye[None, :, None, :, None], part[:, None, :, None, :], 0.0)
        halves.append(wide.reshape(CMP_STRIDE * G * D, G * CMP_HIDDEN).astype(BF16))
    w2bd = jnp.where(eye[:, None, :, None], w2[None, :, None, :], 0.0)
    w2bd = w2bd.reshape(G * CMP_HIDDEN, G * D).astype(BF16)
    posr = pos.reshape(CMP_LEN // CMP_STRIDE, CMP_STRIDE, 1, D)
    posw = jnp.broadcast_to(posr, (CMP_LEN // CMP_STRIDE, CMP_STRIDE, G, D)).reshape(
        CMP_LEN // CMP_STRIDE, CMP_STRIDE * G * D)
    return halves[0], halves[1], w2bd, posw


def _position_features(pos):
    return np.stack([(pos // LANES) * LANES, pos % LANES, np.ones_like(pos), np.ones_like(pos)],
                    axis=1).astype(np.float32)


def _nsa_tables(T):
    kpos = np.arange(T)
    onehot = np.zeros((T, LANES), np.float32)
    onehot[kpos, kpos // SLC_BLOCK] = 1.0
    key_feats, pad_rows = [], []
    for g in range(NSA_GROUPS):
        base = _feat_base(g)
        ft = np.zeros((T, LANES), np.float32)
        ft[:, base:base + 4] = _position_features(kpos)
        key_feats.append(jnp.asarray(ft))
        pad = np.zeros((WINDOW, LANES), np.float32)
        pad[:, base + 4] = 1.0
        pad_rows.append(jnp.asarray(pad, BF16))
    nsub = T // CMP_STRIDE
    c = np.arange(nsub)
    cfeat = np.zeros((nsub, LANES), np.float32)
    cfeat[:, 0:4] = _position_features(c * CMP_STRIDE + CMP_LEN - 1)
    kl = np.arange(WIN_KEYS)[:, None]
    tl = np.arange(TQ_NSA)[None, :]
    band = np.where((kl > tl) & (kl <= tl + WINDOW), 0.0, NEG_INF).astype(np.float32)
    band = np.tile(band, (1, NSA_HPG))
    ratio = SLC_BLOCK // CMP_STRIDE
    mt = np.zeros((LANES, nsub), np.float32)
    valid = c < nsub - 1
    for j in range(CMP_LEN // CMP_STRIDE):
        np.add.at(mt, ((c[valid] + j) // ratio, c[valid]), 1.0)
    qf = np.zeros((NSA_GROUPS, 4, R_NSA, LANES), np.float32)
    tl_rows = np.tile(np.arange(TQ_NSA), NSA_HPG).astype(np.float32)
    for g in range(NSA_GROUPS):
        slope = np.repeat(2.0 ** -(g * NSA_HPG + np.arange(NSA_HPG) + 1.0), TQ_NSA).astype(np.float32)
        for k, base in enumerate((_feat_base(g), 0)):
            qf[g, 2 * k, :, base] = slope
            qf[g, 2 * k, :, base + 1] = slope
            qf[g, 2 * k, :, base + 3] = -slope * tl_rows
            qf[g, 2 * k, :, base + 4] = NEG_INF
            qf[g, 2 * k + 1, :, base + 2] = -slope * TQ_NSA
    return (jnp.asarray(onehot, BF16), key_feats, pad_rows, jnp.asarray(cfeat, BF16),
            jnp.asarray(band), jnp.asarray(mt, BF16), jnp.asarray(qf))


def kernel(x, p, w_in, w_ck1, w_ck2, pos_ck, w_cv1, w_cv2, pos_cv, mla_q_norm, w_uq, mla_kv_norm,
           w_ukv, w_out, ln1_g, ln1_b, w_up, w_down, ln2_g, ln2_b, w_ple_gate, w_ple, ln3_g, ln3_b):
    B, T, _ = x.shape
    N = B * T
    assert T % TQ_MLA == 0 and T % TM_PROJ == 0 and T // SLC_BLOCK <= LANES
    cq_t, sq_t, ck_t, sk_t = _rope_tables(T)
    onehot, key_feats, pad_rows, cfeat, band, mt, qf = _nsa_tables(T)
    row2 = lambda v: v.reshape(1, -1)
    xc = x.reshape(N, D_MODEL)
    for i in range(DEPTH):
        w_all, wvst, wvwt = _arrange_in_weights(w_in[i])
        wqa, wka, wv = _arrange_mla_weights(w_uq[i], w_ukv[i])
        (qn, kc, vc, ks0, ks1, vst, kw0, kw1, vwt, gates, qm, km, vmt) = _in_proj(
            xc, w_all, wvst, wvwt, wqa, wka, wv, row2(mla_q_norm[i]), row2(mla_kv_norm[i]),
            key_feats[0], key_feats[1], cq_t, sq_t, ck_t, sk_t, B, T)
        wk_lo, wk_hi, wk2, pk = _compress_weights(w_ck1[i], w_ck2[i], pos_ck[i])
        wv_lo, wv_hi, wv2, pv = _compress_weights(w_cv1[i], w_cv2[i], pos_cv[i])
        nsub = T // CMP_STRIDE
        kcmp, vcmpt = _compress(kc.reshape(B, nsub, CMP_STRIDE * LANES),
                                vc.reshape(B, nsub, CMP_STRIDE * LANES),
                                pk, pv, wk_lo, wk_hi, wk2, wv_lo, wv_hi, wv2.T)
        pad_k = lambda a, g: jnp.concatenate(
            [jnp.broadcast_to(pad_rows[g], (B, WINDOW, LANES)), a.reshape(B, T, LANES)], axis=1)
        o_nsa = _nsa(qn, kcmp, vcmpt, cfeat, ks0.reshape(B, T, LANES), ks1.reshape(B, T, LANES), vst,
                     pad_k(kw0, 0), pad_k(kw1, 1), jnp.pad(vwt, ((0, 0), (0, 0), (WINDOW, 0))),
                     onehot, band, mt, qf, gates, B, T)
        o_mla = _mla(qm, km.reshape(B, T, -1), vmt, B, T)
        wo = w_out[i]
        nsa_w = N_NSA_HEADS * NSA_D
        wn = wo[:nsa_w].reshape(NSA_GROUPS, NSA_HPG, NSA_D, D_MODEL).transpose(1, 0, 2, 3)
        wn = wn.reshape(nsa_w, D_MODEL).astype(BF16)
        wm = wo[nsa_w:].astype(BF16)
        ln = jnp.stack([ln1_g[i], ln1_b[i], ln2_g[i], ln2_b[i], ln3_g[i], ln3_b[i]])
        xc = _post(o_nsa, o_mla, xc, p[i].reshape(N, D_PLE), wn, wm, w_up[i].astype(BF16),
                   w_down[i].astype(BF16), w_ple_gate[i].astype(BF16), w_ple[i].astype(BF16), ln)
    return xc.reshape(B, T, D_MODEL)
```

```python
import functools

import jax
import jax.numpy as jnp
import numpy as np
from jax import lax
from jax.experimental import pallas as pl
from jax.experimental.pallas import tpu as pltpu

F32 = jnp.float32
BF16 = jnp.bfloat16

D_MODEL = 1024
N_NSA_HEADS = 8
NSA_GROUPS = 2
NSA_HPG = N_NSA_HEADS // NSA_GROUPS
NSA_D = 64
CMP_LEN = 32
CMP_STRIDE = 16
CMP_HIDDEN = 256
SLC_BLOCK = 64
SLC_TOP_N = 16
WINDOW = 512
N_MLA_HEADS = 8
MLA_NOPE = 64
MLA_ROPE = 32
MLA_V = 64
MLA_Q_LORA = 256
MLA_KV_LORA = 128
ROPE_THETA = 10000.0
D_FF = -(-8 * D_MODEL // (3 * 256)) * 256
D_PLE = 256
DEPTH = 1
ALPHA = (2 * DEPTH) ** 0.25
LN_EPS = 1e-5
RMS_EPS = 1e-6
NEG_INF = -1e30
FORCE_SCORE = 1e6
REMOVED_SCORE = -3e38
LOG2_E = 1.4426950408889634

LANES = 128
SUBLANES = 8
V_ROWS = 80
TQ_NSA = 256
R_NSA = NSA_HPG * TQ_NSA
TK_SLC = 512
BLOCKS_PER_CHUNK = TK_SLC // SLC_BLOCK
N_SLC_CHUNKS = LANES // BLOCKS_PER_CHUNK
SLC_UNROLL = 2
WIN_KEYS = WINDOW + TQ_NSA
N_FEAT = 5
TQ_MLA = 512
TK_MLA = 512
MLA_UNROLL = 4
MLA_DEPTH = 2
TM_PROJ = 512
TF_FFN = 256
VMEM_LIMIT = 56 * 1024 * 1024

_C_NQ = 0
_C_KC = _C_NQ + N_NSA_HEADS * NSA_D
_C_VC = _C_KC + LANES
_C_KS = _C_VC + LANES
_C_KW = _C_KS + LANES
_C_GL = _C_KW + LANES
_C_CQ = _C_GL + LANES
_C_CKV = _C_CQ + MLA_Q_LORA
_C_KPA = _C_CKV + MLA_KV_LORA
_C_END = _C_KPA + LANES


def _feat_base(g):
    return NSA_D * (1 - g)


def _nt_dot(a, b):
    return lax.dot_general(a, b, (((1,), (1,)), ((), ())), preferred_element_type=F32)


def _dot(a, b):
    return jnp.dot(a, b, preferred_element_type=F32)


def _layer_norm(v, g, b):
    mu = jnp.mean(v, axis=-1, keepdims=True)
    d = v - mu
    var = jnp.mean(d * d, axis=-1, keepdims=True)
    return d * lax.rsqrt(var + LN_EPS) * g + b


def _rms_norm(v, g):
    return v * lax.rsqrt(jnp.mean(v * v, axis=-1, keepdims=True) + RMS_EPS) * g


def _in_proj_kernel(x_ref, w_ref, wvst_ref, wvwt_ref, wqa_ref, wka_ref, wv_ref,
                    gq_ref, gkv_ref, ft0_ref, ft1_ref, cq_ref, sq_ref, ck_ref, sk_ref,
                    qn_ref, kc_ref, vc_ref, ks0_ref, ks1_ref, vst_ref, kw0_ref, kw1_ref, vwt_ref,
                    gate_ref, qm_ref, km_ref, vm_ref):
    xb = x_ref[...].astype(BF16)
    h = _dot(xb, w_ref[...])
    tm = h.shape[0]
    lane = lax.broadcasted_iota(jnp.int32, (tm, LANES), 1)
    in_half = [lane < NSA_D, lane >= NSA_D]

    def seg(c0, width):
        return h[:, c0:c0 + width]

    def swap_rotary(v):
        half = MLA_ROPE // 2
        first = (lane >= MLA_NOPE) & (lane < MLA_NOPE + half)
        second = (lane >= MLA_NOPE + half) & (lane < MLA_NOPE + MLA_ROPE)
        return jnp.where(first, pltpu.roll(v, LANES - half, 1),
                         jnp.where(second, pltpu.roll(v, half, 1), 0.0))

    for i in range(N_NSA_HEADS):
        g = i // NSA_HPG
        src = seg(_C_NQ + (i // 2) * LANES, LANES)
        if i % 2 != g:
            src = pltpu.roll(src, NSA_D, 1)
        qn_ref[0, i] = (jnp.where(in_half[g], src, 0.0) * (NSA_D ** -0.5)).astype(BF16)
    kc_ref[...] = seg(_C_KC, LANES)
    vc_ref[...] = seg(_C_VC, LANES)
    h_ks, h_kw = seg(_C_KS, LANES), seg(_C_KW, LANES)
    ks0_ref[...] = (jnp.where(in_half[0], h_ks, 0.0) + ft0_ref[...]).astype(BF16)
    ks1_ref[...] = (jnp.where(in_half[1], h_ks, 0.0) + ft1_ref[...]).astype(BF16)
    kw0_ref[...] = (jnp.where(in_half[0], h_kw, 0.0) + ft0_ref[...]).astype(BF16)
    kw1_ref[...] = (jnp.where(in_half[1], h_kw, 0.0) + ft1_ref[...]).astype(BF16)
    def ones_rows(n):
        r = lax.broadcasted_iota(jnp.int32, (n, 1), 0) & (LANES - 1)
        return jnp.where(r == NSA_D, 1.0, 0.0)

    vst_ref[0] = (_nt_dot(wvst_ref[...], xb) + ones_rows(NSA_GROUPS * LANES)).astype(BF16)
    vwt_ref[0] = (_nt_dot(wvwt_ref[...], xb) + ones_rows(NSA_GROUPS * LANES)).astype(BF16)
    gate_ref[...] = jax.nn.sigmoid(seg(_C_GL, LANES))

    cqn = _rms_norm(seg(_C_CQ, MLA_Q_LORA), gq_ref[...]).astype(BF16)
    cq_t = cq_ref[...]
    sq_t = sq_ref[...]
    qa = _dot(cqn, wqa_ref[...])
    for hd in range(N_MLA_HEADS):
        hs = slice(hd * LANES, (hd + 1) * LANES)
        qm_ref[:, hs] = (qa[:, hs] * cq_t + swap_rotary(qa[:, hs]) * sq_t).astype(BF16)

    kvn = _rms_norm(seg(_C_CKV, MLA_KV_LORA), gkv_ref[...]).astype(BF16)
    kpa = seg(_C_KPA, LANES)
    kpe = kpa * ck_ref[...] + swap_rotary(kpa) * sk_ref[...]
    ka = _dot(kvn, wka_ref[...])
    for hd in range(N_MLA_HEADS):
        hs = slice(hd * LANES, (hd + 1) * LANES)
        km_ref[:, hs] = (ka[:, hs] + kpe).astype(BF16)
    vm_ref[0] = (_nt_dot(wv_ref[...], kvn) + ones_rows(N_MLA_HEADS * LANES)).astype(BF16)


def _in_proj(x2, w_all, wvst, wvwt, wqa, wka, wv, gq, gkv, ft0, ft1, cq_t, sq_t, ck_t, sk_t,
             B, T):
    N = B * T
    tm = TM_PROJ
    tpb = T // tm
    full = lambda a: pl.BlockSpec(a.shape, lambda i: (0,) * a.ndim)
    rows = lambda w: pl.BlockSpec((tm, w), lambda i: (i, 0))
    tab = pl.BlockSpec((tm, LANES), lambda i: (i % tpb, 0))
    cols_t = lambda w: pl.BlockSpec((1, w, tm), lambda i: (i // tpb, 0, i % tpb))
    tok = lambda dt: jax.ShapeDtypeStruct((N, LANES), dt)
    out_shape = (
        jax.ShapeDtypeStruct((B, N_NSA_HEADS, T, LANES), BF16),
        tok(F32), tok(F32),
        tok(BF16), tok(BF16),
        jax.ShapeDtypeStruct((B, NSA_GROUPS * LANES, T), BF16),
        tok(BF16), tok(BF16),
        jax.ShapeDtypeStruct((B, NSA_GROUPS * LANES, T), BF16),
        tok(F32),
        jax.ShapeDtypeStruct((N, N_MLA_HEADS * LANES), BF16),
        jax.ShapeDtypeStruct((N, N_MLA_HEADS * LANES), BF16),
        jax.ShapeDtypeStruct((B, N_MLA_HEADS * LANES, T), BF16),
    )
    out_specs = (
        pl.BlockSpec((1, N_NSA_HEADS, tm, LANES), lambda i: (i // tpb, 0, i % tpb, 0)),
        rows(LANES), rows(LANES), rows(LANES), rows(LANES), cols_t(NSA_GROUPS * LANES),
        rows(LANES), rows(LANES), cols_t(NSA_GROUPS * LANES), rows(LANES),
        rows(N_MLA_HEADS * LANES), rows(N_MLA_HEADS * LANES), cols_t(N_MLA_HEADS * LANES),
    )
    return pl.pallas_call(
        _in_proj_kernel,
        grid=(N // tm,),
        in_specs=[rows(D_MODEL), full(w_all), full(wvst), full(wvwt), full(wqa),
                  full(wka), full(wv), full(gq), full(gkv), tab, tab, tab, tab, tab, tab],
        out_specs=out_specs,
        out_shape=out_shape,
        compiler_params=pltpu.CompilerParams(dimension_semantics=("arbitrary",),
                                             vmem_limit_bytes=VMEM_LIMIT),
        name="in_proj",
    )(x2, w_all, wvst, wvwt, wqa, wka, wv, gq, gkv, ft0, ft1, cq_t, sq_t, ck_t, sk_t)


def _compress_kernel(kc_ref, vc_ref, pk_ref, pv_ref, wk_lo_ref, wk_hi_ref, wk2_ref,
                     wv_lo_ref, wv_hi_ref, wv2t_ref, ko_ref, vo_ref):
    nsub = kc_ref.shape[1]

    def hidden(src_ref, pos_ref, lo_ref, hi_ref):
        a = src_ref[0]
        pos = pos_ref[...]
        p0 = _dot((a + pos[0:1]).astype(BF16), lo_ref[...])
        p1 = _dot((a + pos[1:2]).astype(BF16), hi_ref[...])
        hid = p0 + pltpu.roll(p1, nsub - 1, 0)
        return jax.nn.gelu(hid).astype(BF16)

    k_out = _dot(hidden(kc_ref, pk_ref, wk_lo_ref, wk_hi_ref), wk2_ref[...])
    row = lax.broadcasted_iota(jnp.int32, k_out.shape, 0)
    ko_ref[0] = jnp.where(row < nsub - 1, k_out, 0.0).astype(BF16)
    v_out = _nt_dot(wv2t_ref[...], hidden(vc_ref, pv_ref, wv_lo_ref, wv_hi_ref))
    col = lax.broadcasted_iota(jnp.int32, v_out.shape, 1)
    vo_ref[0] = jnp.where(col < nsub - 1, v_out, 0.0).astype(BF16)


def _compress(kc3, vc3, pk, pv, wk_lo, wk_hi, wk2, wv_lo, wv_hi, wv2t):
    B, nsub, width = kc3.shape
    full = lambda a: pl.BlockSpec(a.shape, lambda b: (0,) * a.ndim)
    per_b = pl.BlockSpec((1, nsub, width), lambda b: (b, 0, 0))
    return pl.pallas_call(
        _compress_kernel,
        grid=(B,),
        in_specs=[per_b, per_b, full(pk), full(pv), full(wk_lo), full(wk_hi), full(wk2),
                  full(wv_lo), full(wv_hi), full(wv2t)],
        out_specs=(pl.BlockSpec((1, nsub, LANES), lambda b: (b, 0, 0)),
                   pl.BlockSpec((1, LANES, nsub), lambda b: (b, 0, 0))),
        out_shape=(jax.ShapeDtypeStruct((B, nsub, LANES), BF16),
                   jax.ShapeDtypeStruct((B, LANES, nsub), BF16)),
        compiler_params=pltpu.CompilerParams(dimension_semantics=("arbitrary",),
                                             vmem_limit_bytes=VMEM_LIMIT),
        name="nsa_compress",
    )(kc3, vc3, pk, pv, wk_lo, wk_hi, wk2, wv_lo, wv_hi, wv2t)


def _nsa_kernel(qn_ref, kcmp_ref, vcmpt_ref, cfeat_ref, ks0_ref, ks1_ref, vst_ref,
                kw0_ref, kw1_ref, vwt_ref, onehot_ref, band_ref, mt_ref, qf_ref, gate_ref,
                o_ref, qaug_ref, lst_ref, m_ref, acc_ref, *, n_top):
    qb = pl.program_id(1)
    q0 = qb * TQ_NSA
    R = R_NSA
    ncmp = kcmp_ref.shape[1]

    col = lax.broadcasted_iota(jnp.int32, (1, R), 1)
    t_row = q0 + (col & (TQ_NSA - 1))
    gates_t = gate_ref[...].T
    c_diag = qb // (TK_SLC // TQ_NSA)
    kl_col = lax.broadcasted_iota(jnp.int32, (TK_SLC, 1), 0)
    w0 = pl.multiple_of(q0, TQ_NSA)
    qb_f = qb.astype(F32)

    q_feats, o_wins, o_cmps, imps = [], [], [], []
    for g in range(NSA_GROUPS):
        kw_ref = (kw0_ref, kw1_ref)[g]
        q = qn_ref[0, g * NSA_HPG:(g + 1) * NSA_HPG].reshape(R, LANES)
        q_feat = (q.astype(F32) + qf_ref[g, 0] + qb_f * qf_ref[g, 1]).astype(BF16)
        q_feats.append(q_feat)

        v_rows = slice(g * LANES, g * LANES + V_ROWS)

        sw = _nt_dot(kw_ref[0, pl.ds(w0, WIN_KEYS), :], q_feat) + band_ref[...]
        pw = jnp.exp((sw - jnp.max(sw, axis=0, keepdims=True)).astype(BF16))
        o_win = _dot(vwt_ref[0, v_rows, pl.ds(w0, WIN_KEYS)], pw)
        o_wins.append(o_win[0:NSA_D, :] / o_win[NSA_D:NSA_D + 1, :])

        kc_aug = jnp.concatenate([kcmp_ref[0], cfeat_ref[...]], axis=1)
        q_cmp = jnp.concatenate([q, (qf_ref[g, 2] + qb_f * qf_ref[g, 3]).astype(BF16)], axis=1)
        cend = lax.broadcasted_iota(jnp.int32, (ncmp, 1), 0) * CMP_STRIDE + (CMP_LEN - 1)
        cmask = cend <= t_row
        sc = jnp.where(cmask, _nt_dot(kc_aug, q_cmp), NEG_INF)
        e = jnp.exp(sc - jnp.maximum(jnp.max(sc, axis=0, keepdims=True), 0.1 * NEG_INF))
        den = jnp.sum(e, axis=0, keepdims=True)
        p_cmp = e * jnp.where(den > 0.0, 1.0 / den, 0.0)
        o_cmps.append(_dot(vcmpt_ref[0, g * NSA_D:(g + 1) * NSA_D, :], p_cmp.astype(BF16)))

        p4 = p_cmp[:, 0:TQ_NSA]
        for h in range(1, NSA_HPG):
            p4 = p4 + p_cmp[:, h * TQ_NSA:(h + 1) * TQ_NSA]
        hi = p4.astype(BF16)
        r1 = p4 - hi.astype(F32)
        mid = r1.astype(BF16)
        lo = (r1 - mid.astype(F32)).astype(BF16)
        mt = mt_ref[...]
        imps.append(_dot(mt, hi) + _dot(mt, mid) + _dot(mt, lo))

    imp = jnp.concatenate(imps, axis=1)
    width = NSA_GROUPS * TQ_NSA
    jb = lax.broadcasted_iota(jnp.int32, (LANES, width), 0)
    tl2 = lax.broadcasted_iota(jnp.int32, (LANES, width), 1) & (TQ_NSA - 1)
    cur = (q0 + tl2) >> 6
    forced = (jb == 0) | (jb == cur) | (jb == cur - 1)
    future = jb > cur
    jf = jb.astype(F32)
    rest = jnp.where(forced, REMOVED_SCORE, jnp.where(future, -1.0, imp))
    picked = forced
    for _ in range(max(n_top - 3, 0)):
        mx = jnp.max(rest, axis=0, keepdims=True)
        first = jnp.min(jnp.where(rest == mx, jf, float(LANES)), axis=0, keepdims=True)
        hit = jf == first
        picked = picked | hit
        rest = jnp.where(hit, REMOVED_SCORE, rest)
    allowed2 = jnp.where(picked & jnp.logical_not(future), 1.0, 0.0)

    group_out = []
    for g in range(NSA_GROUPS):
        ks_ref = (ks0_ref, ks1_ref)[g]
        v_rows = slice(g * LANES, g * LANES + V_ROWS)
        o_cmp, o_win = o_cmps[g], o_wins[g]
        allowed = allowed2[:, g * TQ_NSA:(g + 1) * TQ_NSA]
        selneg_t = jnp.where(allowed > 0.0, 0.0, NEG_INF).T.astype(BF16)

        qaug_ref[:, 0:LANES] = q_feats[g]
        for h in range(NSA_HPG):
            qaug_ref[h * TQ_NSA:(h + 1) * TQ_NSA, LANES:2 * LANES] = selneg_t

        count = jnp.int32(0)
        for c in range(N_SLC_CHUNKS):
            used = jnp.max(allowed[c * BLOCKS_PER_CHUNK:(c + 1) * BLOCKS_PER_CHUNK, :]) > 0.0
            lst_ref[count] = jnp.int32(c)
            count = count + (used & (c < c_diag)).astype(jnp.int32)

        def slc_scores(c, causal, ks_ref=ks_ref):
            k0 = pl.multiple_of(c * TK_SLC, TK_SLC)
            k_aug = jnp.concatenate([ks_ref[0, pl.ds(k0, TK_SLC), :],
                                     onehot_ref[pl.ds(k0, TK_SLC), :]], axis=1)
            st = _nt_dot(k_aug, qaug_ref[...])
            if causal:
                st = jnp.where(k0 + kl_col <= t_row, st, NEG_INF)
            return st, jnp.max(st, axis=0, keepdims=True)

        def absorb(tile, c, v_rows=v_rows):
            st, mx = tile
            k0 = pl.multiple_of(c * TK_SLC, TK_SLC)
            m_old = m_ref[...]
            m_new = jnp.maximum(m_old, mx)
            p = jnp.exp((st - m_new).astype(BF16))
            pv = _dot(vst_ref[0, v_rows, pl.ds(k0, TK_SLC)], p)
            acc_ref[...] = jnp.exp(m_old - m_new) * acc_ref[...] + pv
            m_ref[...] = m_new

        def steps(chunks, causal=False, slc_scores=slc_scores, absorb=absorb):
            tiles = [slc_scores(c, causal) for c in chunks]
            for c, tile in zip(chunks, tiles):
                absorb(tile, c)

        m_ref[...] = jnp.full((1, R), NEG_INF, F32)
        acc_ref[...] = jnp.zeros((V_ROWS, R), F32)
        steps([c_diag], causal=True)

        def body(i, carry, steps=steps):
            steps([lst_ref[SLC_UNROLL * i + j] for j in range(SLC_UNROLL)])
            return carry

        n_loop = count // SLC_UNROLL
        lax.fori_loop(0, n_loop, body, 0)
        for j in range(SLC_UNROLL - 1):
            @pl.when(count - n_loop * SLC_UNROLL > j)
            def _(j=j, steps=steps):
                steps([lst_ref[n_loop * SLC_UNROLL + j]])
        o_slc = acc_ref[0:NSA_D, :] / acc_ref[NSA_D:NSA_D + 1, :]

        def gate_row(branch):
            rows = [gates_t[(g * NSA_HPG + h) * 3 + branch:(g * NSA_HPG + h) * 3 + branch + 1, :]
                    for h in range(NSA_HPG)]
            return jnp.concatenate(rows, axis=1)

        group_out.append(gate_row(0) * o_cmp + gate_row(1) * o_slc + gate_row(2) * o_win)

    out_t = jnp.concatenate(group_out, axis=0)
    for h in range(NSA_HPG):
        o_ref[:, h * LANES:(h + 1) * LANES] = out_t[:, h * TQ_NSA:(h + 1) * TQ_NSA].T.astype(BF16)


def _nsa(qn, kcmp, vcmpt, cfeat, ks0, ks1, vst, kw0, kw1, vwt, onehot, band, mt, qf, gates, B, T):
    nq = T // TQ_NSA
    n_top = min(SLC_TOP_N, T // SLC_BLOCK)
    per_b = lambda a: pl.BlockSpec((1,) + a.shape[1:], lambda b, i: (b,) + (0,) * (a.ndim - 1),
                                   pipeline_mode=pl.Buffered(1))
    full = lambda a: pl.BlockSpec(a.shape, lambda b, i: (0,) * a.ndim,
                                  pipeline_mode=pl.Buffered(1))
    return pl.pallas_call(
        functools.partial(_nsa_kernel, n_top=n_top),
        grid=(B, nq),
        in_specs=[pl.BlockSpec((1, N_NSA_HEADS, TQ_NSA, LANES), lambda b, i: (b, 0, i, 0)),
                  per_b(kcmp), per_b(vcmpt), full(cfeat), per_b(ks0), per_b(ks1), per_b(vst),
                  per_b(kw0), per_b(kw1), per_b(vwt), full(onehot), full(band), full(mt), full(qf),
                  pl.BlockSpec((TQ_NSA, LANES), lambda b, i: (b * nq + i, 0))],
        out_specs=pl.BlockSpec((TQ_NSA, NSA_HPG * LANES), lambda b, i: (b * nq + i, 0)),
        out_shape=jax.ShapeDtypeStruct((B * T, NSA_HPG * LANES), BF16),
        scratch_shapes=[pltpu.VMEM((R_NSA, 2 * LANES), BF16),
                        pltpu.SMEM((N_SLC_CHUNKS + 1,), jnp.int32),
                        pltpu.VMEM((1, R_NSA), F32), pltpu.VMEM((V_ROWS, R_NSA), F32)],
        compiler_params=pltpu.CompilerParams(dimension_semantics=("arbitrary", "arbitrary"),
                                             vmem_limit_bytes=VMEM_LIMIT),
        name="nsa_attention",
    )(qn, kcmp, vcmpt, cfeat, ks0, ks1, vst, kw0, kw1, vwt, onehot, band, mt, qf, gates)


def _mla_kernel(q_ref, k_ref, vt_ref, o_ref, m_ref, acc_ref):
    qi = pl.program_id(2)
    q0 = qi * TQ_MLA
    tpos = q0 + lax.broadcasted_iota(jnp.int32, (1, TQ_MLA), 1)
    kl = lax.broadcasted_iota(jnp.int32, (TK_MLA, 1), 0)
    qs = [q_ref[:, hh * LANES:(hh + 1) * LANES] for hh in range(2)]

    def qk(hh, c, causal):
        k0 = pl.multiple_of(c * TK_MLA, TK_MLA)
        st = _nt_dot(k_ref[0, pl.ds(k0, TK_MLA), hh * LANES:(hh + 1) * LANES], qs[hh])
        if causal:
            st = jnp.where(k0 + kl <= tpos, st, NEG_INF)
        return st, jnp.max(st, axis=0, keepdims=True)

    def absorb(hh, tile, c):
        st, mx = tile
        k0 = pl.multiple_of(c * TK_MLA, TK_MLA)
        m_old = m_ref[hh]
        m_new = jnp.maximum(m_old, mx)
        p = jnp.exp2((st - m_new).astype(BF16))
        pv = _dot(vt_ref[0, hh * LANES:(hh + 1) * LANES, pl.ds(k0, TK_MLA)], p)
        acc_ref[hh] = jnp.exp2(m_old - m_new) * acc_ref[hh] + pv
        m_ref[hh] = m_new

    def steps(chunks, causal=False):
        work = [(hh, c) for c in chunks for hh in range(2)]
        tiles = [qk(hh, c, causal) for hh, c in work[:MLA_DEPTH]]
        for i, (hh, c) in enumerate(work):
            absorb(hh, tiles[i], c)
            if i + MLA_DEPTH < len(work):
                nh, nc = work[i + MLA_DEPTH]
                tiles.append(qk(nh, nc, causal))

    m_ref[...] = jnp.full(m_ref.shape, NEG_INF, F32)
    acc_ref[...] = jnp.zeros(acc_ref.shape, F32)
    c_diag = qi // (TK_MLA // TQ_MLA)
    steps([c_diag], causal=True)

    def body(i, carry):
        steps([MLA_UNROLL * i + j for j in range(MLA_UNROLL)])
        return carry

    n_loop = c_diag // MLA_UNROLL
    lax.fori_loop(0, n_loop, body, 0)
    for j in range(MLA_UNROLL - 1):
        @pl.when(c_diag - n_loop * MLA_UNROLL > j)
        def _(j=j):
            steps([n_loop * MLA_UNROLL + j])

    outs = [acc_ref[hh, 0:MLA_V, :] / acc_ref[hh, MLA_V:MLA_V + 1, :] for hh in range(2)]
    o_ref[...] = jnp.concatenate(outs, axis=0).T.astype(BF16)


def _mla(qm, km3, vmt, B, T):
    nq = T // TQ_MLA
    npair = N_MLA_HEADS // 2
    return pl.pallas_call(
        _mla_kernel,
        grid=(B, npair, nq),
        in_specs=[pl.BlockSpec((TQ_MLA, 2 * LANES), lambda b, h, i: (b * nq + i, h)),
                  pl.BlockSpec((1, T, 2 * LANES), lambda b, h, i: (b, 0, h)),
                  pl.BlockSpec((1, 2 * LANES, T), lambda b, h, i: (b, h, 0))],
        out_specs=pl.BlockSpec((TQ_MLA, LANES), lambda b, h, i: (b * nq + i, h)),
        out_shape=jax.ShapeDtypeStruct((B * T, npair * LANES), BF16),
        scratch_shapes=[pltpu.VMEM((2, 1, TQ_MLA), F32), pltpu.VMEM((2, LANES, TQ_MLA), F32)],
        compiler_params=pltpu.CompilerParams(
            dimension_semantics=("arbitrary", "arbitrary", "arbitrary"),
            vmem_limit_bytes=VMEM_LIMIT,
            ),
        name="mla_attention",
    )(qm, km3, vmt)


def _post_kernel(on_ref, om_ref, x_ref, p_ref, wn_ref, wm_ref, wup_ref, wd_ref,
                 wpg_ref, wp_ref, ln_ref, o_ref):
    mix = _dot(on_ref[...], wn_ref[...]) + _dot(om_ref[...], wm_ref[...])
    x1 = _layer_norm(ALPHA * x_ref[...] + mix, ln_ref[0:1, :], ln_ref[1:2, :])
    xb = x1.astype(BF16)
    ffn = jnp.zeros_like(x1)
    for c in range(D_FF // TF_FFN):
        cols = slice(c * TF_FFN, (c + 1) * TF_FFN)
        up_cols = slice(D_FF + c * TF_FFN, D_FF + (c + 1) * TF_FFN)
        hid = jax.nn.silu(_dot(xb, wup_ref[:, cols])) * _dot(xb, wup_ref[:, up_cols])
        ffn = ffn + _dot(hid.astype(BF16), wd_ref[cols, :])
    x2 = _layer_norm(ALPHA * x1 + ffn, ln_ref[2:3, :], ln_ref[3:4, :])
    gate = jax.nn.sigmoid(_dot(x2.astype(BF16), wpg_ref[...]))
    ple = gate * _dot(p_ref[...].astype(BF16), wp_ref[...])
    o_ref[...] = _layer_norm(ALPHA * x2 + ple, ln_ref[4:5, :], ln_ref[5:6, :])


def _post(o_nsa, o_mla, x2, p2, wn, wm, w_up, w_down, wpg, wp, ln):
    N = x2.shape[0]
    tm = TM_PROJ
    rows = lambda w: pl.BlockSpec((tm, w), lambda i: (i, 0))
    full = lambda a: pl.BlockSpec(a.shape, lambda i: (0,) * a.ndim)
    return pl.pallas_call(
        _post_kernel,
        grid=(N // tm,),
        in_specs=[rows(o_nsa.shape[1]), rows(o_mla.shape[1]), rows(D_MODEL), rows(D_PLE),
                  full(wn), full(wm), full(w_up), full(w_down), full(wpg), full(wp), full(ln)],
        out_specs=rows(D_MODEL),
        out_shape=jax.ShapeDtypeStruct((N, D_MODEL), F32),
        compiler_params=pltpu.CompilerParams(dimension_semantics=("arbitrary",),
                                             vmem_limit_bytes=VMEM_LIMIT),
        name="post_attention",
    )(o_nsa, o_mla, x2, p2, wn, wm, w_up, w_down, wpg, wp, ln)


def _arrange_in_weights(w_in):
    splits = (N_NSA_HEADS * NSA_D,) + (NSA_GROUPS * NSA_D,) * 6 + (3 * N_NSA_HEADS, MLA_Q_LORA,
                                                                  MLA_KV_LORA, MLA_ROPE)
    offs = np.cumsum((0,) + splits)
    part = lambda i: w_in[:, offs[i]:offs[i + 1]]

    def place(w, lane0):
        return jnp.pad(w, ((0, 0), (lane0, LANES - lane0 - w.shape[1])))

    cols = [part(0), part(1), part(2), part(3), part(5), place(part(7), 0), part(8), part(9),
            place(part(10), MLA_NOPE)]
    w_all = jnp.concatenate(cols, axis=1)
    assert w_all.shape[1] == _C_END

    def value_rows(w):
        return jnp.concatenate([place(w[:, g * NSA_D:(g + 1) * NSA_D], 0)
                                for g in range(NSA_GROUPS)], axis=1).T

    return w_all.astype(BF16), value_rows(part(4)).astype(BF16), value_rows(part(6)).astype(BF16)


def _arrange_mla_weights(w_uq, w_ukv):
    wq = w_uq.reshape(MLA_Q_LORA, N_MLA_HEADS, MLA_NOPE + MLA_ROPE)

    def place(w, lane0):
        return jnp.pad(w, ((0, 0), (0, 0), (lane0, LANES - lane0 - w.shape[2])))

    wqa = place(wq, 0)
    wkv = w_ukv.reshape(MLA_KV_LORA, N_MLA_HEADS, MLA_NOPE + MLA_V)
    wka = place(wkv[:, :, :MLA_NOPE], 0)
    wv = place(wkv[:, :, MLA_NOPE:], 0)
    flat = lambda a: a.reshape(a.shape[0], -1).astype(BF16)
    return flat(wqa), flat(wka), flat(wv).T


def _rope_tables(T):
    half = MLA_ROPE // 2
    pos = jnp.arange(T, dtype=F32)
    inv_freq = ROPE_THETA ** (-jnp.arange(half, dtype=F32) / half)
    ang = pos[:, None] * inv_freq[None, :]
    cos, sin = jnp.cos(ang), jnp.sin(ang)
    scale = (MLA_NOPE + MLA_ROPE) ** -0.5 * LOG2_E
    ones = jnp.ones((T, MLA_NOPE), F32)
    tail = jnp.zeros((T, LANES - MLA_NOPE - MLA_ROPE), F32)
    cos_row = lambda head: jnp.concatenate([head, cos, cos, tail], axis=1)
    sin_row = jnp.concatenate([0.0 * ones, -sin, sin, tail], axis=1)
    return cos_row(ones) * scale, sin_row * scale, cos_row(0.0 * ones), sin_row


def _compress_weights(w1, w2, pos):
    G, D = NSA_GROUPS, NSA_D
    w1r = w1.reshape(CMP_LEN, D, CMP_HIDDEN)
    eye = jnp.eye(G, dtype=bool)
    halves = []
    for j in range(CMP_LEN // CMP_STRIDE):
        part = w1r[j * CMP_STRIDE:(j + 1) * CMP_STRIDE]
        wide = jnp.where(eye[None, :, None, :, None], part[:, None, :, None, :], 0.0)
        halves.append(wide.reshape(CMP_STRIDE * G * D, G * CMP_HIDDEN).astype(BF16))
    w2bd = jnp.where(eye[:, None, :, None], w2[None, :, None, :], 0.0)
    w2bd = w2bd.reshape(G * CMP_HIDDEN, G * D).astype(BF16)
    posr = pos.reshape(CMP_LEN // CMP_STRIDE, CMP_STRIDE, 1, D)
    posw = jnp.broadcast_to(posr, (CMP_LEN // CMP_STRIDE, CMP_STRIDE, G, D)).reshape(
        CMP_LEN // CMP_STRIDE, CMP_STRIDE * G * D)
    return halves[0], halves[1], w2bd, posw


def _position_features(pos):
    return np.stack([(pos // LANES) * LANES, pos % LANES, np.ones_like(pos), np.ones_like(pos)],
                    axis=1).astype(np.float32)


def _nsa_tables(T):
    kpos = np.arange(T)
    onehot = np.zeros((T, LANES), np.float32)
    onehot[kpos, kpos // SLC_BLOCK] = 1.0
    key_feats, pad_rows = [], []
    for g in range(NSA_GROUPS):
        base = _feat_base(g)
        ft = np.zeros((T, LANES), np.float32)
        ft[:, base:base + 4] = _position_features(kpos)
        key_feats.append(jnp.asarray(ft))
        pad = np.zeros((WINDOW, LANES), np.float32)
        pad[:, base + 4] = 1.0
        pad_rows.append(jnp.asarray(pad, BF16))
    nsub = T // CMP_STRIDE
    c = np.arange(nsub)
    cfeat = np.zeros((nsub, LANES), np.float32)
    cfeat[:, 0:4] = _position_features(c * CMP_STRIDE + CMP_LEN - 1)
    kl = np.arange(WIN_KEYS)[:, None]
    tl = np.arange(TQ_NSA)[None, :]
    band = np.where((kl > tl) & (kl <= tl + WINDOW), 0.0, NEG_INF).astype(np.float32)
    band = np.tile(band, (1, NSA_HPG))
    ratio = SLC_BLOCK // CMP_STRIDE
    mt = np.zeros((LANES, nsub), np.float32)
    valid = c < nsub - 1
    for j in range(CMP_LEN // CMP_STRIDE):
        np.add.at(mt, ((c[valid] + j) // ratio, c[valid]), 1.0)
    qf = np.zeros((NSA_GROUPS, 4, R_NSA, LANES), np.float32)
    tl_rows = np.tile(np.arange(TQ_NSA), NSA_HPG).astype(np.float32)
    for g in range(NSA_GROUPS):
        slope = np.repeat(2.0 ** -(g * NSA_HPG + np.arange(NSA_HPG) + 1.0), TQ_NSA).astype(np.float32)
        for k, base in enumerate((_feat_base(g), 0)):
            qf[g, 2 * k, :, base] = slope
            qf[g, 2 * k, :, base + 1] = slope
            qf[g, 2 * k, :, base + 3] = -slope * tl_rows
            qf[g, 2 * k, :, base + 4] = NEG_INF
            qf[g, 2 * k + 1, :, base + 2] = -slope * TQ_NSA
    return (jnp.asarray(onehot, BF16), key_feats, pad_rows, jnp.asarray(cfeat, BF16),
            jnp.asarray(band), jnp.asarray(mt, BF16), jnp.asarray(qf))


def kernel(x, p, w_in, w_ck1, w_ck2, pos_ck, w_cv1, w_cv2, pos_cv, mla_q_norm, w_uq, mla_kv_norm,
           w_ukv, w_out, ln1_g, ln1_b, w_up, w_down, ln2_g, ln2_b, w_ple_gate, w_ple, ln3_g, ln3_b):
    B, T, _ = x.shape
    N = B * T
    assert T % TQ_MLA == 0 and T % TM_PROJ == 0 and T // SLC_BLOCK <= LANES
    cq_t, sq_t, ck_t, sk_t = _rope_tables(T)
    onehot, key_feats, pad_rows, cfeat, band, mt, qf = _nsa_tables(T)
    row2 = lambda v: v.reshape(1, -1)
    xc = x.reshape(N, D_MODEL)
    for i in range(DEPTH):
        w_all, wvst, wvwt = _arrange_in_weights(w_in[i])
        wqa, wka, wv = _arrange_mla_weights(w_uq[i], w_ukv[i])
        (qn, kc, vc, ks0, ks1, vst, kw0, kw1, vwt, gates, qm, km, vmt) = _in_proj(
            xc, w_all, wvst, wvwt, wqa, wka, wv, row2(mla_q_norm[i]), row2(mla_kv_norm[i]),
            key_feats[0], key_feats[1], cq_t, sq_t, ck_t, sk_t, B, T)
        wk_lo, wk_hi, wk2, pk = _compress_weights(w_ck1[i], w_ck2[i], pos_ck[i])
        wv_lo, wv_hi, wv2, pv = _compress_weights(w_cv1[i], w_cv2[i], pos_cv[i])
        nsub = T // CMP_STRIDE
        kcmp, vcmpt = _compress(kc.reshape(B, nsub, CMP_STRIDE * LANES),
                                vc.reshape(B, nsub, CMP_STRIDE * LANES),
                                pk, pv, wk_lo, wk_hi, wk2, wv_lo, wv_hi, wv2.T)
        pad_k = lambda a, g: jnp.concatenate(
            [jnp.broadcast_to(pad_rows[g], (B, WINDOW, LANES)), a.reshape(B, T, LANES)], axis=1)
        o_nsa = _nsa(qn, kcmp, vcmpt, cfeat, ks0.reshape(B, T, LANES), ks1.reshape(B, T, LANES), vst,
                     pad_k(kw0, 0), pad_k(kw1, 1), jnp.pad(vwt, ((0, 0), (0, 0), (WINDOW, 0))),
                     onehot, band, mt, qf, gates, B, T)
        o_mla = _mla(qm, km.reshape(B, T, -1), vmt, B, T)
        wo = w_out[i]
        nsa_w = N_NSA_HEADS * NSA_D
        wn = wo[:nsa_w].reshape(NSA_GROUPS, NSA_HPG, NSA_D, D_MODEL).transpose(1, 0, 2, 3)
        wn = wn.reshape(nsa_w, D_MODEL).astype(BF16)
        wm = wo[nsa_w:].astype(BF16)
        ln = jnp.stack([ln1_g[i], ln1_b[i], ln2_g[i], ln2_b[i], ln3_g[i], ln3_b[i]])
        xc = _post(o_nsa, o_mla, xc, p[i].reshape(N, D_PLE), wn, wm, w_up[i].astype(BF16),
                   w_down[i].astype(BF16), w_ple_gate[i].astype(BF16), w_ple[i].astype(BF16), ln)
    return xc.reshape(B, T, D_MODEL)
```

```python
import functools

import jax
import jax.numpy as jnp
import numpy as np
from jax import lax
from jax.experimental import pallas as pl
from jax.experimental.pallas import tpu as pltpu

F32 = jnp.float32
BF16 = jnp.bfloat16

D_MODEL = 1024
N_NSA_HEADS = 8
NSA_GROUPS = 2
NSA_HPG = N_NSA_HEADS // NSA_GROUPS
NSA_D = 64
CMP_LEN = 32
CMP_STRIDE = 16
CMP_HIDDEN = 256
SLC_BLOCK = 64
SLC_TOP_N = 16
WINDOW = 512
N_MLA_HEADS = 8
MLA_NOPE = 64
MLA_ROPE = 32
MLA_V = 64
MLA_Q_LORA = 256
MLA_KV_LORA = 128
ROPE_THETA = 10000.0
D_FF = -(-8 * D_MODEL // (3 * 256)) * 256
D_PLE = 256
DEPTH = 1
ALPHA = (2 * DEPTH) ** 0.25
LN_EPS = 1e-5
RMS_EPS = 1e-6
NEG_INF = -1e30
REMOVED_SCORE = -3e38
LOG2_E = 1.4426950408889634
SLC_BLOCK_LOG2 = SLC_BLOCK.bit_length() - 1

LANES = 128
V_ROWS = 80
TQ_NSA = 256
R_NSA = NSA_HPG * TQ_NSA
TK_SLC = 512
BLOCKS_PER_CHUNK = TK_SLC // SLC_BLOCK
N_SLC_CHUNKS = LANES // BLOCKS_PER_CHUNK
SLC_UNROLL = 2
WIN_KEYS = WINDOW + TQ_NSA
TQ_MLA = 512
TK_MLA = 512
MLA_UNROLL = 4
MLA_DEPTH = 2
TM_PROJ = 512
TF_FFN = 256
VMEM_LIMIT = 56 * 1024 * 1024

_C_NQ = 0
_C_KC = _C_NQ + N_NSA_HEADS * NSA_D
_C_VC = _C_KC + LANES
_C_KS = _C_VC + LANES
_C_KW = _C_KS + LANES
_C_GL = _C_KW + LANES
_C_CQ = _C_GL + LANES
_C_CKV = _C_CQ + MLA_Q_LORA
_C_KPA = _C_CKV + MLA_KV_LORA
_C_END = _C_KPA + LANES


def _feat_base(g):
    return NSA_D * (1 - g)


def _nt_dot(a, b):
    return lax.dot_general(a, b, (((1,), (1,)), ((), ())), preferred_element_type=F32)


def _dot(a, b):
    return jnp.dot(a, b, preferred_element_type=F32)


def _layer_norm(v, g, b):
    mu = jnp.mean(v, axis=-1, keepdims=True)
    d = v - mu
    var = jnp.mean(d * d, axis=-1, keepdims=True)
    return d * lax.rsqrt(var + LN_EPS) * g + b


def _rms_norm(v, g):
    return v * lax.rsqrt(jnp.mean(v * v, axis=-1, keepdims=True) + RMS_EPS) * g


def _in_proj_kernel(x_ref, w_ref, wvst_ref, wvwt_ref, wqa_ref, wka_ref, wv_ref,
                    gq_ref, gkv_ref, ft0_ref, ft1_ref, cq_ref, sq_ref, ck_ref, sk_ref,
                    qn_ref, kc_ref, vc_ref, ks0_ref, ks1_ref, vst_ref, kw0_ref, kw1_ref, vwt_ref,
                    gate_ref, qm_ref, km_ref, vm_ref):
    xb = x_ref[...].astype(BF16)
    h = _dot(xb, w_ref[...])
    tm = h.shape[0]
    lane = lax.broadcasted_iota(jnp.int32, (tm, LANES), 1)
    in_half = [lane < NSA_D, lane >= NSA_D]

    def seg(c0, width):
        return h[:, c0:c0 + width]

    def swap_rotary(v):
        half = MLA_ROPE // 2
        first = (lane >= MLA_NOPE) & (lane < MLA_NOPE + half)
        second = (lane >= MLA_NOPE + half) & (lane < MLA_NOPE + MLA_ROPE)
        return jnp.where(first, pltpu.roll(v, LANES - half, 1),
                         jnp.where(second, pltpu.roll(v, half, 1), 0.0))

    for i in range(N_NSA_HEADS):
        g = i // NSA_HPG
        src = seg(_C_NQ + (i // 2) * LANES, LANES)
        if i % 2 != g:
            src = pltpu.roll(src, NSA_D, 1)
        qn_ref[0, i] = (jnp.where(in_half[g], src, 0.0) * (NSA_D ** -0.5)).astype(BF16)
    kc_ref[...] = seg(_C_KC, LANES)
    vc_ref[...] = seg(_C_VC, LANES)
    h_ks, h_kw = seg(_C_KS, LANES), seg(_C_KW, LANES)
    ks0_ref[...] = (jnp.where(in_half[0], h_ks, 0.0) + ft0_ref[...]).astype(BF16)
    ks1_ref[...] = (jnp.where(in_half[1], h_ks, 0.0) + ft1_ref[...]).astype(BF16)
    kw0_ref[...] = (jnp.where(in_half[0], h_kw, 0.0) + ft0_ref[...]).astype(BF16)
    kw1_ref[...] = (jnp.where(in_half[1], h_kw, 0.0) + ft1_ref[...]).astype(BF16)
    def ones_rows(n):
        r = lax.broadcasted_iota(jnp.int32, (n, 1), 0) & (LANES - 1)
        return jnp.where(r == NSA_D, 1.0, 0.0)

    vst_ref[0] = (_nt_dot(wvst_ref[...], xb) + ones_rows(NSA_GROUPS * LANES)).astype(BF16)
    vwt_ref[0] = (_nt_dot(wvwt_ref[...], xb) + ones_rows(NSA_GROUPS * LANES)).astype(BF16)
    gate_ref[...] = jax.nn.sigmoid(seg(_C_GL, LANES))

    cqn = _rms_norm(seg(_C_CQ, MLA_Q_LORA), gq_ref[...]).astype(BF16)
    cq_t = cq_ref[...]
    sq_t = sq_ref[...]
    qa = _dot(cqn, wqa_ref[...])
    for hd in range(N_MLA_HEADS):
        hs = slice(hd * LANES, (hd + 1) * LANES)
        qm_ref[:, hs] = (qa[:, hs] * cq_t + swap_rotary(qa[:, hs]) * sq_t).astype(BF16)

    kvn = _rms_norm(seg(_C_CKV, MLA_KV_LORA), gkv_ref[...]).astype(BF16)
    kpa = seg(_C_KPA, LANES)
    kpe = kpa * ck_ref[...] + swap_rotary(kpa) * sk_ref[...]
    ka = _dot(kvn, wka_ref[...])
    for hd in range(N_MLA_HEADS):
        hs = slice(hd * LANES, (hd + 1) * LANES)
        km_ref[:, hs] = (ka[:, hs] + kpe).astype(BF16)
    vm_ref[0] = (_nt_dot(wv_ref[...], kvn) + ones_rows(N_MLA_HEADS * LANES)).astype(BF16)


def _in_proj(x2, w_all, wvst, wvwt, wqa, wka, wv, gq, gkv, ft0, ft1, cq_t, sq_t, ck_t, sk_t,
             B, T):
    N = B * T
    tm = TM_PROJ
    tpb = T // tm
    full = lambda a: pl.BlockSpec(a.shape, lambda i: (0,) * a.ndim)
    rows = lambda w: pl.BlockSpec((tm, w), lambda i: (i, 0))
    tab = pl.BlockSpec((tm, LANES), lambda i: (i % tpb, 0))
    cols_t = lambda w: pl.BlockSpec((1, w, tm), lambda i: (i // tpb, 0, i % tpb))
    tok = lambda dt: jax.ShapeDtypeStruct((N, LANES), dt)
    out_shape = (
        jax.ShapeDtypeStruct((B, N_NSA_HEADS, T, LANES), BF16),
        tok(F32), tok(F32),
        tok(BF16), tok(BF16),
        jax.ShapeDtypeStruct((B, NSA_GROUPS * LANES, T), BF16),
        tok(BF16), tok(BF16),
        jax.ShapeDtypeStruct((B, NSA_GROUPS * LANES, T), BF16),
        tok(F32),
        jax.ShapeDtypeStruct((N, N_MLA_HEADS * LANES), BF16),
        jax.ShapeDtypeStruct((N, N_MLA_HEADS * LANES), BF16),
        jax.ShapeDtypeStruct((B, N_MLA_HEADS * LANES, T), BF16),
    )
    out_specs = (
        pl.BlockSpec((1, N_NSA_HEADS, tm, LANES), lambda i: (i // tpb, 0, i % tpb, 0)),
        rows(LANES), rows(LANES), rows(LANES), rows(LANES), cols_t(NSA_GROUPS * LANES),
        rows(LANES), rows(LANES), cols_t(NSA_GROUPS * LANES), rows(LANES),
        rows(N_MLA_HEADS * LANES), rows(N_MLA_HEADS * LANES), cols_t(N_MLA_HEADS * LANES),
    )
    return pl.pallas_call(
        _in_proj_kernel,
        grid=(N // tm,),
        in_specs=[rows(D_MODEL), full(w_all), full(wvst), full(wvwt), full(wqa),
                  full(wka), full(wv), full(gq), full(gkv), tab, tab, tab, tab, tab, tab],
        out_specs=out_specs,
        out_shape=out_shape,
        compiler_params=pltpu.CompilerParams(dimension_semantics=("arbitrary",),
                                             vmem_limit_bytes=VMEM_LIMIT),
        name="in_proj",
    )(x2, w_all, wvst, wvwt, wqa, wka, wv, gq, gkv, ft0, ft1, cq_t, sq_t, ck_t, sk_t)


def _compress_kernel(kc_ref, vc_ref, pk_ref, pv_ref, wk_lo_ref, wk_hi_ref, wk2_ref,
                     wv_lo_ref, wv_hi_ref, wv2t_ref, ko_ref, vo_ref):
    nsub = kc_ref.shape[1]

    def hidden(src_ref, pos_ref, lo_ref, hi_ref):
        a = src_ref[0]
        pos = pos_ref[...]
        p0 = _dot((a + pos[0:1]).astype(BF16), lo_ref[...])
        p1 = _dot((a + pos[1:2]).astype(BF16), hi_ref[...])
        hid = p0 + pltpu.roll(p1, nsub - 1, 0)
        return jax.nn.gelu(hid).astype(BF16)

    k_out = _dot(hidden(kc_ref, pk_ref, wk_lo_ref, wk_hi_ref), wk2_ref[...])
    row = lax.broadcasted_iota(jnp.int32, k_out.shape, 0)
    ko_ref[0] = jnp.where(row < nsub - 1, k_out, 0.0).astype(BF16)
    v_out = _nt_dot(wv2t_ref[...], hidden(vc_ref, pv_ref, wv_lo_ref, wv_hi_ref))
    col = lax.broadcasted_iota(jnp.int32, v_out.shape, 1)
    vo_ref[0] = jnp.where(col < nsub - 1, v_out, 0.0).astype(BF16)


def _compress(kc3, vc3, pk, pv, wk_lo, wk_hi, wk2, wv_lo, wv_hi, wv2t):
    B, nsub, width = kc3.shape
    full = lambda a: pl.BlockSpec(a.shape, lambda b: (0,) * a.ndim)
    per_b = pl.BlockSpec((1, nsub, width), lambda b: (b, 0, 0))
    return pl.pallas_call(
        _compress_kernel,
        grid=(B,),
        in_specs=[per_b, per_b, full(pk), full(pv), full(wk_lo), full(wk_hi), full(wk2),
                  full(wv_lo), full(wv_hi), full(wv2t)],
        out_specs=(pl.BlockSpec((1, nsub, LANES), lambda b: (b, 0, 0)),
                   pl.BlockSpec((1, LANES, nsub), lambda b: (b, 0, 0))),
        out_shape=(jax.ShapeDtypeStruct((B, nsub, LANES), BF16),
                   jax.ShapeDtypeStruct((B, LANES, nsub), BF16)),
        compiler_params=pltpu.CompilerParams(dimension_semantics=("arbitrary",),
                                             vmem_limit_bytes=VMEM_LIMIT),
        name="nsa_compress",
    )(kc3, vc3, pk, pv, wk_lo, wk_hi, wk2, wv_lo, wv_hi, wv2t)


def _nsa_kernel(qn_ref, kcmp_ref, vcmpt_ref, cfeat_ref, ks0_ref, ks1_ref, vst_ref,
                kw0_ref, kw1_ref, vwt_ref, onehot_ref, band_ref, mt_ref, qf_ref, gate_ref,
                o_ref, qaug_ref, lst_ref, m_ref, acc_ref, *, n_top):
    qb = pl.program_id(1)
    q0 = qb * TQ_NSA
    R = R_NSA
    ncmp = kcmp_ref.shape[1]

    col = lax.broadcasted_iota(jnp.int32, (1, R), 1)
    t_row = q0 + (col & (TQ_NSA - 1))
    gates_t = gate_ref[...].T
    c_diag = qb // (TK_SLC // TQ_NSA)
    kl_col = lax.broadcasted_iota(jnp.int32, (TK_SLC, 1), 0)
    w0 = pl.multiple_of(q0, TQ_NSA)
    qb_f = qb.astype(F32)

    q_feats, o_wins, o_cmps, imps = [], [], [], []
    for g in range(NSA_GROUPS):
        kw_ref = (kw0_ref, kw1_ref)[g]
        q = qn_ref[0, g * NSA_HPG:(g + 1) * NSA_HPG].reshape(R, LANES)
        q_feat = (q.astype(F32) + qf_ref[g, 0] + qb_f * qf_ref[g, 1]).astype(BF16)
        q_feats.append(q_feat)

        v_rows = slice(g * LANES, g * LANES + V_ROWS)

        sw = _nt_dot(kw_ref[0, pl.ds(w0, WIN_KEYS), :], q_feat) + band_ref[...]
        pw = jnp.exp((sw - jnp.max(sw, axis=0, keepdims=True)).astype(BF16))
        o_win = _dot(vwt_ref[0, v_rows, pl.ds(w0, WIN_KEYS)], pw)
        o_wins.append(o_win[0:NSA_D, :] / o_win[NSA_D:NSA_D + 1, :])

        kc_aug = jnp.concatenate([kcmp_ref[0], cfeat_ref[...]], axis=1)
        q_cmp = jnp.concatenate([q, (qf_ref[g, 2] + qb_f * qf_ref[g, 3]).astype(BF16)], axis=1)
        cend = lax.broadcasted_iota(jnp.int32, (ncmp, 1), 0) * CMP_STRIDE + (CMP_LEN - 1)
        cmask = cend <= t_row
        sc = jnp.where(cmask, _nt_dot(kc_aug, q_cmp), NEG_INF)
        e = jnp.exp(sc - jnp.maximum(jnp.max(sc, axis=0, keepdims=True), 0.1 * NEG_INF))
        den = jnp.sum(e, axis=0, keepdims=True)
        p_cmp = e * jnp.where(den > 0.0, 1.0 / den, 0.0)
        o_cmps.append(_dot(vcmpt_ref[0, g * NSA_D:(g + 1) * NSA_D, :], p_cmp.astype(BF16)))

        p4 = p_cmp[:, 0:TQ_NSA]
        for h in range(1, NSA_HPG):
            p4 = p4 + p_cmp[:, h * TQ_NSA:(h + 1) * TQ_NSA]
        hi = p4.astype(BF16)
        r1 = p4 - hi.astype(F32)
        mid = r1.astype(BF16)
        lo = (r1 - mid.astype(F32)).astype(BF16)
        mt = mt_ref[...]
        imps.append(_dot(mt, hi) + _dot(mt, mid) + _dot(mt, lo))

    imp = jnp.concatenate(imps, axis=1)
    width = NSA_GROUPS * TQ_NSA
    jb = lax.broadcasted_iota(jnp.int32, (LANES, width), 0)
    tl2 = lax.broadcasted_iota(jnp.int32, (LANES, width), 1) & (TQ_NSA - 1)
    cur = (q0 + tl2) >> SLC_BLOCK_LOG2
    forced = (jb == 0) | (jb == cur) | (jb == cur - 1)
    future = jb > cur
    jf = jb.astype(F32)
    rest = jnp.where(forced, REMOVED_SCORE, jnp.where(future, -1.0, imp))
    for _ in range(max(n_top - 3, 0)):
        mx = jnp.max(rest, axis=0, keepdims=True)
        first = jnp.min(jnp.where(rest == mx, jf, float(LANES)), axis=0, keepdims=True)
        rest = jnp.where(jf == first, REMOVED_SCORE, rest)
    allowed2 = jnp.where((rest == REMOVED_SCORE) & jnp.logical_not(future), 1.0, 0.0)

    group_out = []
    for g in range(NSA_GROUPS):
        ks_ref = (ks0_ref, ks1_ref)[g]
        v_rows = slice(g * LANES, g * LANES + V_ROWS)
        o_cmp, o_win = o_cmps[g], o_wins[g]
        allowed = allowed2[:, g * TQ_NSA:(g + 1) * TQ_NSA]
        selneg_t = jnp.where(allowed > 0.0, 0.0, NEG_INF).T.astype(BF16)

        qaug_ref[:, 0:LANES] = q_feats[g]
        for h in range(NSA_HPG):
            qaug_ref[h * TQ_NSA:(h + 1) * TQ_NSA, LANES:2 * LANES] = selneg_t

        count = jnp.int32(0)
        for c in range(N_SLC_CHUNKS):
            used = jnp.max(allowed[c * BLOCKS_PER_CHUNK:(c + 1) * BLOCKS_PER_CHUNK, :]) > 0.0
            lst_ref[count] = jnp.int32(c)
            count = count + (used & (c < c_diag)).astype(jnp.int32)

        def slc_scores(c, causal, ks_ref=ks_ref):
            k0 = pl.multiple_of(c * TK_SLC, TK_SLC)
            k_aug = jnp.concatenate([ks_ref[0, pl.ds(k0, TK_SLC), :],
                                     onehot_ref[pl.ds(k0, TK_SLC), :]], axis=1)
            st = _nt_dot(k_aug, qaug_ref[...])
            if causal:
                st = jnp.where(k0 + kl_col <= t_row, st, NEG_INF)
            return st, jnp.max(st, axis=0, keepdims=True)

        def absorb(tile, c, v_rows=v_rows):
            st, mx = tile
            k0 = pl.multiple_of(c * TK_SLC, TK_SLC)
            m_old = m_ref[...]
            m_new = jnp.maximum(m_old, mx)
            p = jnp.exp((st - m_new).astype(BF16))
            pv = _dot(vst_ref[0, v_rows, pl.ds(k0, TK_SLC)], p)
            acc_ref[...] = jnp.exp(m_old - m_new) * acc_ref[...] + pv
            m_ref[...] = m_new

        def steps(chunks, causal=False, slc_scores=slc_scores, absorb=absorb):
            tiles = [slc_scores(c, causal) for c in chunks]
            for c, tile in zip(chunks, tiles):
                absorb(tile, c)

        m_ref[...] = jnp.full((1, R), NEG_INF, F32)
        acc_ref[...] = jnp.zeros((V_ROWS, R), F32)
        steps([c_diag], causal=True)

        def body(i, carry, steps=steps):
            steps([lst_ref[SLC_UNROLL * i + j] for j in range(SLC_UNROLL)])
            return carry

        n_loop = count // SLC_UNROLL
        lax.fori_loop(0, n_loop, body, 0)
        for j in range(SLC_UNROLL - 1):
            @pl.when(count - n_loop * SLC_UNROLL > j)
            def _(j=j, steps=steps):
                steps([lst_ref[n_loop * SLC_UNROLL + j]])
        o_slc = acc_ref[0:NSA_D, :] / acc_ref[NSA_D:NSA_D + 1, :]

        def gate_row(branch):
            rows = [gates_t[(g * NSA_HPG + h) * 3 + branch:(g * NSA_HPG + h) * 3 + branch + 1, :]
                    for h in range(NSA_HPG)]
            return jnp.concatenate(rows, axis=1)

        group_out.append(gate_row(0) * o_cmp + gate_row(1) * o_slc + gate_row(2) * o_win)

    out_t = jnp.concatenate(group_out, axis=0)
    for h in range(NSA_HPG):
        o_ref[:, h * LANES:(h + 1) * LANES] = out_t[:, h * TQ_NSA:(h + 1) * TQ_NSA].T.astype(BF16)


def _nsa(qn, kcmp, vcmpt, cfeat, ks0, ks1, vst, kw0, kw1, vwt, onehot, band, mt, qf, gates, B, T):
    nq = T // TQ_NSA
    n_top = min(SLC_TOP_N, T // SLC_BLOCK)
    per_b = lambda a: pl.BlockSpec((1,) + a.shape[1:], lambda b, i: (b,) + (0,) * (a.ndim - 1),
                                   pipeline_mode=pl.Buffered(1))
    full = lambda a: pl.BlockSpec(a.shape, lambda b, i: (0,) * a.ndim,
                                  pipeline_mode=pl.Buffered(1))
    return pl.pallas_call(
        functools.partial(_nsa_kernel, n_top=n_top),
        grid=(B, nq),
        in_specs=[pl.BlockSpec((1, N_NSA_HEADS, TQ_NSA, LANES), lambda b, i: (b, 0, i, 0)),
                  per_b(kcmp), per_b(vcmpt), full(cfeat), per_b(ks0), per_b(ks1), per_b(vst),
                  per_b(kw0), per_b(kw1), per_b(vwt), full(onehot), full(band), full(mt), full(qf),
                  pl.BlockSpec((TQ_NSA, LANES), lambda b, i: (b * nq + i, 0))],
        out_specs=pl.BlockSpec((TQ_NSA, NSA_HPG * LANES), lambda b, i: (b * nq + i, 0)),
        out_shape=jax.ShapeDtypeStruct((B * T, NSA_HPG * LANES), BF16),
        scratch_shapes=[pltpu.VMEM((R_NSA, 2 * LANES), BF16),
                        pltpu.SMEM((N_SLC_CHUNKS + 1,), jnp.int32),
                        pltpu.VMEM((1, R_NSA), F32), pltpu.VMEM((V_ROWS, R_NSA), F32)],
        compiler_params=pltpu.CompilerParams(dimension_semantics=("arbitrary", "arbitrary"),
                                             vmem_limit_bytes=VMEM_LIMIT),
        name="nsa_attention",
    )(qn, kcmp, vcmpt, cfeat, ks0, ks1, vst, kw0, kw1, vwt, onehot, band, mt, qf, gates)


def _mla_kernel(q_ref, k_ref, vt_ref, o_ref, m_ref, acc_ref):
    qi = pl.program_id(2)
    q0 = qi * TQ_MLA
    tpos = q0 + lax.broadcasted_iota(jnp.int32, (1, TQ_MLA), 1)
    kl = lax.broadcasted_iota(jnp.int32, (TK_MLA, 1), 0)
    qs = [q_ref[:, hh * LANES:(hh + 1) * LANES] for hh in range(2)]

    def qk(hh, c, causal):
        k0 = pl.multiple_of(c * TK_MLA, TK_MLA)
        st = _nt_dot(k_ref[0, pl.ds(k0, TK_MLA), hh * LANES:(hh + 1) * LANES], qs[hh])
        if causal:
            st = jnp.where(k0 + kl <= tpos, st, NEG_INF)
        return st, jnp.max(st, axis=0, keepdims=True)

    def absorb(hh, tile, c):
        st, mx = tile
        k0 = pl.multiple_of(c * TK_MLA, TK_MLA)
        m_old = m_ref[hh]
        m_new = jnp.maximum(m_old, mx)
        p = jnp.exp2((st - m_new).astype(BF16))
        pv = _dot(vt_ref[0, hh * LANES:(hh + 1) * LANES, pl.ds(k0, TK_MLA)], p)
        acc_ref[hh] = jnp.exp2(m_old - m_new) * acc_ref[hh] + pv
        m_ref[hh] = m_new

    def steps(chunks, causal=False):
        work = [(hh, c) for c in chunks for hh in range(2)]
        tiles = [qk(hh, c, causal) for hh, c in work[:MLA_DEPTH]]
        for i, (hh, c) in enumerate(work):
            absorb(hh, tiles[i], c)
            if i + MLA_DEPTH < len(work):
                nh, nc = work[i + MLA_DEPTH]
                tiles.append(qk(nh, nc, causal))

    m_ref[...] = jnp.full(m_ref.shape, NEG_INF, F32)
    acc_ref[...] = jnp.zeros(acc_ref.shape, F32)
    c_diag = qi // (TK_MLA // TQ_MLA)
    steps([c_diag], causal=True)

    def body(i, carry):
        steps([MLA_UNROLL * i + j for j in range(MLA_UNROLL)])
        return carry

    n_loop = c_diag // MLA_UNROLL
    lax.fori_loop(0, n_loop, body, 0)
    for j in range(MLA_UNROLL - 1):
        @pl.when(c_diag - n_loop * MLA_UNROLL > j)
        def _(j=j):
            steps([n_loop * MLA_UNROLL + j])

    outs = [acc_ref[hh, 0:MLA_V, :] / acc_ref[hh, MLA_V:MLA_V + 1, :] for hh in range(2)]
    o_ref[...] = jnp.concatenate(outs, axis=0).T.astype(BF16)


def _mla(qm, km3, vmt, B, T):
    nq = T // TQ_MLA
    npair = N_MLA_HEADS // 2
    return pl.pallas_call(
        _mla_kernel,
        grid=(B, npair, nq),
        in_specs=[pl.BlockSpec((TQ_MLA, 2 * LANES), lambda b, h, i: (b * nq + i, h)),
                  pl.BlockSpec((1, T, 2 * LANES), lambda b, h, i: (b, 0, h)),
                  pl.BlockSpec((1, 2 * LANES, T), lambda b, h, i: (b, h, 0))],
        out_specs=pl.BlockSpec((TQ_MLA, LANES), lambda b, h, i: (b * nq + i, h)),
        out_shape=jax.ShapeDtypeStruct((B * T, npair * LANES), BF16),
        scratch_shapes=[pltpu.VMEM((2, 1, TQ_MLA), F32), pltpu.VMEM((2, LANES, TQ_MLA), F32)],
        compiler_params=pltpu.CompilerParams(
            dimension_semantics=("arbitrary", "arbitrary", "arbitrary"),
            vmem_limit_bytes=VMEM_LIMIT,
            ),
        name="mla_attention",
    )(qm, km3, vmt)


def _post_kernel(on_ref, om_ref, x_ref, p_ref, wn_ref, wm_ref, wup_ref, wd_ref,
                 wpg_ref, wp_ref, ln_ref, o_ref):
    mix = _dot(on_ref[...], wn_ref[...]) + _dot(om_ref[...], wm_ref[...])
    x1 = _layer_norm(ALPHA * x_ref[...] + mix, ln_ref[0:1, :], ln_ref[1:2, :])
    xb = x1.astype(BF16)
    ffn = jnp.zeros_like(x1)
    for c in range(D_FF // TF_FFN):
        cols = slice(c * TF_FFN, (c + 1) * TF_FFN)
        up_cols = slice(D_FF + c * TF_FFN, D_FF + (c + 1) * TF_FFN)
        hid = jax.nn.silu(_dot(xb, wup_ref[:, cols])) * _dot(xb, wup_ref[:, up_cols])
        ffn = ffn + _dot(hid.astype(BF16), wd_ref[cols, :])
    x2 = _layer_norm(ALPHA * x1 + ffn, ln_ref[2:3, :], ln_ref[3:4, :])
    gate = jax.nn.sigmoid(_dot(x2.astype(BF16), wpg_ref[...]))
    ple = gate * _dot(p_ref[...].astype(BF16), wp_ref[...])
    o_ref[...] = _layer_norm(ALPHA * x2 + ple, ln_ref[4:5, :], ln_ref[5:6, :])


def _post(o_nsa, o_mla, x2, p2, wn, wm, w_up, w_down, wpg, wp, ln):
    N = x2.shape[0]
    tm = TM_PROJ
    rows = lambda w: pl.BlockSpec((tm, w), lambda i: (i, 0))
    full = lambda a: pl.BlockSpec(a.shape, lambda i: (0,) * a.ndim)
    return pl.pallas_call(
        _post_kernel,
        grid=(N // tm,),
        in_specs=[rows(o_nsa.shape[1]), rows(o_mla.shape[1]), rows(D_MODEL), rows(D_PLE),
                  full(wn), full(wm), full(w_up), full(w_down), full(wpg), full(wp), full(ln)],
        out_specs=rows(D_MODEL),
        out_shape=jax.ShapeDtypeStruct((N, D_MODEL), F32),
        compiler_params=pltpu.CompilerParams(dimension_semantics=("arbitrary",),
                                             vmem_limit_bytes=VMEM_LIMIT),
        name="post_attention",
    )(o_nsa, o_mla, x2, p2, wn, wm, w_up, w_down, wpg, wp, ln)


def _arrange_in_weights(w_in):
    splits = (N_NSA_HEADS * NSA_D,) + (NSA_GROUPS * NSA_D,) * 6 + (3 * N_NSA_HEADS, MLA_Q_LORA,
                                                                  MLA_KV_LORA, MLA_ROPE)
    offs = np.cumsum((0,) + splits)
    part = lambda i: w_in[:, offs[i]:offs[i + 1]]

    def place(w, lane0):
        return jnp.pad(w, ((0, 0), (lane0, LANES - lane0 - w.shape[1])))

    cols = [part(0), part(1), part(2), part(3), part(5), place(part(7), 0), part(8), part(9),
            place(part(10), MLA_NOPE)]
    w_all = jnp.concatenate(cols, axis=1)
    assert w_all.shape[1] == _C_END

    def value_rows(w):
        return jnp.concatenate([place(w[:, g * NSA_D:(g + 1) * NSA_D], 0)
                                for g in range(NSA_GROUPS)], axis=1).T

    return w_all.astype(BF16), value_rows(part(4)).astype(BF16), value_rows(part(6)).astype(BF16)


def _arrange_mla_weights(w_uq, w_ukv):
    wq = w_uq.reshape(MLA_Q_LORA, N_MLA_HEADS, MLA_NOPE + MLA_ROPE)

    def place(w, lane0):
        return jnp.pad(w, ((0, 0), (0, 0), (lane0, LANES - lane0 - w.shape[2])))

    wqa = place(wq, 0)
    wkv = w_ukv.reshape(MLA_KV_LORA, N_MLA_HEADS, MLA_NOPE + MLA_V)
    wka = place(wkv[:, :, :MLA_NOPE], 0)
    wv = place(wkv[:, :, MLA_NOPE:], 0)
    flat = lambda a: a.reshape(a.shape[0], -1).astype(BF16)
    return flat(wqa), flat(wka), flat(wv).T


def _rope_tables(T):
    half = MLA_ROPE // 2
    pos = jnp.arange(T, dtype=F32)
    inv_freq = ROPE_THETA ** (-jnp.arange(half, dtype=F32) / half)
    ang = pos[:, None] * inv_freq[None, :]
    cos, sin = jnp.cos(ang), jnp.sin(ang)
    scale = (MLA_NOPE + MLA_ROPE) ** -0.5 * LOG2_E
    ones = jnp.ones((T, MLA_NOPE), F32)
    tail = jnp.zeros((T, LANES - MLA_NOPE - MLA_ROPE), F32)
    cos_row = lambda head: jnp.concatenate([head, cos, cos, tail], axis=1)
    sin_row = jnp.concatenate([0.0 * ones, -sin, sin, tail], axis=1)
    return cos_row(ones) * scale, sin_row * scale, cos_row(0.0 * ones), sin_row


def _compress_weights(w1, w2, pos):
    G, D = NSA_GROUPS, NSA_D
    w1r = w1.reshape(CMP_LEN, D, CMP_HIDDEN)
    eye = jnp.eye(G, dtype=bool)
    halves = []
    for j in range(CMP_LEN // CMP_STRIDE):
        part = w1r[j * CMP_STRIDE:(j + 1) * CMP_STRIDE]
        wide = jnp.where(eye[None, :, None, :, None], part[:, None, :, None, :], 0.0)
        halves.append(wide.reshape(CMP_STRIDE * G * D, G * CMP_HIDDEN).astype(BF16))
    w2bd = jnp.where(eye[:, None, :, None], w2[None, :, None, :], 0.0)
    w2bd = w2bd.reshape(G * CMP_HIDDEN, G * D).astype(BF16)
    posr = pos.reshape(CMP_LEN // CMP_STRIDE, CMP_STRIDE, 1, D)
    posw = jnp.broadcast_to(posr, (CMP_LEN // CMP_STRIDE, CMP_STRIDE, G, D)).reshape(
        CMP_LEN // CMP_STRIDE, CMP_STRIDE * G * D)
    return halves[0], halves[1], w2bd, posw


def _position_features(pos):
    return np.stack([(pos // LANES) * LANES, pos % LANES, np.ones_like(pos), np.ones_like(pos)],
                    axis=1).astype(np.float32)


def _nsa_tables(T):
    kpos = np.arange(T)
    onehot = np.zeros((T, LANES), np.float32)
    onehot[kpos, kpos // SLC_BLOCK] = 1.0
    key_feats, pad_rows = [], []
    for g in range(NSA_GROUPS):
        base = _feat_base(g)
        ft = np.zeros((T, LANES), np.float32)
        ft[:, base:base + 4] = _position_features(kpos)
        key_feats.append(jnp.asarray(ft))
        pad = np.zeros((WINDOW, LANES), np.float32)
        pad[:, base + 4] = 1.0
        pad_rows.append(jnp.asarray(pad, BF16))
    nsub = T // CMP_STRIDE
    c = np.arange(nsub)
    cfeat = np.zeros((nsub, LANES), np.float32)
    cfeat[:, 0:4] = _position_features(c * CMP_STRIDE + CMP_LEN - 1)
    kl = np.arange(WIN_KEYS)[:, None]
    tl = np.arange(TQ_NSA)[None, :]
    band = np.where((kl > tl) & (kl <= tl + WINDOW), 0.0, NEG_INF).astype(np.float32)
    band = np.tile(band, (1, NSA_HPG))
    ratio = SLC_BLOCK // CMP_STRIDE
    mt = np.zeros((LANES, nsub), np.float32)
    valid = c < nsub - 1
    for j in range(CMP_LEN // CMP_STRIDE):
        np.add.at(mt, ((c[valid] + j) // ratio, c[valid]), 1.0)
    qf = np.zeros((NSA_GROUPS, 4, R_NSA, LANES), np.float32)
    tl_rows = np.tile(np.arange(TQ_NSA), NSA_HPG).astype(np.float32)
    for g in range(NSA_GROUPS):
        slope = np.repeat(2.0 ** -(g * NSA_HPG + np.arange(NSA_HPG) + 1.0), TQ_NSA).astype(np.float32)
        for k, base in enumerate((_feat_base(g), 0)):
            qf[g, 2 * k, :, base] = slope
            qf[g, 2 * k, :, base + 1] = slope
            qf[g, 2 * k, :, base + 3] = -slope * tl_rows
            qf[g, 2 * k, :, base + 4] = NEG_INF
            qf[g, 2 * k + 1, :, base + 2] = -slope * TQ_NSA
    return (jnp.asarray(onehot, BF16), key_feats, pad_rows, jnp.asarray(cfeat, BF16),
            jnp.asarray(band), jnp.asarray(mt, BF16), jnp.asarray(qf))


def kernel(x, p, w_in, w_ck1, w_ck2, pos_ck, w_cv1, w_cv2, pos_cv, mla_q_norm, w_uq, mla_kv_norm,
           w_ukv, w_out, ln1_g, ln1_b, w_up, w_down, ln2_g, ln2_b, w_ple_gate, w_ple, ln3_g, ln3_b):
    B, T, _ = x.shape
    N = B * T
    assert T % TQ_MLA == 0 and T % TM_PROJ == 0 and T // SLC_BLOCK <= LANES
    cq_t, sq_t, ck_t, sk_t = _rope_tables(T)
    onehot, key_feats, pad_rows, cfeat, band, mt, qf = _nsa_tables(T)
    row2 = lambda v: v.reshape(1, -1)
    xc = x.reshape(N, D_MODEL)
    for i in range(DEPTH):
        w_all, wvst, wvwt = _arrange_in_weights(w_in[i])
        wqa, wka, wv = _arrange_mla_weights(w_uq[i], w_ukv[i])
        (qn, kc, vc, ks0, ks1, vst, kw0, kw1, vwt, gates, qm, km, vmt) = _in_proj(
            xc, w_all, wvst, wvwt, wqa, wka, wv, row2(mla_q_norm[i]), row2(mla_kv_norm[i]),
            key_feats[0], key_feats[1], cq_t, sq_t, ck_t, sk_t, B, T)
        wk_lo, wk_hi, wk2, pk = _compress_weights(w_ck1[i], w_ck2[i], pos_ck[i])
        wv_lo, wv_hi, wv2, pv = _compress_weights(w_cv1[i], w_cv2[i], pos_cv[i])
        nsub = T // CMP_STRIDE
        kcmp, vcmpt = _compress(kc.reshape(B, nsub, CMP_STRIDE * LANES),
                                vc.reshape(B, nsub, CMP_STRIDE * LANES),
                                pk, pv, wk_lo, wk_hi, wk2, wv_lo, wv_hi, wv2.T)
        pad_k = lambda a, g: jnp.concatenate(
            [jnp.broadcast_to(pad_rows[g], (B, WINDOW, LANES)), a.reshape(B, T, LANES)], axis=1)
        o_nsa = _nsa(qn, kcmp, vcmpt, cfeat, ks0.reshape(B, T, LANES), ks1.reshape(B, T, LANES), vst,
                     pad_k(kw0, 0), pad_k(kw1, 1), jnp.pad(vwt, ((0, 0), (0, 0), (WINDOW, 0))),
                     onehot, band, mt, qf, gates, B, T)
        o_mla = _mla(qm, km.reshape(B, T, -1), vmt, B, T)
        wo = w_out[i]
        nsa_w = N_NSA_HEADS * NSA_D
        wn = wo[:nsa_w].reshape(NSA_GROUPS, NSA_HPG, NSA_D, D_MODEL).transpose(1, 0, 2, 3)
        wn = wn.reshape(nsa_w, D_MODEL).astype(BF16)
        wm = wo[nsa_w:].astype(BF16)
        ln = jnp.stack([ln1_g[i], ln1_b[i], ln2_g[i], ln2_b[i], ln3_g[i], ln3_b[i]])
        xc = _post(o_nsa, o_mla, xc, p[i].reshape(N, D_PLE), wn, wm, w_up[i].astype(BF16),
                   w_down[i].astype(BF16), w_ple_gate[i].astype(BF16), w_ple[i].astype(BF16), ln)
    return xc.reshape(B, T, D_MODEL)
```

```python
import functools

import jax
import jax.numpy as jnp
import numpy as np
from jax import lax
from jax.experimental import pallas as pl
from jax.experimental.pallas import tpu as pltpu

F32 = jnp.float32
BF16 = jnp.bfloat16

D_MODEL = 1024
N_NSA_HEADS = 8
NSA_GROUPS = 2
NSA_HPG = N_NSA_HEADS // NSA_GROUPS
NSA_D = 64
CMP_LEN = 32
CMP_STRIDE = 16
CMP_HIDDEN = 256
SLC_BLOCK = 64
SLC_TOP_N = 16
WINDOW = 512
N_MLA_HEADS = 8
MLA_NOPE = 64
MLA_ROPE = 32
MLA_V = 64
MLA_Q_LORA = 256
MLA_KV_LORA = 128
ROPE_THETA = 10000.0
D_FF = -(-8 * D_MODEL // (3 * 256)) * 256
D_PLE = 256
DEPTH = 1
ALPHA = (2 * DEPTH) ** 0.25
LN_EPS = 1e-5
RMS_EPS = 1e-6
NEG_INF = -1e30
REMOVED_SCORE = -3e38
LOG2_E = 1.4426950408889634
SLC_BLOCK_LOG2 = SLC_BLOCK.bit_length() - 1

LANES = 128
V_ROWS = 80
TQ_NSA = 256
R_NSA = NSA_HPG * TQ_NSA
TK_SLC = 512
BLOCKS_PER_CHUNK = TK_SLC // SLC_BLOCK
N_SLC_CHUNKS = LANES // BLOCKS_PER_CHUNK
SLC_UNROLL = 2
WIN_KEYS = WINDOW + TQ_NSA
TQ_MLA = 512
TK_MLA = 512
MLA_UNROLL = 4
MLA_DEPTH = 2
TM_PROJ = 512
TF_FFN = 256
VMEM_LIMIT = 56 * 1024 * 1024

_C_NQ = 0
_C_KC = _C_NQ + N_NSA_HEADS * NSA_D
_C_VC = _C_KC + LANES
_C_KS = _C_VC + LANES
_C_KW = _C_KS + LANES
_C_GL = _C_KW + LANES
_C_CQ = _C_GL + LANES
_C_CKV = _C_CQ + MLA_Q_LORA
_C_KPA = _C_CKV + MLA_KV_LORA
_C_END = _C_KPA + LANES


def _feat_base(g):
    return NSA_D * (1 - g)


def _nt_dot(a, b):
    return lax.dot_general(a, b, (((1,), (1,)), ((), ())), preferred_element_type=F32)


def _dot(a, b):
    return jnp.dot(a, b, preferred_element_type=F32)


def _layer_norm(v, g, b):
    mu = jnp.mean(v, axis=-1, keepdims=True)
    d = v - mu
    var = jnp.mean(d * d, axis=-1, keepdims=True)
    return d * lax.rsqrt(var + LN_EPS) * g + b


def _rms_norm(v, g):
    return v * lax.rsqrt(jnp.mean(v * v, axis=-1, keepdims=True) + RMS_EPS) * g


def _in_proj_kernel(x_ref, w_ref, wvst_ref, wvwt_ref, wqa_ref, wka_ref, wv_ref,
                    gq_ref, gkv_ref, ft0_ref, ft1_ref, cq_ref, sq_ref, ck_ref, sk_ref,
                    qn_ref, kc_ref, vc_ref, ks0_ref, ks1_ref, vst_ref, kw0_ref, kw1_ref, vwt_ref,
                    gate_ref, qm_ref, km_ref, vm_ref):
    xb = x_ref[...].astype(BF16)
    h = _dot(xb, w_ref[...])
    tm = h.shape[0]
    lane = lax.broadcasted_iota(jnp.int32, (tm, LANES), 1)
    in_half = [lane < NSA_D, lane >= NSA_D]

    def seg(c0, width):
        return h[:, c0:c0 + width]

    def swap_rotary(v):
        half = MLA_ROPE // 2
        first = (lane >= MLA_NOPE) & (lane < MLA_NOPE + half)
        second = (lane >= MLA_NOPE + half) & (lane < MLA_NOPE + MLA_ROPE)
        return jnp.where(first, pltpu.roll(v, LANES - half, 1),
                         jnp.where(second, pltpu.roll(v, half, 1), 0.0))

    for i in range(N_NSA_HEADS):
        g = i // NSA_HPG
        src = seg(_C_NQ + (i // 2) * LANES, LANES)
        if i % 2 != g:
            src = pltpu.roll(src, NSA_D, 1)
        qn_ref[0, i] = (jnp.where(in_half[g], src, 0.0) * (NSA_D ** -0.5)).astype(BF16)
    kc_ref[...] = seg(_C_KC, LANES)
    vc_ref[...] = seg(_C_VC, LANES)
    h_ks, h_kw = seg(_C_KS, LANES), seg(_C_KW, LANES)
    ks0_ref[...] = (jnp.where(in_half[0], h_ks, 0.0) + ft0_ref[...]).astype(BF16)
    ks1_ref[...] = (jnp.where(in_half[1], h_ks, 0.0) + ft1_ref[...]).astype(BF16)
    kw0_ref[...] = (jnp.where(in_half[0], h_kw, 0.0) + ft0_ref[...]).astype(BF16)
    kw1_ref[...] = (jnp.where(in_half[1], h_kw, 0.0) + ft1_ref[...]).astype(BF16)
    def ones_rows(n):
        r = lax.broadcasted_iota(jnp.int32, (n, 1), 0) & (LANES - 1)
        return jnp.where(r == NSA_D, 1.0, 0.0)

    vst_ref[0] = (_nt_dot(wvst_ref[...], xb) + ones_rows(NSA_GROUPS * LANES)).astype(BF16)
    vwt_ref[0] = (_nt_dot(wvwt_ref[...], xb) + ones_rows(NSA_GROUPS * LANES)).astype(BF16)
    gate_ref[...] = jax.nn.sigmoid(seg(_C_GL, LANES))

    cqn = _rms_norm(seg(_C_CQ, MLA_Q_LORA), gq_ref[...]).astype(BF16)
    cq_t = cq_ref[...]
    sq_t = sq_ref[...]
    qa = _dot(cqn, wqa_ref[...])
    for hd in range(N_MLA_HEADS):
        hs = slice(hd * LANES, (hd + 1) * LANES)
        qm_ref[:, hs] = (qa[:, hs] * cq_t + swap_rotary(qa[:, hs]) * sq_t).astype(BF16)

    kvn = _rms_norm(seg(_C_CKV, MLA_KV_LORA), gkv_ref[...]).astype(BF16)
    kpa = seg(_C_KPA, LANES)
    kpe = kpa * ck_ref[...] + swap_rotary(kpa) * sk_ref[...]
    ka = _dot(kvn, wka_ref[...])
    for hd in range(N_MLA_HEADS):
        hs = slice(hd * LANES, (hd + 1) * LANES)
        km_ref[:, hs] = (ka[:, hs] + kpe).astype(BF16)
    vm_ref[0] = (_nt_dot(wv_ref[...], kvn) + ones_rows(N_MLA_HEADS * LANES)).astype(BF16)


def _in_proj(x2, w_all, wvst, wvwt, wqa, wka, wv, gq, gkv, ft0, ft1, cq_t, sq_t, ck_t, sk_t,
             B, T):
    N = B * T
    tm = TM_PROJ
    tpb = T // tm
    full = lambda a: pl.BlockSpec(a.shape, lambda i: (0,) * a.ndim)
    rows = lambda w: pl.BlockSpec((tm, w), lambda i: (i, 0))
    tab = pl.BlockSpec((tm, LANES), lambda i: (i % tpb, 0))
    cols_t = lambda w: pl.BlockSpec((1, w, tm), lambda i: (i // tpb, 0, i % tpb))
    tok = lambda dt: jax.ShapeDtypeStruct((N, LANES), dt)
    out_shape = (
        jax.ShapeDtypeStruct((B, N_NSA_HEADS, T, LANES), BF16),
        tok(F32), tok(F32),
        tok(BF16), tok(BF16),
        jax.ShapeDtypeStruct((B, NSA_GROUPS * LANES, T), BF16),
        tok(BF16), tok(BF16),
        jax.ShapeDtypeStruct((B, NSA_GROUPS * LANES, T), BF16),
        tok(F32),
        jax.ShapeDtypeStruct((N, N_MLA_HEADS * LANES), BF16),
        jax.ShapeDtypeStruct((N, N_MLA_HEADS * LANES), BF16),
        jax.ShapeDtypeStruct((B, N_MLA_HEADS * LANES, T), BF16),
    )
    out_specs = (
        pl.BlockSpec((1, N_NSA_HEADS, tm, LANES), lambda i: (i // tpb, 0, i % tpb, 0)),
        rows(LANES), rows(LANES), rows(LANES), rows(LANES), cols_t(NSA_GROUPS * LANES),
        rows(LANES), rows(LANES), cols_t(NSA_GROUPS * LANES), rows(LANES),
        rows(N_MLA_HEADS * LANES), rows(N_MLA_HEADS * LANES), cols_t(N_MLA_HEADS * LANES),
    )
    return pl.pallas_call(
        _in_proj_kernel,
        grid=(N // tm,),
        in_specs=[rows(D_MODEL), full(w_all), full(wvst), full(wvwt), full(wqa),
                  full(wka), full(wv), full(gq), full(gkv), tab, tab, tab, tab, tab, tab],
        out_specs=out_specs,
        out_shape=out_shape,
        compiler_params=pltpu.CompilerParams(dimension_semantics=("arbitrary",),
                                             vmem_limit_bytes=VMEM_LIMIT),
        name="in_proj",
    )(x2, w_all, wvst, wvwt, wqa, wka, wv, gq, gkv, ft0, ft1, cq_t, sq_t, ck_t, sk_t)


def _compress_kernel(kc_ref, vc_ref, pk_ref, pv_ref, wk_lo_ref, wk_hi_ref, wk2_ref,
                     wv_lo_ref, wv_hi_ref, wv2t_ref, ko_ref, vo_ref):
    nsub = kc_ref.shape[1]

    def hidden(src_ref, pos_ref, lo_ref, hi_ref):
        a = src_ref[0]
        pos = pos_ref[...]
        p0 = _dot((a + pos[0:1]).astype(BF16), lo_ref[...])
        p1 = _dot((a + pos[1:2]).astype(BF16), hi_ref[...])
        hid = p0 + pltpu.roll(p1, nsub - 1, 0)
        return jax.nn.gelu(hid).astype(BF16)

    k_out = _dot(hidden(kc_ref, pk_ref, wk_lo_ref, wk_hi_ref), wk2_ref[...])
    row = lax.broadcasted_iota(jnp.int32, k_out.shape, 0)
    ko_ref[0] = jnp.where(row < nsub - 1, k_out, 0.0).astype(BF16)
    v_out = _nt_dot(wv2t_ref[...], hidden(vc_ref, pv_ref, wv_lo_ref, wv_hi_ref))
    col = lax.broadcasted_iota(jnp.int32, v_out.shape, 1)
    vo_ref[0] = jnp.where(col < nsub - 1, v_out, 0.0).astype(BF16)


def _compress(kc3, vc3, pk, pv, wk_lo, wk_hi, wk2, wv_lo, wv_hi, wv2t):
    B, nsub, width = kc3.shape
    full = lambda a: pl.BlockSpec(a.shape, lambda b: (0,) * a.ndim)
    per_b = pl.BlockSpec((1, nsub, width), lambda b: (b, 0, 0))
    return pl.pallas_call(
        _compress_kernel,
        grid=(B,),
        in_specs=[per_b, per_b, full(pk), full(pv), full(wk_lo), full(wk_hi), full(wk2),
                  full(wv_lo), full(wv_hi), full(wv2t)],
        out_specs=(pl.BlockSpec((1, nsub, LANES), lambda b: (b, 0, 0)),
                   pl.BlockSpec((1, LANES, nsub), lambda b: (b, 0, 0))),
        out_shape=(jax.ShapeDtypeStruct((B, nsub, LANES), BF16),
                   jax.ShapeDtypeStruct((B, LANES, nsub), BF16)),
        compiler_params=pltpu.CompilerParams(dimension_semantics=("arbitrary",),
                                             vmem_limit_bytes=VMEM_LIMIT),
        name="nsa_compress",
    )(kc3, vc3, pk, pv, wk_lo, wk_hi, wk2, wv_lo, wv_hi, wv2t)


def _nsa_kernel(qn_ref, kcmp_ref, vcmpt_ref, cfeat_ref, ks0_ref, ks1_ref, vst_ref,
                kw0_ref, kw1_ref, vwt_ref, onehot_ref, band_ref, mt_ref, qf_ref, gate_ref,
                o_ref, qaug_ref, lst_ref, m_ref, acc_ref, *, n_top):
    qb = pl.program_id(1)
    q0 = qb * TQ_NSA
    R = R_NSA
    ncmp = kcmp_ref.shape[1]

    col = lax.broadcasted_iota(jnp.int32, (1, R), 1)
    t_row = q0 + (col & (TQ_NSA - 1))
    gates_t = gate_ref[...].T
    c_diag = qb // (TK_SLC // TQ_NSA)
    kl_col = lax.broadcasted_iota(jnp.int32, (TK_SLC, 1), 0)
    w0 = pl.multiple_of(q0, TQ_NSA)
    qb_f = qb.astype(F32)

    q_feats, o_wins, o_cmps, imps = [], [], [], []
    for g in range(NSA_GROUPS):
        kw_ref = (kw0_ref, kw1_ref)[g]
        q = qn_ref[0, g * NSA_HPG:(g + 1) * NSA_HPG].reshape(R, LANES)
        q_feat = (q.astype(F32) + qf_ref[g, 0] + qb_f * qf_ref[g, 1]).astype(BF16)
        q_feats.append(q_feat)

        v_rows = slice(g * LANES, g * LANES + V_ROWS)

        sw = _nt_dot(kw_ref[0, pl.ds(w0, WIN_KEYS), :], q_feat) + band_ref[...]
        pw = jnp.exp((sw - jnp.max(sw, axis=0, keepdims=True)).astype(BF16))
        o_win = _dot(vwt_ref[0, v_rows, pl.ds(w0, WIN_KEYS)], pw)
        o_wins.append(o_win[0:NSA_D, :] / o_win[NSA_D:NSA_D + 1, :])

        kc_aug = jnp.concatenate([kcmp_ref[0], cfeat_ref[...]], axis=1)
        q_cmp = jnp.concatenate([q, (qf_ref[g, 2] + qb_f * qf_ref[g, 3]).astype(BF16)], axis=1)
        cend = lax.broadcasted_iota(jnp.int32, (ncmp, 1), 0) * CMP_STRIDE + (CMP_LEN - 1)
        cmask = cend <= t_row
        sc = jnp.where(cmask, _nt_dot(kc_aug, q_cmp), NEG_INF)
        e = jnp.exp(sc - jnp.maximum(jnp.max(sc, axis=0, keepdims=True), 0.1 * NEG_INF))
        den = jnp.sum(e, axis=0, keepdims=True)
        p_cmp = e * jnp.where(den > 0.0, 1.0 / den, 0.0)
        o_cmps.append(_dot(vcmpt_ref[0, g * NSA_D:(g + 1) * NSA_D, :], p_cmp.astype(BF16)))

        p4 = p_cmp[:, 0:TQ_NSA]
        for h in range(1, NSA_HPG):
            p4 = p4 + p_cmp[:, h * TQ_NSA:(h + 1) * TQ_NSA]
        hi = p4.astype(BF16)
        r1 = p4 - hi.astype(F32)
        mid = r1.astype(BF16)
        lo = (r1 - mid.astype(F32)).astype(BF16)
        mt = mt_ref[...]
        imps.append(_dot(mt, hi) + _dot(mt, mid) + _dot(mt, lo))

    imp = jnp.concatenate(imps, axis=1)
    width = NSA_GROUPS * TQ_NSA
    jb = lax.broadcasted_iota(jnp.int32, (LANES, width), 0)
    tl2 = lax.broadcasted_iota(jnp.int32, (LANES, width), 1) & (TQ_NSA - 1)
    cur = (q0 + tl2) >> SLC_BLOCK_LOG2
    forced = (jb == 0) | (jb == cur) | (jb == cur - 1)
    future = jb > cur
    jf = jb.astype(F32)
    rest = jnp.where(forced, REMOVED_SCORE, jnp.where(future, -1.0, imp))
    picked = forced
    for _ in range(max(n_top - 3, 0)):
        mx = jnp.max(rest, axis=0, keepdims=True)
        first = jnp.min(jnp.where(rest == mx, jf, float(LANES)), axis=0, keepdims=True)
        hit = jf == first
        picked = picked | hit
        rest = jnp.where(hit, REMOVED_SCORE, rest)
    allowed2 = jnp.where(picked & jnp.logical_not(future), 1.0, 0.0)

    group_out = []
    for g in range(NSA_GROUPS):
        ks_ref = (ks0_ref, ks1_ref)[g]
        v_rows = slice(g * LANES, g * LANES + V_ROWS)
        o_cmp, o_win = o_cmps[g], o_wins[g]
        allowed = allowed2[:, g * TQ_NSA:(g + 1) * TQ_NSA]
        selneg_t = jnp.where(allowed > 0.0, 0.0, NEG_INF).T.astype(BF16)

        qaug_ref[:, 0:LANES] = q_feats[g]
        for h in range(NSA_HPG):
            qaug_ref[h * TQ_NSA:(h + 1) * TQ_NSA, LANES:2 * LANES] = selneg_t

        count = jnp.int32(0)
        for c in range(N_SLC_CHUNKS):
            used = jnp.max(allowed[c * BLOCKS_PER_CHUNK:(c + 1) * BLOCKS_PER_CHUNK, :]) > 0.0
            lst_ref[count] = jnp.int32(c)
            count = count + (used & (c < c_diag)).astype(jnp.int32)

        def slc_scores(c, causal, ks_ref=ks_ref):
            k0 = pl.multiple_of(c * TK_SLC, TK_SLC)
            k_aug = jnp.concatenate([ks_ref[0, pl.ds(k0, TK_SLC), :],
                                     onehot_ref[pl.ds(k0, TK_SLC), :]], axis=1)
            st = _nt_dot(k_aug, qaug_ref[...])
            if causal:
                st = jnp.where(k0 + kl_col <= t_row, st, NEG_INF)
            return st, jnp.max(st, axis=0, keepdims=True)

        def absorb(tile, c, v_rows=v_rows):
            st, mx = tile
            k0 = pl.multiple_of(c * TK_SLC, TK_SLC)
            m_old = m_ref[...]
            m_new = jnp.maximum(m_old, mx)
            p = jnp.exp((st - m_new).astype(BF16))
            pv = _dot(vst_ref[0, v_rows, pl.ds(k0, TK_SLC)], p)
            acc_ref[...] = jnp.exp(m_old - m_new) * acc_ref[...] + pv
            m_ref[...] = m_new

        def steps(chunks, n_causal=0, slc_scores=slc_scores, absorb=absorb):
            tiles = [slc_scores(c, i < n_causal) for i, c in enumerate(chunks)]
            for c, tile in zip(chunks, tiles):
                absorb(tile, c)

        m_ref[...] = jnp.full((1, R), NEG_INF, F32)
        acc_ref[...] = jnp.zeros((V_ROWS, R), F32)

        @pl.when(count == 0)
        def _(steps=steps):
            steps([c_diag], n_causal=1)

        @pl.when(count > 0)
        def _(steps=steps):
            steps([c_diag, lst_ref[0]], n_causal=1)

        def body(i, carry, steps=steps):
            steps([lst_ref[1 + SLC_UNROLL * i + j] for j in range(SLC_UNROLL)])
            return carry

        n_rest = jnp.maximum(count - 1, 0)
        n_loop = n_rest // SLC_UNROLL
        lax.fori_loop(0, n_loop, body, 0)
        for j in range(SLC_UNROLL - 1):
            @pl.when(n_rest - n_loop * SLC_UNROLL > j)
            def _(j=j, steps=steps):
                steps([lst_ref[1 + n_loop * SLC_UNROLL + j]])
        o_slc = acc_ref[0:NSA_D, :] / acc_ref[NSA_D:NSA_D + 1, :]

        def gate_row(branch):
            rows = [gates_t[(g * NSA_HPG + h) * 3 + branch:(g * NSA_HPG + h) * 3 + branch + 1, :]
                    for h in range(NSA_HPG)]
            return jnp.concatenate(rows, axis=1)

        group_out.append(gate_row(0) * o_cmp + gate_row(1) * o_slc + gate_row(2) * o_win)

    out_t = jnp.concatenate(group_out, axis=0)
    for h in range(NSA_HPG):
        o_ref[:, h * LANES:(h + 1) * LANES] = out_t[:, h * TQ_NSA:(h + 1) * TQ_NSA].T.astype(BF16)


def _nsa(qn, kcmp, vcmpt, cfeat, ks0, ks1, vst, kw0, kw1, vwt, onehot, band, mt, qf, gates, B, T):
    nq = T // TQ_NSA
    n_top = min(SLC_TOP_N, T // SLC_BLOCK)
    per_b = lambda a: pl.BlockSpec((1,) + a.shape[1:], lambda b, i: (b,) + (0,) * (a.ndim - 1),
                                   pipeline_mode=pl.Buffered(1))
    full = lambda a: pl.BlockSpec(a.shape, lambda b, i: (0,) * a.ndim,
                                  pipeline_mode=pl.Buffered(1))
    return pl.pallas_call(
        functools.partial(_nsa_kernel, n_top=n_top),
        grid=(B, nq),
        in_specs=[pl.BlockSpec((1, N_NSA_HEADS, TQ_NSA, LANES), lambda b, i: (b, 0, i, 0)),
                  per_b(kcmp), per_b(vcmpt), full(cfeat), per_b(ks0), per_b(ks1), per_b(vst),
                  per_b(kw0), per_b(kw1), per_b(vwt), full(onehot), full(band), full(mt), full(qf),
                  pl.BlockSpec((TQ_NSA, LANES), lambda b, i: (b * nq + i, 0))],
        out_specs=pl.BlockSpec((TQ_NSA, NSA_HPG * LANES), lambda b, i: (b * nq + i, 0)),
        out_shape=jax.ShapeDtypeStruct((B * T, NSA_HPG * LANES), BF16),
        scratch_shapes=[pltpu.VMEM((R_NSA, 2 * LANES), BF16),
                        pltpu.SMEM((N_SLC_CHUNKS + 1,), jnp.int32),
                        pltpu.VMEM((1, R_NSA), F32), pltpu.VMEM((V_ROWS, R_NSA), F32)],
        compiler_params=pltpu.CompilerParams(dimension_semantics=("arbitrary", "arbitrary"),
                                             vmem_limit_bytes=VMEM_LIMIT),
        name="nsa_attention",
    )(qn, kcmp, vcmpt, cfeat, ks0, ks1, vst, kw0, kw1, vwt, onehot, band, mt, qf, gates)


def _mla_kernel(q_ref, k_ref, vt_ref, o_ref, m_ref, acc_ref):
    qi = pl.program_id(2)
    q0 = qi * TQ_MLA
    tpos = q0 + lax.broadcasted_iota(jnp.int32, (1, TQ_MLA), 1)
    kl = lax.broadcasted_iota(jnp.int32, (TK_MLA, 1), 0)
    qs = [q_ref[:, hh * LANES:(hh + 1) * LANES] for hh in range(2)]

    def qk(hh, c, causal):
        k0 = pl.multiple_of(c * TK_MLA, TK_MLA)
        st = _nt_dot(k_ref[0, pl.ds(k0, TK_MLA), hh * LANES:(hh + 1) * LANES], qs[hh])
        if causal:
            st = jnp.where(k0 + kl <= tpos, st, NEG_INF)
        return st, jnp.max(st, axis=0, keepdims=True)

    def absorb(hh, tile, c):
        st, mx = tile
        k0 = pl.multiple_of(c * TK_MLA, TK_MLA)
        m_old = m_ref[hh]
        m_new = jnp.maximum(m_old, mx)
        p = jnp.exp2((st - m_new).astype(BF16))
        pv = _dot(vt_ref[0, hh * LANES:(hh + 1) * LANES, pl.ds(k0, TK_MLA)], p)
        acc_ref[hh] = jnp.exp2(m_old - m_new) * acc_ref[hh] + pv
        m_ref[hh] = m_new

    def steps(chunks, n_causal=0):
        work = [(hh, c, i < n_causal) for i, c in enumerate(chunks) for hh in range(2)]
        tiles = [qk(*w) for w in work[:MLA_DEPTH]]
        for i, (hh, c, _) in enumerate(work):
            absorb(hh, tiles[i], c)
            if i + MLA_DEPTH < len(work):
                tiles.append(qk(*work[i + MLA_DEPTH]))

    m_ref[...] = jnp.full(m_ref.shape, NEG_INF, F32)
    acc_ref[...] = jnp.zeros(acc_ref.shape, F32)
    c_diag = qi // (TK_MLA // TQ_MLA)

    @pl.when(c_diag == 0)
    def _():
        steps([c_diag], n_causal=1)

    @pl.when(c_diag > 0)
    def _():
        steps([c_diag, c_diag - 1], n_causal=1)

    def body(i, carry):
        steps([MLA_UNROLL * i + j for j in range(MLA_UNROLL)])
        return carry

    n_rest = jnp.maximum(c_diag - 1, 0)
    n_loop = n_rest // MLA_UNROLL
    lax.fori_loop(0, n_loop, body, 0)
    for j in range(MLA_UNROLL - 1):
        @pl.when(n_rest - n_loop * MLA_UNROLL > j)
        def _(j=j):
            steps([n_loop * MLA_UNROLL + j])

    outs = [acc_ref[hh, 0:MLA_V, :] / acc_ref[hh, MLA_V:MLA_V + 1, :] for hh in range(2)]
    o_ref[...] = jnp.concatenate(outs, axis=0).T.astype(BF16)


def _mla(qm, km3, vmt, B, T):
    nq = T // TQ_MLA
    npair = N_MLA_HEADS // 2
    return pl.pallas_call(
        _mla_kernel,
        grid=(B, npair, nq),
        in_specs=[pl.BlockSpec((TQ_MLA, 2 * LANES), lambda b, h, i: (b * nq + i, h)),
                  pl.BlockSpec((1, T, 2 * LANES), lambda b, h, i: (b, 0, h)),
                  pl.BlockSpec((1, 2 * LANES, T), lambda b, h, i: (b, h, 0))],
        out_specs=pl.BlockSpec((TQ_MLA, LANES), lambda b, h, i: (b * nq + i, h)),
        out_shape=jax.ShapeDtypeStruct((B * T, npair * LANES), BF16),
        scratch_shapes=[pltpu.VMEM((2, 1, TQ_MLA), F32), pltpu.VMEM((2, LANES, TQ_MLA), F32)],
        compiler_params=pltpu.CompilerParams(
            dimension_semantics=("arbitrary", "arbitrary", "arbitrary"),
            vmem_limit_bytes=VMEM_LIMIT,
            ),
        name="mla_attention",
    )(qm, km3, vmt)


def _post_kernel(on_ref, om_ref, x_ref, p_ref, wn_ref, wm_ref, wup_ref, wd_ref,
                 wpg_ref, wp_ref, ln_ref, o_ref):
    mix = _dot(on_ref[...], wn_ref[...]) + _dot(om_ref[...], wm_ref[...])
    x1 = _layer_norm(ALPHA * x_ref[...] + mix, ln_ref[0:1, :], ln_ref[1:2, :])
    xb = x1.astype(BF16)
    ffn = jnp.zeros_like(x1)
    for c in range(D_FF // TF_FFN):
        cols = slice(c * TF_FFN, (c + 1) * TF_FFN)
        up_cols = slice(D_FF + c * TF_FFN, D_FF + (c + 1) * TF_FFN)
        hid = jax.nn.silu(_dot(xb, wup_ref[:, cols])) * _dot(xb, wup_ref[:, up_cols])
        ffn = ffn + _dot(hid.astype(BF16), wd_ref[cols, :])
    x2 = _layer_norm(ALPHA * x1 + ffn, ln_ref[2:3, :], ln_ref[3:4, :])
    gate = jax.nn.sigmoid(_dot(x2.astype(BF16), wpg_ref[...]))
    ple = gate * _dot(p_ref[...].astype(BF16), wp_ref[...])
    o_ref[...] = _layer_norm(ALPHA * x2 + ple, ln_ref[4:5, :], ln_ref[5:6, :])


def _post(o_nsa, o_mla, x2, p2, wn, wm, w_up, w_down, wpg, wp, ln):
    N = x2.shape[0]
    tm = TM_PROJ
    rows = lambda w: pl.BlockSpec((tm, w), lambda i: (i, 0))
    full = lambda a: pl.BlockSpec(a.shape, lambda i: (0,) * a.ndim)
    return pl.pallas_call(
        _post_kernel,
        grid=(N // tm,),
        in_specs=[rows(o_nsa.shape[1]), rows(o_mla.shape[1]), rows(D_MODEL), rows(D_PLE),
                  full(wn), full(wm), full(w_up), full(w_down), full(wpg), full(wp), full(ln)],
        out_specs=rows(D_MODEL),
        out_shape=jax.ShapeDtypeStruct((N, D_MODEL), F32),
        compiler_params=pltpu.CompilerParams(dimension_semantics=("arbitrary",),
                                             vmem_limit_bytes=VMEM_LIMIT),
        name="post_attention",
    )(o_nsa, o_mla, x2, p2, wn, wm, w_up, w_down, wpg, wp, ln)


def _arrange_in_weights(w_in):
    splits = (N_NSA_HEADS * NSA_D,) + (NSA_GROUPS * NSA_D,) * 6 + (3 * N_NSA_HEADS, MLA_Q_LORA,
                                                                  MLA_KV_LORA, MLA_ROPE)
    offs = np.cumsum((0,) + splits)
    part = lambda i: w_in[:, offs[i]:offs[i + 1]]

    def place(w, lane0):
        return jnp.pad(w, ((0, 0), (lane0, LANES - lane0 - w.shape[1])))

    cols = [part(0), part(1), part(2), part(3), part(5), place(part(7), 0), part(8), part(9),
            place(part(10), MLA_NOPE)]
    w_all = jnp.concatenate(cols, axis=1)
    assert w_all.shape[1] == _C_END

    def value_rows(w):
        return jnp.concatenate([place(w[:, g * NSA_D:(g + 1) * NSA_D], 0)
                                for g in range(NSA_GROUPS)], axis=1).T

    return w_all.astype(BF16), value_rows(part(4)).astype(BF16), value_rows(part(6)).astype(BF16)


def _arrange_mla_weights(w_uq, w_ukv):
    wq = w_uq.reshape(MLA_Q_LORA, N_MLA_HEADS, MLA_NOPE + MLA_ROPE)

    def place(w, lane0):
        return jnp.pad(w, ((0, 0), (0, 0), (lane0, LANES - lane0 - w.shape[2])))

    wqa = place(wq, 0)
    wkv = w_ukv.reshape(MLA_KV_LORA, N_MLA_HEADS, MLA_NOPE + MLA_V)
    wka = place(wkv[:, :, :MLA_NOPE], 0)
    wv = place(wkv[:, :, MLA_NOPE:], 0)
    flat = lambda a: a.reshape(a.shape[0], -1).astype(BF16)
    return flat(wqa), flat(wka), flat(wv).T


def _rope_tables(T):
    half = MLA_ROPE // 2
    pos = jnp.arange(T, dtype=F32)
    inv_freq = ROPE_THETA ** (-jnp.arange(half, dtype=F32) / half)
    ang = pos[:, None] * inv_freq[None, :]
    cos, sin = jnp.cos(ang), jnp.sin(ang)
    scale = (MLA_NOPE + MLA_ROPE) ** -0.5 * LOG2_E
    ones = jnp.ones((T, MLA_NOPE), F32)
    tail = jnp.zeros((T, LANES - MLA_NOPE - MLA_ROPE), F32)
    cos_row = lambda head: jnp.concatenate([head, cos, cos, tail], axis=1)
    sin_row = jnp.concatenate([0.0 * ones, -sin, sin, tail], axis=1)
    return cos_row(ones) * scale, sin_row * scale, cos_row(0.0 * ones), sin_row


def _compress_weights(w1, w2, pos):
    G, D = NSA_GROUPS, NSA_D
    w1r = w1.reshape(CMP_LEN, D, CMP_HIDDEN)
    eye = jnp.eye(G, dtype=bool)
    halves = []
    for j in range(CMP_LEN // CMP_STRIDE):
        part = w1r[j * CMP_STRIDE:(j + 1) * CMP_STRIDE]
        wide = jnp.where(eye[None, :, None, :, None], part[:, None, :, None, :], 0.0)
        halves.append(wide.reshape(CMP_STRIDE * G * D, G * CMP_HIDDEN).astype(BF16))
    w2bd = jnp.where(eye[:, None, :, None], w2[None, :, None, :], 0.0)
    w2bd = w2bd.reshape(G * CMP_HIDDEN, G * D).astype(BF16)
    posr = pos.reshape(CMP_LEN // CMP_STRIDE, CMP_STRIDE, 1, D)
    posw = jnp.broadcast_to(posr, (CMP_LEN // CMP_STRIDE, CMP_STRIDE, G, D)).reshape(
        CMP_LEN // CMP_STRIDE, CMP_STRIDE * G * D)
    return halves[0], halves[1], w2bd, posw


def _position_features(pos):
    return np.stack([(pos // LANES) * LANES, pos % LANES, np.ones_like(pos), np.ones_like(pos)],
                    axis=1).astype(np.float32)


def _nsa_tables(T):
    kpos = np.arange(T)
    onehot = np.zeros((T, LANES), np.float32)
    onehot[kpos, kpos // SLC_BLOCK] = 1.0
    key_feats, pad_rows = [], []
    for g in range(NSA_GROUPS):
        base = _feat_base(g)
        ft = np.zeros((T, LANES), np.float32)
        ft[:, base:base + 4] = _position_features(kpos)
        key_feats.append(jnp.asarray(ft))
        pad = np.zeros((WINDOW, LANES), np.float32)
        pad[:, base + 4] = 1.0
        pad_rows.append(jnp.asarray(pad, BF16))
    nsub = T // CMP_STRIDE
    c = np.arange(nsub)
    cfeat = np.zeros((nsub, LANES), np.float32)
    cfeat[:, 0:4] = _position_features(c * CMP_STRIDE + CMP_LEN - 1)
    kl = np.arange(WIN_KEYS)[:, None]
    tl = np.arange(TQ_NSA)[None, :]
    band = np.where((kl > tl) & (kl <= tl + WINDOW), 0.0, NEG_INF).astype(np.float32)
    band = np.tile(band, (1, NSA_HPG))
    ratio = SLC_BLOCK // CMP_STRIDE
    mt = np.zeros((LANES, nsub), np.float32)
    valid = c < nsub - 1
    for j in range(CMP_LEN // CMP_STRIDE):
        np.add.at(mt, ((c[valid] + j) // ratio, c[valid]), 1.0)
    qf = np.zeros((NSA_GROUPS, 4, R_NSA, LANES), np.float32)
    tl_rows = np.tile(np.arange(TQ_NSA), NSA_HPG).astype(np.float32)
    for g in range(NSA_GROUPS):
        slope = np.repeat(2.0 ** -(g * NSA_HPG + np.arange(NSA_HPG) + 1.0), TQ_NSA).astype(np.float32)
        for k, base in enumerate((_feat_base(g), 0)):
            qf[g, 2 * k, :, base] = slope
            qf[g, 2 * k, :, base + 1] = slope
            qf[g, 2 * k, :, base + 3] = -slope * tl_rows
            qf[g, 2 * k, :, base + 4] = NEG_INF
            qf[g, 2 * k + 1, :, base + 2] = -slope * TQ_NSA
    return (jnp.asarray(onehot, BF16), key_feats, pad_rows, jnp.asarray(cfeat, BF16),
            jnp.asarray(band), jnp.asarray(mt, BF16), jnp.asarray(qf))


def kernel(x, p, w_in, w_ck1, w_ck2, pos_ck, w_cv1, w_cv2, pos_cv, mla_q_norm, w_uq, mla_kv_norm,
           w_ukv, w_out, ln1_g, ln1_b, w_up, w_down, ln2_g, ln2_b, w_ple_gate, w_ple, ln3_g, ln3_b):
    B, T, _ = x.shape
    N = B * T
    assert T % TQ_MLA == 0 and T % TM_PROJ == 0 and T // SLC_BLOCK <= LANES
    cq_t, sq_t, ck_t, sk_t = _rope_tables(T)
    onehot, key_feats, pad_rows, cfeat, band, mt, qf = _nsa_tables(T)
    row2 = lambda v: v.reshape(1, -1)
    xc = x.reshape(N, D_MODEL)
    for i in range(DEPTH):
        w_all, wvst, wvwt = _arrange_in_weights(w_in[i])
        wqa, wka, wv = _arrange_mla_weights(w_uq[i], w_ukv[i])
        (qn, kc, vc, ks0, ks1, vst, kw0, kw1, vwt, gates, qm, km, vmt) = _in_proj(
            xc, w_all, wvst, wvwt, wqa, wka, wv, row2(mla_q_norm[i]), row2(mla_kv_norm[i]),
            key_feats[0], key_feats[1], cq_t, sq_t, ck_t, sk_t, B, T)
        wk_lo, wk_hi, wk2, pk = _compress_weights(w_ck1[i], w_ck2[i], pos_ck[i])
        wv_lo, wv_hi, wv2, pv = _compress_weights(w_cv1[i], w_cv2[i], pos_cv[i])
        nsub = T // CMP_STRIDE
        kcmp, vcmpt = _compress(kc.reshape(B, nsub, CMP_STRIDE * LANES),
                                vc.reshape(B, nsub, CMP_STRIDE * LANES),
                                pk, pv, wk_lo, wk_hi, wk2, wv_lo, wv_hi, wv2.T)
        pad_k = lambda a, g: jnp.concatenate(
            [jnp.broadcast_to(pad_rows[g], (B, WINDOW, LANES)), a.reshape(B, T, LANES)], axis=1)
        o_nsa = _nsa(qn, kcmp, vcmpt, cfeat, ks0.reshape(B, T, LANES), ks1.reshape(B, T, LANES), vst,
                     pad_k(kw0, 0), pad_k(kw1, 1), jnp.pad(vwt, ((0, 0), (0, 0), (WINDOW, 0))),
                     onehot, band, mt, qf, gates, B, T)
        o_mla = _mla(qm, km.reshape(B, T, -1), vmt, B, T)
        wo = w_out[i]
        nsa_w = N_NSA_HEADS * NSA_D
        wn = wo[:nsa_w].reshape(NSA_GROUPS, NSA_HPG, NSA_D, D_MODEL).transpose(1, 0, 2, 3)
        wn = wn.reshape(nsa_w, D_MODEL).astype(BF16)
        wm = wo[nsa_w:].astype(BF16)
        ln = jnp.stack([ln1_g[i], ln1_b[i], ln2_g[i], ln2_b[i], ln3_g[i], ln3_b[i]])
        xc = _post(o_nsa, o_mla, xc, p[i].reshape(N, D_PLE), wn, wm, w_up[i].astype(BF16),
                   w_down[i].astype(BF16), w_ple_gate[i].astype(BF16), w_ple[i].astype(BF16), ln)
    return xc.reshape(B, T, D_MODEL)
```

```python
import functools

import jax
import jax.numpy as jnp
import numpy as np
from jax import lax
from jax.experimental import pallas as pl
from jax.experimental.pallas import tpu as pltpu

F32 = jnp.float32
BF16 = jnp.bfloat16

D_MODEL = 1024
N_NSA_HEADS = 8
NSA_GROUPS = 2
NSA_HPG = N_NSA_HEADS // NSA_GROUPS
NSA_D = 64
CMP_LEN = 32
CMP_STRIDE = 16
CMP_HIDDEN = 256
SLC_BLOCK = 64
SLC_TOP_N = 16
WINDOW = 512
N_MLA_HEADS = 8
MLA_NOPE = 64
MLA_ROPE = 32
MLA_V = 64
MLA_Q_LORA = 256
MLA_KV_LORA = 128
ROPE_THETA = 10000.0
D_FF = -(-8 * D_MODEL // (3 * 256)) * 256
D_PLE = 256
DEPTH = 1
ALPHA = (2 * DEPTH) ** 0.25
LN_EPS = 1e-5
RMS_EPS = 1e-6
NEG_INF = -1e30
REMOVED_SCORE = -3e38
LOG2_E = 1.4426950408889634
SLC_BLOCK_LOG2 = SLC_BLOCK.bit_length() - 1

LANES = 128
V_ROWS = 80
TQ_NSA = 256
R_NSA = NSA_HPG * TQ_NSA
TK_SLC = 256
BLOCKS_PER_CHUNK = TK_SLC // SLC_BLOCK
N_SLC_CHUNKS = LANES // BLOCKS_PER_CHUNK
SLC_UNROLL = 2
WIN_KEYS = WINDOW + TQ_NSA
TQ_MLA = 512
TK_MLA = 512
MLA_UNROLL = 4
MLA_DEPTH = 2
TM_PROJ = 512
TF_FFN = 256
VMEM_LIMIT = 56 * 1024 * 1024

_C_NQ = 0
_C_KC = _C_NQ + N_NSA_HEADS * NSA_D
_C_VC = _C_KC + LANES
_C_KS = _C_VC + LANES
_C_KW = _C_KS + LANES
_C_GL = _C_KW + LANES
_C_CQ = _C_GL + LANES
_C_CKV = _C_CQ + MLA_Q_LORA
_C_KPA = _C_CKV + MLA_KV_LORA
_C_END = _C_KPA + LANES


def _feat_base(g):
    return NSA_D * (1 - g)


def _nt_dot(a, b):
    return lax.dot_general(a, b, (((1,), (1,)), ((), ())), preferred_element_type=F32)


def _dot(a, b):
    return jnp.dot(a, b, preferred_element_type=F32)


def _layer_norm(v, g, b):
    mu = jnp.mean(v, axis=-1, keepdims=True)
    d = v - mu
    var = jnp.mean(d * d, axis=-1, keepdims=True)
    return d * lax.rsqrt(var + LN_EPS) * g + b


def _rms_norm(v, g):
    return v * lax.rsqrt(jnp.mean(v * v, axis=-1, keepdims=True) + RMS_EPS) * g


def _in_proj_kernel(x_ref, w_ref, wvst_ref, wvwt_ref, wqa_ref, wka_ref, wv_ref,
                    gq_ref, gkv_ref, ft0_ref, ft1_ref, cq_ref, sq_ref, ck_ref, sk_ref,
                    qn_ref, kc_ref, vc_ref, ks0_ref, ks1_ref, vst_ref, kw0_ref, kw1_ref, vwt_ref,
                    gate_ref, qm_ref, km_ref, vm_ref):
    xb = x_ref[...].astype(BF16)
    h = _dot(xb, w_ref[...])
    tm = h.shape[0]
    lane = lax.broadcasted_iota(jnp.int32, (tm, LANES), 1)
    in_half = [lane < NSA_D, lane >= NSA_D]

    def seg(c0, width):
        return h[:, c0:c0 + width]

    def swap_rotary(v):
        half = MLA_ROPE // 2
        first = (lane >= MLA_NOPE) & (lane < MLA_NOPE + half)
        second = (lane >= MLA_NOPE + half) & (lane < MLA_NOPE + MLA_ROPE)
        return jnp.where(first, pltpu.roll(v, LANES - half, 1),
                         jnp.where(second, pltpu.roll(v, half, 1), 0.0))

    for i in range(N_NSA_HEADS):
        g = i // NSA_HPG
        src = seg(_C_NQ + (i // 2) * LANES, LANES)
        if i % 2 != g:
            src = pltpu.roll(src, NSA_D, 1)
        qn_ref[0, i] = (jnp.where(in_half[g], src, 0.0) * (NSA_D ** -0.5)).astype(BF16)
    kc_ref[...] = seg(_C_KC, LANES)
    vc_ref[...] = seg(_C_VC, LANES)
    h_ks, h_kw = seg(_C_KS, LANES), seg(_C_KW, LANES)
    ks0_ref[...] = (jnp.where(in_half[0], h_ks, 0.0) + ft0_ref[...]).astype(BF16)
    ks1_ref[...] = (jnp.where(in_half[1], h_ks, 0.0) + ft1_ref[...]).astype(BF16)
    kw0_ref[...] = (jnp.where(in_half[0], h_kw, 0.0) + ft0_ref[...]).astype(BF16)
    kw1_ref[...] = (jnp.where(in_half[1], h_kw, 0.0) + ft1_ref[...]).astype(BF16)
    def ones_rows(n):
        r = lax.broadcasted_iota(jnp.int32, (n, 1), 0) & (LANES - 1)
        return jnp.where(r == NSA_D, 1.0, 0.0)

    vst_ref[0] = (_nt_dot(wvst_ref[...], xb) + ones_rows(NSA_GROUPS * LANES)).astype(BF16)
    vwt_ref[0] = (_nt_dot(wvwt_ref[...], xb) + ones_rows(NSA_GROUPS * LANES)).astype(BF16)
    gate_ref[...] = jax.nn.sigmoid(seg(_C_GL, LANES))

    cqn = _rms_norm(seg(_C_CQ, MLA_Q_LORA), gq_ref[...]).astype(BF16)
    cq_t = cq_ref[...]
    sq_t = sq_ref[...]
    qa = _dot(cqn, wqa_ref[...])
    for hd in range(N_MLA_HEADS):
        hs = slice(hd * LANES, (hd + 1) * LANES)
        qm_ref[:, hs] = (qa[:, hs] * cq_t + swap_rotary(qa[:, hs]) * sq_t).astype(BF16)

    kvn = _rms_norm(seg(_C_CKV, MLA_KV_LORA), gkv_ref[...]).astype(BF16)
    kpa = seg(_C_KPA, LANES)
    kpe = kpa * ck_ref[...] + swap_rotary(kpa) * sk_ref[...]
    ka = _dot(kvn, wka_ref[...])
    for hd in range(N_MLA_HEADS):
        hs = slice(hd * LANES, (hd + 1) * LANES)
        km_ref[:, hs] = (ka[:, hs] + kpe).astype(BF16)
    vm_ref[0] = (_nt_dot(wv_ref[...], kvn) + ones_rows(N_MLA_HEADS * LANES)).astype(BF16)


def _in_proj(x2, w_all, wvst, wvwt, wqa, wka, wv, gq, gkv, ft0, ft1, cq_t, sq_t, ck_t, sk_t,
             B, T):
    N = B * T
    tm = TM_PROJ
    tpb = T // tm
    full = lambda a: pl.BlockSpec(a.shape, lambda i: (0,) * a.ndim)
    rows = lambda w: pl.BlockSpec((tm, w), lambda i: (i, 0))
    tab = pl.BlockSpec((tm, LANES), lambda i: (i % tpb, 0))
    cols_t = lambda w: pl.BlockSpec((1, w, tm), lambda i: (i // tpb, 0, i % tpb))
    tok = lambda dt: jax.ShapeDtypeStruct((N, LANES), dt)
    out_shape = (
        jax.ShapeDtypeStruct((B, N_NSA_HEADS, T, LANES), BF16),
        tok(F32), tok(F32),
        tok(BF16), tok(BF16),
        jax.ShapeDtypeStruct((B, NSA_GROUPS * LANES, T), BF16),
        tok(BF16), tok(BF16),
        jax.ShapeDtypeStruct((B, NSA_GROUPS * LANES, T), BF16),
        tok(F32),
        jax.ShapeDtypeStruct((N, N_MLA_HEADS * LANES), BF16),
        jax.ShapeDtypeStruct((N, N_MLA_HEADS * LANES), BF16),
        jax.ShapeDtypeStruct((B, N_MLA_HEADS * LANES, T), BF16),
    )
    out_specs = (
        pl.BlockSpec((1, N_NSA_HEADS, tm, LANES), lambda i: (i // tpb, 0, i % tpb, 0)),
        rows(LANES), rows(LANES), rows(LANES), rows(LANES), cols_t(NSA_GROUPS * LANES),
        rows(LANES), rows(LANES), cols_t(NSA_GROUPS * LANES), rows(LANES),
        rows(N_MLA_HEADS * LANES), rows(N_MLA_HEADS * LANES), cols_t(N_MLA_HEADS * LANES),
    )
    return pl.pallas_call(
        _in_proj_kernel,
        grid=(N // tm,),
        in_specs=[rows(D_MODEL), full(w_all), full(wvst), full(wvwt), full(wqa),
                  full(wka), full(wv), full(gq), full(gkv), tab, tab, tab, tab, tab, tab],
        out_specs=out_specs,
        out_shape=out_shape,
        compiler_params=pltpu.CompilerParams(dimension_semantics=("arbitrary",),
                                             vmem_limit_bytes=VMEM_LIMIT),
        name="in_proj",
    )(x2, w_all, wvst, wvwt, wqa, wka, wv, gq, gkv, ft0, ft1, cq_t, sq_t, ck_t, sk_t)


def _compress_kernel(kc_ref, vc_ref, pk_ref, pv_ref, wk_lo_ref, wk_hi_ref, wk2_ref,
                     wv_lo_ref, wv_hi_ref, wv2t_ref, ko_ref, vo_ref):
    nsub = kc_ref.shape[1]

    def hidden(src_ref, pos_ref, lo_ref, hi_ref):
        a = src_ref[0]
        pos = pos_ref[...]
        p0 = _dot((a + pos[0:1]).astype(BF16), lo_ref[...])
        p1 = _dot((a + pos[1:2]).astype(BF16), hi_ref[...])
        hid = p0 + pltpu.roll(p1, nsub - 1, 0)
        return jax.nn.gelu(hid).astype(BF16)

    k_out = _dot(hidden(kc_ref, pk_ref, wk_lo_ref, wk_hi_ref), wk2_ref[...])
    row = lax.broadcasted_iota(jnp.int32, k_out.shape, 0)
    ko_ref[0] = jnp.where(row < nsub - 1, k_out, 0.0).astype(BF16)
    v_out = _nt_dot(wv2t_ref[...], hidden(vc_ref, pv_ref, wv_lo_ref, wv_hi_ref))
    col = lax.broadcasted_iota(jnp.int32, v_out.shape, 1)
    vo_ref[0] = jnp.where(col < nsub - 1, v_out, 0.0).astype(BF16)


def _compress(kc3, vc3, pk, pv, wk_lo, wk_hi, wk2, wv_lo, wv_hi, wv2t):
    B, nsub, width = kc3.shape
    full = lambda a: pl.BlockSpec(a.shape, lambda b: (0,) * a.ndim)
    per_b = pl.BlockSpec((1, nsub, width), lambda b: (b, 0, 0))
    return pl.pallas_call(
        _compress_kernel,
        grid=(B,),
        in_specs=[per_b, per_b, full(pk), full(pv), full(wk_lo), full(wk_hi), full(wk2),
                  full(wv_lo), full(wv_hi), full(wv2t)],
        out_specs=(pl.BlockSpec((1, nsub, LANES), lambda b: (b, 0, 0)),
                   pl.BlockSpec((1, LANES, nsub), lambda b: (b, 0, 0))),
        out_shape=(jax.ShapeDtypeStruct((B, nsub, LANES), BF16),
                   jax.ShapeDtypeStruct((B, LANES, nsub), BF16)),
        compiler_params=pltpu.CompilerParams(dimension_semantics=("arbitrary",),
                                             vmem_limit_bytes=VMEM_LIMIT),
        name="nsa_compress",
    )(kc3, vc3, pk, pv, wk_lo, wk_hi, wk2, wv_lo, wv_hi, wv2t)


def _nsa_kernel(qn_ref, kcmp_ref, vcmpt_ref, cfeat_ref, ks0_ref, ks1_ref, vst_ref,
                kw0_ref, kw1_ref, vwt_ref, onehot_ref, band_ref, mt_ref, qf_ref, gate_ref,
                o_ref, qaug_ref, lst_ref, m_ref, acc_ref, *, n_top):
    qb = pl.program_id(1)
    q0 = qb * TQ_NSA
    R = R_NSA
    ncmp = kcmp_ref.shape[1]

    col = lax.broadcasted_iota(jnp.int32, (1, R), 1)
    t_row = q0 + (col & (TQ_NSA - 1))
    gates_t = gate_ref[...].T
    c_diag = qb // (TK_SLC // TQ_NSA)
    kl_col = lax.broadcasted_iota(jnp.int32, (TK_SLC, 1), 0)
    w0 = pl.multiple_of(q0, TQ_NSA)
    qb_f = qb.astype(F32)

    q_feats, o_wins, o_cmps, imps = [], [], [], []
    for g in range(NSA_GROUPS):
        kw_ref = (kw0_ref, kw1_ref)[g]
        q = qn_ref[0, g * NSA_HPG:(g + 1) * NSA_HPG].reshape(R, LANES)
        q_feat = (q.astype(F32) + qf_ref[g, 0] + qb_f * qf_ref[g, 1]).astype(BF16)
        q_feats.append(q_feat)

        v_rows = slice(g * LANES, g * LANES + V_ROWS)

        sw = _nt_dot(kw_ref[0, pl.ds(w0, WIN_KEYS), :], q_feat) + band_ref[...]
        pw = jnp.exp((sw - jnp.max(sw, axis=0, keepdims=True)).astype(BF16))
        o_win = _dot(vwt_ref[0, v_rows, pl.ds(w0, WIN_KEYS)], pw)
        o_wins.append(o_win[0:NSA_D, :] / o_win[NSA_D:NSA_D + 1, :])

        kc_aug = jnp.concatenate([kcmp_ref[0], cfeat_ref[...]], axis=1)
        q_cmp = jnp.concatenate([q, (qf_ref[g, 2] + qb_f * qf_ref[g, 3]).astype(BF16)], axis=1)
        cend = lax.broadcasted_iota(jnp.int32, (ncmp, 1), 0) * CMP_STRIDE + (CMP_LEN - 1)
        cmask = cend <= t_row
        sc = jnp.where(cmask, _nt_dot(kc_aug, q_cmp), NEG_INF)
        e = jnp.exp(sc - jnp.maximum(jnp.max(sc, axis=0, keepdims=True), 0.1 * NEG_INF))
        den = jnp.sum(e, axis=0, keepdims=True)
        p_cmp = e * jnp.where(den > 0.0, 1.0 / den, 0.0)
        o_cmps.append(_dot(vcmpt_ref[0, g * NSA_D:(g + 1) * NSA_D, :], p_cmp.astype(BF16)))

        p4 = p_cmp[:, 0:TQ_NSA]
        for h in range(1, NSA_HPG):
            p4 = p4 + p_cmp[:, h * TQ_NSA:(h + 1) * TQ_NSA]
        hi = p4.astype(BF16)
        r1 = p4 - hi.astype(F32)
        mid = r1.astype(BF16)
        lo = (r1 - mid.astype(F32)).astype(BF16)
        mt = mt_ref[...]
        imps.append(_dot(mt, hi) + _dot(mt, mid) + _dot(mt, lo))

    imp = jnp.concatenate(imps, axis=1)
    width = NSA_GROUPS * TQ_NSA
    jb = lax.broadcasted_iota(jnp.int32, (LANES, width), 0)
    tl2 = lax.broadcasted_iota(jnp.int32, (LANES, width), 1) & (TQ_NSA - 1)
    cur = (q0 + tl2) >> SLC_BLOCK_LOG2
    forced = (jb == 0) | (jb == cur) | (jb == cur - 1)
    future = jb > cur
    jf = jb.astype(F32)
    rest = jnp.where(forced, REMOVED_SCORE, jnp.where(future, -1.0, imp))
    picked = forced
    for _ in range(max(n_top - 3, 0)):
        mx = jnp.max(rest, axis=0, keepdims=True)
        first = jnp.min(jnp.where(rest == mx, jf, float(LANES)), axis=0, keepdims=True)
        hit = jf == first
        picked = picked | hit
        rest = jnp.where(hit, REMOVED_SCORE, rest)
    allowed2 = jnp.where(picked & jnp.logical_not(future), 1.0, 0.0)

    group_out = []
    for g in range(NSA_GROUPS):
        ks_ref = (ks0_ref, ks1_ref)[g]
        v_rows = slice(g * LANES, g * LANES + V_ROWS)
        o_cmp, o_win = o_cmps[g], o_wins[g]
        allowed = allowed2[:, g * TQ_NSA:(g + 1) * TQ_NSA]
        selneg_t = jnp.where(allowed > 0.0, 0.0, NEG_INF).T.astype(BF16)

        qaug_ref[:, 0:LANES] = q_feats[g]
        for h in range(NSA_HPG):
            qaug_ref[h * TQ_NSA:(h + 1) * TQ_NSA, LANES:2 * LANES] = selneg_t

        count = jnp.int32(0)
        for c in range(N_SLC_CHUNKS):
            used = jnp.max(allowed[c * BLOCKS_PER_CHUNK:(c + 1) * BLOCKS_PER_CHUNK, :]) > 0.0
            lst_ref[count] = jnp.int32(c)
            count = count + (used & (c < c_diag)).astype(jnp.int32)

        def slc_scores(c, causal, ks_ref=ks_ref):
            k0 = pl.multiple_of(c * TK_SLC, TK_SLC)
            k_aug = jnp.concatenate([ks_ref[0, pl.ds(k0, TK_SLC), :],
                                     onehot_ref[pl.ds(k0, TK_SLC), :]], axis=1)
            st = _nt_dot(k_aug, qaug_ref[...])
            if causal:
                st = jnp.where(k0 + kl_col <= t_row, st, NEG_INF)
            return st, jnp.max(st, axis=0, keepdims=True)

        def absorb(tile, c, v_rows=v_rows):
            st, mx = tile
            k0 = pl.multiple_of(c * TK_SLC, TK_SLC)
            m_old = m_ref[...]
            m_new = jnp.maximum(m_old, mx)
            p = jnp.exp((st - m_new).astype(BF16))
            pv = _dot(vst_ref[0, v_rows, pl.ds(k0, TK_SLC)], p)
            acc_ref[...] = jnp.exp(m_old - m_new) * acc_ref[...] + pv
            m_ref[...] = m_new

        def steps(chunks, causal=False, slc_scores=slc_scores, absorb=absorb):
            tiles = [slc_scores(c, causal) for c in chunks]
            for c, tile in zip(chunks, tiles):
                absorb(tile, c)

        m_ref[...] = jnp.full((1, R), NEG_INF, F32)
        acc_ref[...] = jnp.zeros((V_ROWS, R), F32)
        steps([c_diag], causal=True)

        def body(i, carry, steps=steps):
            steps([lst_ref[SLC_UNROLL * i + j] for j in range(SLC_UNROLL)])
            return carry

        n_loop = count // SLC_UNROLL
        lax.fori_loop(0, n_loop, body, 0)
        for j in range(SLC_UNROLL - 1):
            @pl.when(count - n_loop * SLC_UNROLL > j)
            def _(j=j, steps=steps):
                steps([lst_ref[n_loop * SLC_UNROLL + j]])
        o_slc = acc_ref[0:NSA_D, :] / acc_ref[NSA_D:NSA_D + 1, :]

        def gate_row(branch):
            rows = [gates_t[(g * NSA_HPG + h) * 3 + branch:(g * NSA_HPG + h) * 3 + branch + 1, :]
                    for h in range(NSA_HPG)]
            return jnp.concatenate(rows, axis=1)

        group_out.append(gate_row(0) * o_cmp + gate_row(1) * o_slc + gate_row(2) * o_win)

    out_t = jnp.concatenate(group_out, axis=0)
    for h in range(NSA_HPG):
        o_ref[:, h * LANES:(h + 1) * LANES] = out_t[:, h * TQ_NSA:(h + 1) * TQ_NSA].T.astype(BF16)


def _nsa(qn, kcmp, vcmpt, cfeat, ks0, ks1, vst, kw0, kw1, vwt, onehot, band, mt, qf, gates, B, T):
    nq = T // TQ_NSA
    n_top = min(SLC_TOP_N, T // SLC_BLOCK)
    per_b = lambda a: pl.BlockSpec((1,) + a.shape[1:], lambda b, i: (b,) + (0,) * (a.ndim - 1),
                                   pipeline_mode=pl.Buffered(1))
    full = lambda a: pl.BlockSpec(a.shape, lambda b, i: (0,) * a.ndim,
                                  pipeline_mode=pl.Buffered(1))
    return pl.pallas_call(
        functools.partial(_nsa_kernel, n_top=n_top),
        grid=(B, nq),
        in_specs=[pl.BlockSpec((1, N_NSA_HEADS, TQ_NSA, LANES), lambda b, i: (b, 0, i, 0)),
                  per_b(kcmp), per_b(vcmpt), full(cfeat), per_b(ks0), per_b(ks1), per_b(vst),
                  per_b(kw0), per_b(kw1), per_b(vwt), full(onehot), full(band), full(mt), full(qf),
                  pl.BlockSpec((TQ_NSA, LANES), lambda b, i: (b * nq + i, 0))],
        out_specs=pl.BlockSpec((TQ_NSA, NSA_HPG * LANES), lambda b, i: (b * nq + i, 0)),
        out_shape=jax.ShapeDtypeStruct((B * T, NSA_HPG * LANES), BF16),
        scratch_shapes=[pltpu.VMEM((R_NSA, 2 * LANES), BF16),
                        pltpu.SMEM((N_SLC_CHUNKS + 1,), jnp.int32),
                        pltpu.VMEM((1, R_NSA), F32), pltpu.VMEM((V_ROWS, R_NSA), F32)],
        compiler_params=pltpu.CompilerParams(dimension_semantics=("arbitrary", "arbitrary"),
                                             vmem_limit_bytes=VMEM_LIMIT),
        name="nsa_attention",
    )(qn, kcmp, vcmpt, cfeat, ks0, ks1, vst, kw0, kw1, vwt, onehot, band, mt, qf, gates)


def _mla_kernel(q_ref, k_ref, vt_ref, o_ref, m_ref, acc_ref):
    qi = pl.program_id(2)
    q0 = qi * TQ_MLA
    tpos = q0 + lax.broadcasted_iota(jnp.int32, (1, TQ_MLA), 1)
    kl = lax.broadcasted_iota(jnp.int32, (TK_MLA, 1), 0)
    qs = [q_ref[:, hh * LANES:(hh + 1) * LANES] for hh in range(2)]

    def qk(hh, c, causal):
        k0 = pl.multiple_of(c * TK_MLA, TK_MLA)
        st = _nt_dot(k_ref[0, pl.ds(k0, TK_MLA), hh * LANES:(hh + 1) * LANES], qs[hh])
        if causal:
            st = jnp.where(k0 + kl <= tpos, st, NEG_INF)
        return st, jnp.max(st, axis=0, keepdims=True)

    def absorb(hh, tile, c):
        st, mx = tile
        k0 = pl.multiple_of(c * TK_MLA, TK_MLA)
        m_old = m_ref[hh]
        m_new = jnp.maximum(m_old, mx)
        p = jnp.exp2((st - m_new).astype(BF16))
        pv = _dot(vt_ref[0, hh * LANES:(hh + 1) * LANES, pl.ds(k0, TK_MLA)], p)
        acc_ref[hh] = jnp.exp2(m_old - m_new) * acc_ref[hh] + pv
        m_ref[hh] = m_new

    def steps(chunks, causal=False):
        work = [(hh, c) for c in chunks for hh in range(2)]
        tiles = [qk(hh, c, causal) for hh, c in work[:MLA_DEPTH]]
        for i, (hh, c) in enumerate(work):
            absorb(hh, tiles[i], c)
            if i + MLA_DEPTH < len(work):
                nh, nc = work[i + MLA_DEPTH]
                tiles.append(qk(nh, nc, causal))

    m_ref[...] = jnp.full(m_ref.shape, NEG_INF, F32)
    acc_ref[...] = jnp.zeros(acc_ref.shape, F32)
    c_diag = qi // (TK_MLA // TQ_MLA)
    steps([c_diag], causal=True)

    def body(i, carry):
        steps([MLA_UNROLL * i + j for j in range(MLA_UNROLL)])
        return carry

    n_loop = c_diag // MLA_UNROLL
    lax.fori_loop(0, n_loop, body, 0)
    for j in range(MLA_UNROLL - 1):
        @pl.when(c_diag - n_loop * MLA_UNROLL > j)
        def _(j=j):
            steps([n_loop * MLA_UNROLL + j])

    outs = [acc_ref[hh, 0:MLA_V, :] / acc_ref[hh, MLA_V:MLA_V + 1, :] for hh in range(2)]
    o_ref[...] = jnp.concatenate(outs, axis=0).T.astype(BF16)


def _mla(qm, km3, vmt, B, T):
    nq = T // TQ_MLA
    npair = N_MLA_HEADS // 2
    return pl.pallas_call(
        _mla_kernel,
        grid=(B, npair, nq),
        in_specs=[pl.BlockSpec((TQ_MLA, 2 * LANES), lambda b, h, i: (b * nq + i, h)),
                  pl.BlockSpec((1, T, 2 * LANES), lambda b, h, i: (b, 0, h)),
                  pl.BlockSpec((1, 2 * LANES, T), lambda b, h, i: (b, h, 0))],
        out_specs=pl.BlockSpec((TQ_MLA, LANES), lambda b, h, i: (b * nq + i, h)),
        out_shape=jax.ShapeDtypeStruct((B * T, npair * LANES), BF16),
        scratch_shapes=[pltpu.VMEM((2, 1, TQ_MLA), F32), pltpu.VMEM((2, LANES, TQ_MLA), F32)],
        compiler_params=pltpu.CompilerParams(
            dimension_semantics=("arbitrary", "arbitrary", "arbitrary"),
            vmem_limit_bytes=VMEM_LIMIT,
            ),
        name="mla_attention",
    )(qm, km3, vmt)


def _post_kernel(on_ref, om_ref, x_ref, p_ref, wn_ref, wm_ref, wup_ref, wd_ref,
                 wpg_ref, wp_ref, ln_ref, o_ref):
    mix = _dot(on_ref[...], wn_ref[...]) + _dot(om_ref[...], wm_ref[...])
    x1 = _layer_norm(ALPHA * x_ref[...] + mix, ln_ref[0:1, :], ln_ref[1:2, :])
    xb = x1.astype(BF16)
    ffn = jnp.zeros_like(x1)
    for c in range(D_FF // TF_FFN):
        cols = slice(c * TF_FFN, (c + 1) * TF_FFN)
        up_cols = slice(D_FF + c * TF_FFN, D_FF + (c + 1) * TF_FFN)
        hid = jax.nn.silu(_dot(xb, wup_ref[:, cols])) * _dot(xb, wup_ref[:, up_cols])
        ffn = ffn + _dot(hid.astype(BF16), wd_ref[cols, :])
    x2 = _layer_norm(ALPHA * x1 + ffn, ln_ref[2:3, :], ln_ref[3:4, :])
    gate = jax.nn.sigmoid(_dot(x2.astype(BF16), wpg_ref[...]))
    ple = gate * _dot(p_ref[...].astype(BF16), wp_ref[...])
    o_ref[...] = _layer_norm(ALPHA * x2 + ple, ln_ref[4:5, :], ln_ref[5:6, :])


def _post(o_nsa, o_mla, x2, p2, wn, wm, w_up, w_down, wpg, wp, ln):
    N = x2.shape[0]
    tm = TM_PROJ
    rows = lambda w: pl.BlockSpec((tm, w), lambda i: (i, 0))
    full = lambda a: pl.BlockSpec(a.shape, lambda i: (0,) * a.ndim)
    return pl.pallas_call(
        _post_kernel,
        grid=(N // tm,),
        in_specs=[rows(o_nsa.shape[1]), rows(o_mla.shape[1]), rows(D_MODEL), rows(D_PLE),
                  full(wn), full(wm), full(w_up), full(w_down), full(wpg), full(wp), full(ln)],
        out_specs=rows(D_MODEL),
        out_shape=jax.ShapeDtypeStruct((N, D_MODEL), F32),
        compiler_params=pltpu.CompilerParams(dimension_semantics=("arbitrary",),
                                             vmem_limit_bytes=VMEM_LIMIT),
        name="post_attention",
    )(o_nsa, o_mla, x2, p2, wn, wm, w_up, w_down, wpg, wp, ln)


def _arrange_in_weights(w_in):
    splits = (N_NSA_HEADS * NSA_D,) + (NSA_GROUPS * NSA_D,) * 6 + (3 * N_NSA_HEADS, MLA_Q_LORA,
                                                                  MLA_KV_LORA, MLA_ROPE)
    offs = np.cumsum((0,) + splits)
    part = lambda i: w_in[:, offs[i]:offs[i + 1]]

    def place(w, lane0):
        return jnp.pad(w, ((0, 0), (lane0, LANES - lane0 - w.shape[1])))

    cols = [part(0), part(1), part(2), part(3), part(5), place(part(7), 0), part(8), part(9),
            place(part(10), MLA_NOPE)]
    w_all = jnp.concatenate(cols, axis=1)
    assert w_all.shape[1] == _C_END

    def value_rows(w):
        return jnp.concatenate([place(w[:, g * NSA_D:(g + 1) * NSA_D], 0)
                                for g in range(NSA_GROUPS)], axis=1).T

    return w_all.astype(BF16), value_rows(part(4)).astype(BF16), value_rows(part(6)).astype(BF16)


def _arrange_mla_weights(w_uq, w_ukv):
    wq = w_uq.reshape(MLA_Q_LORA, N_MLA_HEADS, MLA_NOPE + MLA_ROPE)

    def place(w, lane0):
        return jnp.pad(w, ((0, 0), (0, 0), (lane0, LANES - lane0 - w.shape[2])))

    wqa = place(wq, 0)
    wkv = w_ukv.reshape(MLA_KV_LORA, N_MLA_HEADS, MLA_NOPE + MLA_V)
    wka = place(wkv[:, :, :MLA_NOPE], 0)
    wv = place(wkv[:, :, MLA_NOPE:], 0)
    flat = lambda a: a.reshape(a.shape[0], -1).astype(BF16)
    return flat(wqa), flat(wka), flat(wv).T


def _rope_tables(T):
    half = MLA_ROPE // 2
    pos = jnp.arange(T, dtype=F32)
    inv_freq = ROPE_THETA ** (-jnp.arange(half, dtype=F32) / half)
    ang = pos[:, None] * inv_freq[None, :]
    cos, sin = jnp.cos(ang), jnp.sin(ang)
    scale = (MLA_NOPE + MLA_ROPE) ** -0.5 * LOG2_E
    ones = jnp.ones((T, MLA_NOPE), F32)
    tail = jnp.zeros((T, LANES - MLA_NOPE - MLA_ROPE), F32)
    cos_row = lambda head: jnp.concatenate([head, cos, cos, tail], axis=1)
    sin_row = jnp.concatenate([0.0 * ones, -sin, sin, tail], axis=1)
    return cos_row(ones) * scale, sin_row * scale, cos_row(0.0 * ones), sin_row


def _compress_weights(w1, w2, pos):
    G, D = NSA_GROUPS, NSA_D
    w1r = w1.reshape(CMP_LEN, D, CMP_HIDDEN)
    eye = jnp.eye(G, dtype=bool)
    halves = []
    for j in range(CMP_LEN // CMP_STRIDE):
        part = w1r[j * CMP_STRIDE:(j + 1) * CMP_STRIDE]
        wide = jnp.where(eye[None, :, None, :, None], part[:, None, :, None, :], 0.0)
        halves.append(wide.reshape(CMP_STRIDE * G * D, G * CMP_HIDDEN).astype(BF16))
    w2bd = jnp.where(eye[:, None, :, None], w2[None, :, None, :], 0.0)
    w2bd = w2bd.reshape(G * CMP_HIDDEN, G * D).astype(BF16)
    posr = pos.reshape(CMP_LEN // CMP_STRIDE, CMP_STRIDE, 1, D)
    posw = jnp.broadcast_to(posr, (CMP_LEN // CMP_STRIDE, CMP_STRIDE, G, D)).reshape(
        CMP_LEN // CMP_STRIDE, CMP_STRIDE * G * D)
    return halves[0], halves[1], w2bd, posw


def _position_features(pos):
    return np.stack([(pos // LANES) * LANES, pos % LANES, np.ones_like(pos), np.ones_like(pos)],
                    axis=1).astype(np.float32)


def _nsa_tables(T):
    kpos = np.arange(T)
    onehot = np.zeros((T, LANES), np.float32)
    onehot[kpos, kpos // SLC_BLOCK] = 1.0
    key_feats, pad_rows = [], []
    for g in range(NSA_GROUPS):
        base = _feat_base(g)
        ft = np.zeros((T, LANES), np.float32)
        ft[:, base:base + 4] = _position_features(kpos)
        key_feats.append(jnp.asarray(ft))
        pad = np.zeros((WINDOW, LANES), np.float32)
        pad[:, base + 4] = 1.0
        pad_rows.append(jnp.asarray(pad, BF16))
    nsub = T // CMP_STRIDE
    c = np.arange(nsub)
    cfeat = np.zeros((nsub, LANES), np.float32)
    cfeat[:, 0:4] = _position_features(c * CMP_STRIDE + CMP_LEN - 1)
    kl = np.arange(WIN_KEYS)[:, None]
    tl = np.arange(TQ_NSA)[None, :]
    band = np.where((kl > tl) & (kl <= tl + WINDOW), 0.0, NEG_INF).astype(np.float32)
    band = np.tile(band, (1, NSA_HPG))
    ratio = SLC_BLOCK // CMP_STRIDE
    mt = np.zeros((LANES, nsub), np.float32)
    valid = c < nsub - 1
    for j in range(CMP_LEN // CMP_STRIDE):
        np.add.at(mt, ((c[valid] + j) // ratio, c[valid]), 1.0)
    qf = np.zeros((NSA_GROUPS, 4, R_NSA, LANES), np.float32)
    tl_rows = np.tile(np.arange(TQ_NSA), NSA_HPG).astype(np.float32)
    for g in range(NSA_GROUPS):
        slope = np.repeat(2.0 ** -(g * NSA_HPG + np.arange(NSA_HPG) + 1.0), TQ_NSA).astype(np.float32)
        for k, base in enumerate((_feat_base(g), 0)):
            qf[g, 2 * k, :, base] = slope
            qf[g, 2 * k, :, base + 1] = slope
            qf[g, 2 * k, :, base + 3] = -slope * tl_rows
            qf[g, 2 * k, :, base + 4] = NEG_INF
            qf[g, 2 * k + 1, :, base + 2] = -slope * TQ_NSA
    return (jnp.asarray(onehot, BF16), key_feats, pad_rows, jnp.asarray(cfeat, BF16),
            jnp.asarray(band), jnp.asarray(mt, BF16), jnp.asarray(qf))


def kernel(x, p, w_in, w_ck1, w_ck2, pos_ck, w_cv1, w_cv2, pos_cv, mla_q_norm, w_uq, mla_kv_norm,
           w_ukv, w_out, ln1_g, ln1_b, w_up, w_down, ln2_g, ln2_b, w_ple_gate, w_ple, ln3_g, ln3_b):
    B, T, _ = x.shape
    N = B * T
    assert T % TQ_MLA == 0 and T % TM_PROJ == 0 and T // SLC_BLOCK <= LANES
    cq_t, sq_t, ck_t, sk_t = _rope_tables(T)
    onehot, key_feats, pad_rows, cfeat, band, mt, qf = _nsa_tables(T)
    row2 = lambda v: v.reshape(1, -1)
    xc = x.reshape(N, D_MODEL)
    for i in range(DEPTH):
        w_all, wvst, wvwt = _arrange_in_weights(w_in[i])
        wqa, wka, wv = _arrange_mla_weights(w_uq[i], w_ukv[i])
        (qn, kc, vc, ks0, ks1, vst, kw0, kw1, vwt, gates, qm, km, vmt) = _in_proj(
            xc, w_all, wvst, wvwt, wqa, wka, wv, row2(mla_q_norm[i]), row2(mla_kv_norm[i]),
            key_feats[0], key_feats[1], cq_t, sq_t, ck_t, sk_t, B, T)
        wk_lo, wk_hi, wk2, pk = _compress_weights(w_ck1[i], w_ck2[i], pos_ck[i])
        wv_lo, wv_hi, wv2, pv = _compress_weights(w_cv1[i], w_cv2[i], pos_cv[i])
        nsub = T // CMP_STRIDE
        kcmp, vcmpt = _compress(kc.reshape(B, nsub, CMP_STRIDE * LANES),
                                vc.reshape(B, nsub, CMP_STRIDE * LANES),
                                pk, pv, wk_lo, wk_hi, wk2, wv_lo, wv_hi, wv2.T)
        pad_k = lambda a, g: jnp.concatenate(
            [jnp.broadcast_to(pad_rows[g], (B, WINDOW, LANES)), a.reshape(B, T, LANES)], axis=1)
        o_nsa = _nsa(qn, kcmp, vcmpt, cfeat, ks0.reshape(B, T, LANES), ks1.reshape(B, T, LANES), vst,
                     pad_k(kw0, 0), pad_k(kw1, 1), jnp.pad(vwt, ((0, 0), (0, 0), (WINDOW, 0))),
                     onehot, band, mt, qf, gates, B, T)
        o_mla = _mla(qm, km.reshape(B, T, -1), vmt, B, T)
        wo = w_out[i]
        nsa_w = N_NSA_HEADS * NSA_D
        wn = wo[:nsa_w].reshape(NSA_GROUPS, NSA_HPG, NSA_D, D_MODEL).transpose(1, 0, 2, 3)
        wn = wn.reshape(nsa_w, D_MODEL).astype(BF16)
        wm = wo[nsa_w:].astype(BF16)
        ln = jnp.stack([ln1_g[i], ln1_b[i], ln2_g[i], ln2_b[i], ln3_g[i], ln3_b[i]])
        xc = _post(o_nsa, o_mla, xc, p[i].reshape(N, D_PLE), wn, wm, w_up[i].astype(BF16),
                   w_down[i].astype(BF16), w_ple_gate[i].astype(BF16), w_ple[i].astype(BF16), ln)
    return xc.reshape(B, T, D_MODEL)
```

```python
import functools

import jax
import jax.numpy as jnp
import numpy as np
from jax import lax
from jax.experimental import pallas as pl
from jax.experimental.pallas import tpu as pltpu

F32 = jnp.float32
BF16 = jnp.bfloat16

D_MODEL = 1024
N_NSA_HEADS = 8
NSA_GROUPS = 2
NSA_HPG = N_NSA_HEADS // NSA_GROUPS
NSA_D = 64
CMP_LEN = 32
CMP_STRIDE = 16
CMP_HIDDEN = 256
SLC_BLOCK = 64
SLC_TOP_N = 16
WINDOW = 512
N_MLA_HEADS = 8
MLA_NOPE = 64
MLA_ROPE = 32
MLA_V = 64
MLA_Q_LORA = 256
MLA_KV_LORA = 128
ROPE_THETA = 10000.0
D_FF = -(-8 * D_MODEL // (3 * 256)) * 256
D_PLE = 256
DEPTH = 1
ALPHA = (2 * DEPTH) ** 0.25
LN_EPS = 1e-5
RMS_EPS = 1e-6
NEG_INF = -1e30
REMOVED_SCORE = -3e38
LOG2_E = 1.4426950408889634
SLC_BLOCK_LOG2 = SLC_BLOCK.bit_length() - 1

LANES = 128
V_ROWS = 80
TQ_NSA = 256
R_NSA = NSA_HPG * TQ_NSA
TK_SLC = 256
BLOCKS_PER_CHUNK = TK_SLC // SLC_BLOCK
N_SLC_CHUNKS = LANES // BLOCKS_PER_CHUNK
SLC_UNROLL = 3
WIN_KEYS = WINDOW + TQ_NSA
TQ_MLA = 512
TK_MLA = 512
MLA_UNROLL = 4
MLA_DEPTH = 2
TM_PROJ = 512
TF_FFN = 256
VMEM_LIMIT = 56 * 1024 * 1024

_C_NQ = 0
_C_KC = _C_NQ + N_NSA_HEADS * NSA_D
_C_VC = _C_KC + LANES
_C_KS = _C_VC + LANES
_C_KW = _C_KS + LANES
_C_GL = _C_KW + LANES
_C_CQ = _C_GL + LANES
_C_CKV = _C_CQ + MLA_Q_LORA
_C_KPA = _C_CKV + MLA_KV_LORA
_C_END = _C_KPA + LANES


def _feat_base(g):
    return NSA_D * (1 - g)


def _nt_dot(a, b):
    return lax.dot_general(a, b, (((1,), (1,)), ((), ())), preferred_element_type=F32)


def _dot(a, b):
    return jnp.dot(a, b, preferred_element_type=F32)


def _layer_norm(v, g, b):
    mu = jnp.mean(v, axis=-1, keepdims=True)
    d = v - mu
    var = jnp.mean(d * d, axis=-1, keepdims=True)
    return d * lax.rsqrt(var + LN_EPS) * g + b


def _rms_norm(v, g):
    return v * lax.rsqrt(jnp.mean(v * v, axis=-1, keepdims=True) + RMS_EPS) * g


def _in_proj_kernel(x_ref, w_ref, wvst_ref, wvwt_ref, wqa_ref, wka_ref, wv_ref,
                    gq_ref, gkv_ref, ft0_ref, ft1_ref, cq_ref, sq_ref, ck_ref, sk_ref,
                    qn_ref, kc_ref, vc_ref, ks0_ref, ks1_ref, vst_ref, kw0_ref, kw1_ref, vwt_ref,
                    gate_ref, qm_ref, km_ref, vm_ref):
    xb = x_ref[...].astype(BF16)
    h = _dot(xb, w_ref[...])
    tm = h.shape[0]
    lane = lax.broadcasted_iota(jnp.int32, (tm, LANES), 1)
    in_half = [lane < NSA_D, lane >= NSA_D]

    def seg(c0, width):
        return h[:, c0:c0 + width]

    def swap_rotary(v):
        half = MLA_ROPE // 2
        first = (lane >= MLA_NOPE) & (lane < MLA_NOPE + half)
        second = (lane >= MLA_NOPE + half) & (lane < MLA_NOPE + MLA_ROPE)
        return jnp.where(first, pltpu.roll(v, LANES - half, 1),
                         jnp.where(second, pltpu.roll(v, half, 1), 0.0))

    for i in range(N_NSA_HEADS):
        g = i // NSA_HPG
        src = seg(_C_NQ + (i // 2) * LANES, LANES)
        if i % 2 != g:
            src = pltpu.roll(src, NSA_D, 1)
        qn_ref[0, i] = (jnp.where(in_half[g], src, 0.0) * (NSA_D ** -0.5)).astype(BF16)
    kc_ref[...] = seg(_C_KC, LANES)
    vc_ref[...] = seg(_C_VC, LANES)
    h_ks, h_kw = seg(_C_KS, LANES), seg(_C_KW, LANES)
    ks0_ref[...] = (jnp.where(in_half[0], h_ks, 0.0) + ft0_ref[...]).astype(BF16)
    ks1_ref[...] = (jnp.where(in_half[1], h_ks, 0.0) + ft1_ref[...]).astype(BF16)
    kw0_ref[...] = (jnp.where(in_half[0], h_kw, 0.0) + ft0_ref[...]).astype(BF16)
    kw1_ref[...] = (jnp.where(in_half[1], h_kw, 0.0) + ft1_ref[...]).astype(BF16)
    def ones_rows(n):
        r = lax.broadcasted_iota(jnp.int32, (n, 1), 0) & (LANES - 1)
        return jnp.where(r == NSA_D, 1.0, 0.0)

    vst_ref[0] = (_nt_dot(wvst_ref[...], xb) + ones_rows(NSA_GROUPS * LANES)).astype(BF16)
    vwt_ref[0] = (_nt_dot(wvwt_ref[...], xb) + ones_rows(NSA_GROUPS * LANES)).astype(BF16)
    gate_ref[...] = jax.nn.sigmoid(seg(_C_GL, LANES))

    cqn = _rms_norm(seg(_C_CQ, MLA_Q_LORA), gq_ref[...]).astype(BF16)
    cq_t = cq_ref[...]
    sq_t = sq_ref[...]
    qa = _dot(cqn, wqa_ref[...])
    for hd in range(N_MLA_HEADS):
        hs = slice(hd * LANES, (hd + 1) * LANES)
        qm_ref[:, hs] = (qa[:, hs] * cq_t + swap_rotary(qa[:, hs]) * sq_t).astype(BF16)

    kvn = _rms_norm(seg(_C_CKV, MLA_KV_LORA), gkv_ref[...]).astype(BF16)
    kpa = seg(_C_KPA, LANES)
    kpe = kpa * ck_ref[...] + swap_rotary(kpa) * sk_ref[...]
    ka = _dot(kvn, wka_ref[...])
    for hd in range(N_MLA_HEADS):
        hs = slice(hd * LANES, (hd + 1) * LANES)
        km_ref[:, hs] = (ka[:, hs] + kpe).astype(BF16)
    vm_ref[0] = (_nt_dot(wv_ref[...], kvn) + ones_rows(N_MLA_HEADS * LANES)).astype(BF16)


def _in_proj(x2, w_all, wvst, wvwt, wqa, wka, wv, gq, gkv, ft0, ft1, cq_t, sq_t, ck_t, sk_t,
             B, T):
    N = B * T
    tm = TM_PROJ
    tpb = T // tm
    full = lambda a: pl.BlockSpec(a.shape, lambda i: (0,) * a.ndim)
    rows = lambda w: pl.BlockSpec((tm, w), lambda i: (i, 0))
    tab = pl.BlockSpec((tm, LANES), lambda i: (i % tpb, 0))
    cols_t = lambda w: pl.BlockSpec((1, w, tm), lambda i: (i // tpb, 0, i % tpb))
    tok = lambda dt: jax.ShapeDtypeStruct((N, LANES), dt)
    out_shape = (
        jax.ShapeDtypeStruct((B, N_NSA_HEADS, T, LANES), BF16),
        tok(F32), tok(F32),
        tok(BF16), tok(BF16),
        jax.ShapeDtypeStruct((B, NSA_GROUPS * LANES, T), BF16),
        tok(BF16), tok(BF16),
        jax.ShapeDtypeStruct((B, NSA_GROUPS * LANES, T), BF16),
        tok(F32),
        jax.ShapeDtypeStruct((N, N_MLA_HEADS * LANES), BF16),
        jax.ShapeDtypeStruct((N, N_MLA_HEADS * LANES), BF16),
        jax.ShapeDtypeStruct((B, N_MLA_HEADS * LANES, T), BF16),
    )
    out_specs = (
        pl.BlockSpec((1, N_NSA_HEADS, tm, LANES), lambda i: (i // tpb, 0, i % tpb, 0)),
        rows(LANES), rows(LANES), rows(LANES), rows(LANES), cols_t(NSA_GROUPS * LANES),
        rows(LANES), rows(LANES), cols_t(NSA_GROUPS * LANES), rows(LANES),
        rows(N_MLA_HEADS * LANES), rows(N_MLA_HEADS * LANES), cols_t(N_MLA_HEADS * LANES),
    )
    return pl.pallas_call(
        _in_proj_kernel,
        grid=(N // tm,),
        in_specs=[rows(D_MODEL), full(w_all), full(wvst), full(wvwt), full(wqa),
                  full(wka), full(wv), full(gq), full(gkv), tab, tab, tab, tab, tab, tab],
        out_specs=out_specs,
        out_shape=out_shape,
        compiler_params=pltpu.CompilerParams(dimension_semantics=("arbitrary",),
                                             vmem_limit_bytes=VMEM_LIMIT),
        name="in_proj",
    )(x2, w_all, wvst, wvwt, wqa, wka, wv, gq, gkv, ft0, ft1, cq_t, sq_t, ck_t, sk_t)


def _compress_kernel(kc_ref, vc_ref, pk_ref, pv_ref, wk_lo_ref, wk_hi_ref, wk2_ref,
                     wv_lo_ref, wv_hi_ref, wv2t_ref, ko_ref, vo_ref):
    nsub = kc_ref.shape[1]

    def hidden(src_ref, pos_ref, lo_ref, hi_ref):
        a = src_ref[0]
        pos = pos_ref[...]
        p0 = _dot((a + pos[0:1]).astype(BF16), lo_ref[...])
        p1 = _dot((a + pos[1:2]).astype(BF16), hi_ref[...])
        hid = p0 + pltpu.roll(p1, nsub - 1, 0)
        return jax.nn.gelu(hid).astype(BF16)

    k_out = _dot(hidden(kc_ref, pk_ref, wk_lo_ref, wk_hi_ref), wk2_ref[...])
    row = lax.broadcasted_iota(jnp.int32, k_out.shape, 0)
    ko_ref[0] = jnp.where(row < nsub - 1, k_out, 0.0).astype(BF16)
    v_out = _nt_dot(wv2t_ref[...], hidden(vc_ref, pv_ref, wv_lo_ref, wv_hi_ref))
    col = lax.broadcasted_iota(jnp.int32, v_out.shape, 1)
    vo_ref[0] = jnp.where(col < nsub - 1, v_out, 0.0).astype(BF16)


def _compress(kc3, vc3, pk, pv, wk_lo, wk_hi, wk2, wv_lo, wv_hi, wv2t):
    B, nsub, width = kc3.shape
    full = lambda a: pl.BlockSpec(a.shape, lambda b: (0,) * a.ndim)
    per_b = pl.BlockSpec((1, nsub, width), lambda b: (b, 0, 0))
    return pl.pallas_call(
        _compress_kernel,
        grid=(B,),
        in_specs=[per_b, per_b, full(pk), full(pv), full(wk_lo), full(wk_hi), full(wk2),
                  full(wv_lo), full(wv_hi), full(wv2t)],
        out_specs=(pl.BlockSpec((1, nsub, LANES), lambda b: (b, 0, 0)),
                   pl.BlockSpec((1, LANES, nsub), lambda b: (b, 0, 0))),
        out_shape=(jax.ShapeDtypeStruct((B, nsub, LANES), BF16),
                   jax.ShapeDtypeStruct((B, LANES, nsub), BF16)),
        compiler_params=pltpu.CompilerParams(dimension_semantics=("arbitrary",),
                                             vmem_limit_bytes=VMEM_LIMIT),
        name="nsa_compress",
    )(kc3, vc3, pk, pv, wk_lo, wk_hi, wk2, wv_lo, wv_hi, wv2t)


def _nsa_kernel(qn_ref, kcmp_ref, vcmpt_ref, cfeat_ref, ks0_ref, ks1_ref, vst_ref,
                kw0_ref, kw1_ref, vwt_ref, onehot_ref, band_ref, mt_ref, qf_ref, gate_ref,
                o_ref, qaug_ref, lst_ref, m_ref, acc_ref, *, n_top):
    qb = pl.program_id(1)
    q0 = qb * TQ_NSA
    R = R_NSA
    ncmp = kcmp_ref.shape[1]

    col = lax.broadcasted_iota(jnp.int32, (1, R), 1)
    t_row = q0 + (col & (TQ_NSA - 1))
    gates_t = gate_ref[...].T
    c_diag = qb // (TK_SLC // TQ_NSA)
    kl_col = lax.broadcasted_iota(jnp.int32, (TK_SLC, 1), 0)
    w0 = pl.multiple_of(q0, TQ_NSA)
    qb_f = qb.astype(F32)

    q_feats, o_wins, o_cmps, imps = [], [], [], []
    for g in range(NSA_GROUPS):
        kw_ref = (kw0_ref, kw1_ref)[g]
        q = qn_ref[0, g * NSA_HPG:(g + 1) * NSA_HPG].reshape(R, LANES)
        q_feat = (q.astype(F32) + qf_ref[g, 0] + qb_f * qf_ref[g, 1]).astype(BF16)
        q_feats.append(q_feat)

        v_rows = slice(g * LANES, g * LANES + V_ROWS)

        sw = _nt_dot(kw_ref[0, pl.ds(w0, WIN_KEYS), :], q_feat) + band_ref[...]
        pw = jnp.exp((sw - jnp.max(sw, axis=0, keepdims=True)).astype(BF16))
        o_win = _dot(vwt_ref[0, v_rows, pl.ds(w0, WIN_KEYS)], pw)
        o_wins.append(o_win[0:NSA_D, :] / o_win[NSA_D:NSA_D + 1, :])

        kc_aug = jnp.concatenate([kcmp_ref[0], cfeat_ref[...]], axis=1)
        q_cmp = jnp.concatenate([q, (qf_ref[g, 2] + qb_f * qf_ref[g, 3]).astype(BF16)], axis=1)
        cend = lax.broadcasted_iota(jnp.int32, (ncmp, 1), 0) * CMP_STRIDE + (CMP_LEN - 1)
        cmask = cend <= t_row
        sc = jnp.where(cmask, _nt_dot(kc_aug, q_cmp), NEG_INF)
        e = jnp.exp(sc - jnp.maximum(jnp.max(sc, axis=0, keepdims=True), 0.1 * NEG_INF))
        den = jnp.sum(e, axis=0, keepdims=True)
        p_cmp = e * jnp.where(den > 0.0, 1.0 / den, 0.0)
        o_cmps.append(_dot(vcmpt_ref[0, g * NSA_D:(g + 1) * NSA_D, :], p_cmp.astype(BF16)))

        p4 = p_cmp[:, 0:TQ_NSA]
        for h in range(1, NSA_HPG):
            p4 = p4 + p_cmp[:, h * TQ_NSA:(h + 1) * TQ_NSA]
        hi = p4.astype(BF16)
        r1 = p4 - hi.astype(F32)
        mid = r1.astype(BF16)
        lo = (r1 - mid.astype(F32)).astype(BF16)
        mt = mt_ref[...]
        imps.append(_dot(mt, hi) + _dot(mt, mid) + _dot(mt, lo))

    imp = jnp.concatenate(imps, axis=1)
    width = NSA_GROUPS * TQ_NSA
    jb = lax.broadcasted_iota(jnp.int32, (LANES, width), 0)
    tl2 = lax.broadcasted_iota(jnp.int32, (LANES, width), 1) & (TQ_NSA - 1)
    cur = (q0 + tl2) >> SLC_BLOCK_LOG2
    forced = (jb == 0) | (jb == cur) | (jb == cur - 1)
    future = jb > cur
    jf = jb.astype(F32)
    rest = jnp.where(forced, REMOVED_SCORE, jnp.where(future, -1.0, imp))
    picked = forced
    for _ in range(max(n_top - 3, 0)):
        mx = jnp.max(rest, axis=0, keepdims=True)
        first = jnp.min(jnp.where(rest == mx, jf, float(LANES)), axis=0, keepdims=True)
        hit = jf == first
        picked = picked | hit
        rest = jnp.where(hit, REMOVED_SCORE, rest)
    allowed2 = jnp.where(picked & jnp.logical_not(future), 1.0, 0.0)

    group_out = []
    for g in range(NSA_GROUPS):
        ks_ref = (ks0_ref, ks1_ref)[g]
        v_rows = slice(g * LANES, g * LANES + V_ROWS)
        o_cmp, o_win = o_cmps[g], o_wins[g]
        allowed = allowed2[:, g * TQ_NSA:(g + 1) * TQ_NSA]
        selneg_t = jnp.where(allowed > 0.0, 0.0, NEG_INF).T.astype(BF16)

        qaug_ref[:, 0:LANES] = q_feats[g]
        for h in range(NSA_HPG):
            qaug_ref[h * TQ_NSA:(h + 1) * TQ_NSA, LANES:2 * LANES] = selneg_t

        count = jnp.int32(0)
        for c in range(N_SLC_CHUNKS):
            used = jnp.max(allowed[c * BLOCKS_PER_CHUNK:(c + 1) * BLOCKS_PER_CHUNK, :]) > 0.0
            lst_ref[count] = jnp.int32(c)
            count = count + (used & (c < c_diag)).astype(jnp.int32)

        def slc_scores(c, causal, ks_ref=ks_ref):
            k0 = pl.multiple_of(c * TK_SLC, TK_SLC)
            k_aug = jnp.concatenate([ks_ref[0, pl.ds(k0, TK_SLC), :],
                                     onehot_ref[pl.ds(k0, TK_SLC), :]], axis=1)
            st = _nt_dot(k_aug, qaug_ref[...])
            if causal:
                st = jnp.where(k0 + kl_col <= t_row, st, NEG_INF)
            return st, jnp.max(st, axis=0, keepdims=True)

        def absorb(tile, c, v_rows=v_rows):
            st, mx = tile
            k0 = pl.multiple_of(c * TK_SLC, TK_SLC)
            m_old = m_ref[...]
            m_new = jnp.maximum(m_old, mx)
            p = jnp.exp((st - m_new).astype(BF16))
            pv = _dot(vst_ref[0, v_rows, pl.ds(k0, TK_SLC)], p)
            acc_ref[...] = jnp.exp(m_old - m_new) * acc_ref[...] + pv
            m_ref[...] = m_new

        def steps(chunks, causal=False, slc_scores=slc_scores, absorb=absorb):
            tiles = [slc_scores(c, causal) for c in chunks]
            for c, tile in zip(chunks, tiles):
                absorb(tile, c)

        m_ref[...] = jnp.full((1, R), NEG_INF, F32)
        acc_ref[...] = jnp.zeros((V_ROWS, R), F32)
        steps([c_diag], causal=True)

        def body(i, carry, steps=steps):
            steps([lst_ref[SLC_UNROLL * i + j] for j in range(SLC_UNROLL)])
            return carry

        n_loop = count // SLC_UNROLL
        lax.fori_loop(0, n_loop, body, 0)
        for j in range(SLC_UNROLL - 1):
            @pl.when(count - n_loop * SLC_UNROLL > j)
            def _(j=j, steps=steps):
                steps([lst_ref[n_loop * SLC_UNROLL + j]])
        o_slc = acc_ref[0:NSA_D, :] / acc_ref[NSA_D:NSA_D + 1, :]

        def gate_row(branch):
            rows = [gates_t[(g * NSA_HPG + h) * 3 + branch:(g * NSA_HPG + h) * 3 + branch + 1, :]
                    for h in range(NSA_HPG)]
            return jnp.concatenate(rows, axis=1)

        group_out.append(gate_row(0) * o_cmp + gate_row(1) * o_slc + gate_row(2) * o_win)

    out_t = jnp.concatenate(group_out, axis=0)
    for h in range(NSA_HPG):
        o_ref[:, h * LANES:(h + 1) * LANES] = out_t[:, h * TQ_NSA:(h + 1) * TQ_NSA].T.astype(BF16)


def _nsa(qn, kcmp, vcmpt, cfeat, ks0, ks1, vst, kw0, kw1, vwt, onehot, band, mt, qf, gates, B, T):
    nq = T // TQ_NSA
    n_top = min(SLC_TOP_N, T // SLC_BLOCK)
    per_b = lambda a: pl.BlockSpec((1,) + a.shape[1:], lambda b, i: (b,) + (0,) * (a.ndim - 1),
                                   pipeline_mode=pl.Buffered(1))
    full = lambda a: pl.BlockSpec(a.shape, lambda b, i: (0,) * a.ndim,
                                  pipeline_mode=pl.Buffered(1))
    return pl.pallas_call(
        functools.partial(_nsa_kernel, n_top=n_top),
        grid=(B, nq),
        in_specs=[pl.BlockSpec((1, N_NSA_HEADS, TQ_NSA, LANES), lambda b, i: (b, 0, i, 0)),
                  per_b(kcmp), per_b(vcmpt), full(cfeat), per_b(ks0), per_b(ks1), per_b(vst),
                  per_b(kw0), per_b(kw1), per_b(vwt), full(onehot), full(band), full(mt), full(qf),
                  pl.BlockSpec((TQ_NSA, LANES), lambda b, i: (b * nq + i, 0))],
        out_specs=pl.BlockSpec((TQ_NSA, NSA_HPG * LANES), lambda b, i: (b * nq + i, 0)),
        out_shape=jax.ShapeDtypeStruct((B * T, NSA_HPG * LANES), BF16),
        scratch_shapes=[pltpu.VMEM((R_NSA, 2 * LANES), BF16),
                        pltpu.SMEM((N_SLC_CHUNKS + 1,), jnp.int32),
                        pltpu.VMEM((1, R_NSA), F32), pltpu.VMEM((V_ROWS, R_NSA), F32)],
        compiler_params=pltpu.CompilerParams(dimension_semantics=("arbitrary", "arbitrary"),
                                             vmem_limit_bytes=VMEM_LIMIT),
        name="nsa_attention",
    )(qn, kcmp, vcmpt, cfeat, ks0, ks1, vst, kw0, kw1, vwt, onehot, band, mt, qf, gates)


def _mla_kernel(q_ref, k_ref, vt_ref, o_ref, m_ref, acc_ref):
    qi = pl.program_id(2)
    q0 = qi * TQ_MLA
    tpos = q0 + lax.broadcasted_iota(jnp.int32, (1, TQ_MLA), 1)
    kl = lax.broadcasted_iota(jnp.int32, (TK_MLA, 1), 0)
    qs = [q_ref[:, hh * LANES:(hh + 1) * LANES] for hh in range(2)]

    def qk(hh, c, causal):
        k0 = pl.multiple_of(c * TK_MLA, TK_MLA)
        st = _nt_dot(k_ref[0, pl.ds(k0, TK_MLA), hh * LANES:(hh + 1) * LANES], qs[hh])
        if causal:
            st = jnp.where(k0 + kl <= tpos, st, NEG_INF)
        return st, jnp.max(st, axis=0, keepdims=True)

    def absorb(hh, tile, c):
        st, mx = tile
        k0 = pl.multiple_of(c * TK_MLA, TK_MLA)
        m_old = m_ref[hh]
        m_new = jnp.maximum(m_old, mx)
        p = jnp.exp2((st - m_new).astype(BF16))
        pv = _dot(vt_ref[0, hh * LANES:(hh + 1) * LANES, pl.ds(k0, TK_MLA)], p)
        acc_ref[hh] = jnp.exp2(m_old - m_new) * acc_ref[hh] + pv
        m_ref[hh] = m_new

    def steps(chunks, causal=False):
        work = [(hh, c) for c in chunks for hh in range(2)]
        tiles = [qk(hh, c, causal) for hh, c in work[:MLA_DEPTH]]
        for i, (hh, c) in enumerate(work):
            absorb(hh, tiles[i], c)
            if i + MLA_DEPTH < len(work):
                nh, nc = work[i + MLA_DEPTH]
                tiles.append(qk(nh, nc, causal))

    m_ref[...] = jnp.full(m_ref.shape, NEG_INF, F32)
    acc_ref[...] = jnp.zeros(acc_ref.shape, F32)
    c_diag = qi // (TK_MLA // TQ_MLA)
    steps([c_diag], causal=True)

    def body(i, carry):
        steps([MLA_UNROLL * i + j for j in range(MLA_UNROLL)])
        return carry

    n_loop = c_diag // MLA_UNROLL
    lax.fori_loop(0, n_loop, body, 0)
    for j in range(MLA_UNROLL - 1):
        @pl.when(c_diag - n_loop * MLA_UNROLL > j)
        def _(j=j):
            steps([n_loop * MLA_UNROLL + j])

    outs = [acc_ref[hh, 0:MLA_V, :] / acc_ref[hh, MLA_V:MLA_V + 1, :] for hh in range(2)]
    o_ref[...] = jnp.concatenate(outs, axis=0).T.astype(BF16)


def _mla(qm, km3, vmt, B, T):
    nq = T // TQ_MLA
    npair = N_MLA_HEADS // 2
    return pl.pallas_call(
        _mla_kernel,
        grid=(B, npair, nq),
        in_specs=[pl.BlockSpec((TQ_MLA, 2 * LANES), lambda b, h, i: (b * nq + i, h)),
                  pl.BlockSpec((1, T, 2 * LANES), lambda b, h, i: (b, 0, h)),
                  pl.BlockSpec((1, 2 * LANES, T), lambda b, h, i: (b, h, 0))],
        out_specs=pl.BlockSpec((TQ_MLA, LANES), lambda b, h, i: (b * nq + i, h)),
        out_shape=jax.ShapeDtypeStruct((B * T, npair * LANES), BF16),
        scratch_shapes=[pltpu.VMEM((2, 1, TQ_MLA), F32), pltpu.VMEM((2, LANES, TQ_MLA), F32)],
        compiler_params=pltpu.CompilerParams(
            dimension_semantics=("arbitrary", "arbitrary", "arbitrary"),
            vmem_limit_bytes=VMEM_LIMIT,
            ),
        name="mla_attention",
    )(qm, km3, vmt)


def _post_kernel(on_ref, om_ref, x_ref, p_ref, wn_ref, wm_ref, wup_ref, wd_ref,
                 wpg_ref, wp_ref, ln_ref, o_ref):
    mix = _dot(on_ref[...], wn_ref[...]) + _dot(om_ref[...], wm_ref[...])
    x1 = _layer_norm(ALPHA * x_ref[...] + mix, ln_ref[0:1, :], ln_ref[1:2, :])
    xb = x1.astype(BF16)
    ffn = jnp.zeros_like(x1)
    for c in range(D_FF // TF_FFN):
        cols = slice(c * TF_FFN, (c + 1) * TF_FFN)
        up_cols = slice(D_FF + c * TF_FFN, D_FF + (c + 1) * TF_FFN)
        hid = jax.nn.silu(_dot(xb, wup_ref[:, cols])) * _dot(xb, wup_ref[:, up_cols])
        ffn = ffn + _dot(hid.astype(BF16), wd_ref[cols, :])
    x2 = _layer_norm(ALPHA * x1 + ffn, ln_ref[2:3, :], ln_ref[3:4, :])
    gate = jax.nn.sigmoid(_dot(x2.astype(BF16), wpg_ref[...]))
    ple = gate * _dot(p_ref[...].astype(BF16), wp_ref[...])
    o_ref[...] = _layer_norm(ALPHA * x2 + ple, ln_ref[4:5, :], ln_ref[5:6, :])


def _post(o_nsa, o_mla, x2, p2, wn, wm, w_up, w_down, wpg, wp, ln):
    N = x2.shape[0]
    tm = TM_PROJ
    rows = lambda w: pl.BlockSpec((tm, w), lambda i: (i, 0))
    full = lambda a: pl.BlockSpec(a.shape, lambda i: (0,) * a.ndim)
    return pl.pallas_call(
        _post_kernel,
        grid=(N // tm,),
        in_specs=[rows(o_nsa.shape[1]), rows(o_mla.shape[1]), rows(D_MODEL), rows(D_PLE),
                  full(wn), full(wm), full(w_up), full(w_down), full(wpg), full(wp), full(ln)],
        out_specs=rows(D_MODEL),
        out_shape=jax.ShapeDtypeStruct((N, D_MODEL), F32),
        compiler_params=pltpu.CompilerParams(dimension_semantics=("arbitrary",),
                                             vmem_limit_bytes=VMEM_LIMIT),
        name="post_attention",
    )(o_nsa, o_mla, x2, p2, wn, wm, w_up, w_down, wpg, wp, ln)


def _arrange_in_weights(w_in):
    splits = (N_NSA_HEADS * NSA_D,) + (NSA_GROUPS * NSA_D,) * 6 + (3 * N_NSA_HEADS, MLA_Q_LORA,
                                                                  MLA_KV_LORA, MLA_ROPE)
    offs = np.cumsum((0,) + splits)
    part = lambda i: w_in[:, offs[i]:offs[i + 1]]

    def place(w, lane0):
        return jnp.pad(w, ((0, 0), (lane0, LANES - lane0 - w.shape[1])))

    cols = [part(0), part(1), part(2), part(3), part(5), place(part(7), 0), part(8), part(9),
            place(part(10), MLA_NOPE)]
    w_all = jnp.concatenate(cols, axis=1)
    assert w_all.shape[1] == _C_END

    def value_rows(w):
        return jnp.concatenate([place(w[:, g * NSA_D:(g + 1) * NSA_D], 0)
                                for g in range(NSA_GROUPS)], axis=1).T

    return w_all.astype(BF16), value_rows(part(4)).astype(BF16), value_rows(part(6)).astype(BF16)


def _arrange_mla_weights(w_uq, w_ukv):
    wq = w_uq.reshape(MLA_Q_LORA, N_MLA_HEADS, MLA_NOPE + MLA_ROPE)

    def place(w, lane0):
        return jnp.pad(w, ((0, 0), (0, 0), (lane0, LANES - lane0 - w.shape[2])))

    wqa = place(wq, 0)
    wkv = w_ukv.reshape(MLA_KV_LORA, N_MLA_HEADS, MLA_NOPE + MLA_V)
    wka = place(wkv[:, :, :MLA_NOPE], 0)
    wv = place(wkv[:, :, MLA_NOPE:], 0)
    flat = lambda a: a.reshape(a.shape[0], -1).astype(BF16)
    return flat(wqa), flat(wka), flat(wv).T


def _rope_tables(T):
    half = MLA_ROPE // 2
    pos = jnp.arange(T, dtype=F32)
    inv_freq = ROPE_THETA ** (-jnp.arange(half, dtype=F32) / half)
    ang = pos[:, None] * inv_freq[None, :]
    cos, sin = jnp.cos(ang), jnp.sin(ang)
    scale = (MLA_NOPE + MLA_ROPE) ** -0.5 * LOG2_E
    ones = jnp.ones((T, MLA_NOPE), F32)
    tail = jnp.zeros((T, LANES - MLA_NOPE - MLA_ROPE), F32)
    cos_row = lambda head: jnp.concatenate([head, cos, cos, tail], axis=1)
    sin_row = jnp.concatenate([0.0 * ones, -sin, sin, tail], axis=1)
    return cos_row(ones) * scale, sin_row * scale, cos_row(0.0 * ones), sin_row


def _compress_weights(w1, w2, pos):
    G, D = NSA_GROUPS, NSA_D
    w1r = w1.reshape(CMP_LEN, D, CMP_HIDDEN)
    eye = jnp.eye(G, dtype=bool)
    halves = []
    for j in range(CMP_LEN // CMP_STRIDE):
        part = w1r[j * CMP_STRIDE:(j + 1) * CMP_STRIDE]
        wide = jnp.where(eye[None, :, None, :, None], part[:, None, :, None, :], 0.0)
        halves.append(wide.reshape(CMP_STRIDE * G * D, G * CMP_HIDDEN).astype(BF16))
    w2bd = jnp.where(eye[:, None, :, None], w2[None, :, None, :], 0.0)
    w2bd = w2bd.reshape(G * CMP_HIDDEN, G * D).astype(BF16)
    posr = pos.reshape(CMP_LEN // CMP_STRIDE, CMP_STRIDE, 1, D)
    posw = jnp.broadcast_to(posr, (CMP_LEN // CMP_STRIDE, CMP_STRIDE, G, D)).reshape(
        CMP_LEN // CMP_STRIDE, CMP_STRIDE * G * D)
    return halves[0], halves[1], w2bd, posw


def _position_features(pos):
    return np.stack([(pos // LANES) * LANES, pos % LANES, np.ones_like(pos), np.ones_like(pos)],
                    axis=1).astype(np.float32)


def _nsa_tables(T):
    kpos = np.arange(T)
    onehot = np.zeros((T, LANES), np.float32)
    onehot[kpos, kpos // SLC_BLOCK] = 1.0
    key_feats, pad_rows = [], []
    for g in range(NSA_GROUPS):
        base = _feat_base(g)
        ft = np.zeros((T, LANES), np.float32)
        ft[:, base:base + 4] = _position_features(kpos)
        key_feats.append(jnp.asarray(ft))
        pad = np.zeros((WINDOW, LANES), np.float32)
        pad[:, base + 4] = 1.0
        pad_rows.append(jnp.asarray(pad, BF16))
    nsub = T // CMP_STRIDE
    c = np.arange(nsub)
    cfeat = np.zeros((nsub, LANES), np.float32)
    cfeat[:, 0:4] = _position_features(c * CMP_STRIDE + CMP_LEN - 1)
    kl = np.arange(WIN_KEYS)[:, None]
    tl = np.arange(TQ_NSA)[None, :]
    band = np.where((kl > tl) & (kl <= tl + WINDOW), 0.0, NEG_INF).astype(np.float32)
    band = np.tile(band, (1, NSA_HPG))
    ratio = SLC_BLOCK // CMP_STRIDE
    mt = np.zeros((LANES, nsub), np.float32)
    valid = c < nsub - 1
    for j in range(CMP_LEN // CMP_STRIDE):
        np.add.at(mt, ((c[valid] + j) // ratio, c[valid]), 1.0)
    qf = np.zeros((NSA_GROUPS, 4, R_NSA, LANES), np.float32)
    tl_rows = np.tile(np.arange(TQ_NSA), NSA_HPG).astype(np.float32)
    for g in range(NSA_GROUPS):
        slope = np.repeat(2.0 ** -(g * NSA_HPG + np.arange(NSA_HPG) + 1.0), TQ_NSA).astype(np.float32)
        for k, base in enumerate((_feat_base(g), 0)):
            qf[g, 2 * k, :, base] = slope
            qf[g, 2 * k, :, base + 1] = slope
            qf[g, 2 * k, :, base + 3] = -slope * tl_rows
            qf[g, 2 * k, :, base + 4] = NEG_INF
            qf[g, 2 * k + 1, :, base + 2] = -slope * TQ_NSA
    return (jnp.asarray(onehot, BF16), key_feats, pad_rows, jnp.asarray(cfeat, BF16),
            jnp.asarray(band), jnp.asarray(mt, BF16), jnp.asarray(qf))


def kernel(x, p, w_in, w_ck1, w_ck2, pos_ck, w_cv1, w_cv2, pos_cv, mla_q_norm, w_uq, mla_kv_norm,
           w_ukv, w_out, ln1_g, ln1_b, w_up, w_down, ln2_g, ln2_b, w_ple_gate, w_ple, ln3_g, ln3_b):
    B, T, _ = x.shape
    N = B * T
    assert T % TQ_MLA == 0 and T % TM_PROJ == 0 and T // SLC_BLOCK <= LANES
    cq_t, sq_t, ck_t, sk_t = _rope_tables(T)
    onehot, key_feats, pad_rows, cfeat, band, mt, qf = _nsa_tables(T)
    row2 = lambda v: v.reshape(1, -1)
    xc = x.reshape(N, D_MODEL)
    for i in range(DEPTH):
        w_all, wvst, wvwt = _arrange_in_weights(w_in[i])
        wqa, wka, wv = _arrange_mla_weights(w_uq[i], w_ukv[i])
        (qn, kc, vc, ks0, ks1, vst, kw0, kw1, vwt, gates, qm, km, vmt) = _in_proj(
            xc, w_all, wvst, wvwt, wqa, wka, wv, row2(mla_q_norm[i]), row2(mla_kv_norm[i]),
            key_feats[0], key_feats[1], cq_t, sq_t, ck_t, sk_t, B, T)
        wk_lo, wk_hi, wk2, pk = _compress_weights(w_ck1[i], w_ck2[i], pos_ck[i])
        wv_lo, wv_hi, wv2, pv = _compress_weights(w_cv1[i], w_cv2[i], pos_cv[i])
        nsub = T // CMP_STRIDE
        kcmp, vcmpt = _compress(kc.reshape(B, nsub, CMP_STRIDE * LANES),
                                vc.reshape(B, nsub, CMP_STRIDE * LANES),
                                pk, pv, wk_lo, wk_hi, wk2, wv_lo, wv_hi, wv2.T)
        pad_k = lambda a, g: jnp.concatenate(
            [jnp.broadcast_to(pad_rows[g], (B, WINDOW, LANES)), a.reshape(B, T, LANES)], axis=1)
        o_nsa = _nsa(qn, kcmp, vcmpt, cfeat, ks0.reshape(B, T, LANES), ks1.reshape(B, T, LANES), vst,
                     pad_k(kw0, 0), pad_k(kw1, 1), jnp.pad(vwt, ((0, 0), (0, 0), (WINDOW, 0))),
                     onehot, band, mt, qf, gates, B, T)
        o_mla = _mla(qm, km.reshape(B, T, -1), vmt, B, T)
        wo = w_out[i]
        nsa_w = N_NSA_HEADS * NSA_D
        wn = wo[:nsa_w].reshape(NSA_GROUPS, NSA_HPG, NSA_D, D_MODEL).transpose(1, 0, 2, 3)
        wn = wn.reshape(nsa_w, D_MODEL).astype(BF16)
        wm = wo[nsa_w:].astype(BF16)
        ln = jnp.stack([ln1_g[i], ln1_b[i], ln2_g[i], ln2_b[i], ln3_g[i], ln3_b[i]])
        xc = _post(o_nsa, o_mla, xc, p[i].reshape(N, D_PLE), wn, wm, w_up[i].astype(BF16),
                   w_down[i].astype(BF16), w_ple_gate[i].astype(BF16), w_ple[i].astype(BF16), ln)
    return xc.reshape(B, T, D_MODEL)
```

```python
import functools

import jax
import jax.numpy as jnp
import numpy as np
from jax import lax
from jax.experimental import pallas as pl
from jax.experimental.pallas import tpu as pltpu

F32 = jnp.float32
BF16 = jnp.bfloat16

D_MODEL = 1024
N_NSA_HEADS = 8
NSA_GROUPS = 2
NSA_HPG = N_NSA_HEADS // NSA_GROUPS
NSA_D = 64
CMP_LEN = 32
CMP_STRIDE = 16
CMP_HIDDEN = 256
SLC_BLOCK = 64
SLC_TOP_N = 16
WINDOW = 512
N_MLA_HEADS = 8
MLA_NOPE = 64
MLA_ROPE = 32
MLA_V = 64
MLA_Q_LORA = 256
MLA_KV_LORA = 128
ROPE_THETA = 10000.0
D_FF = -(-8 * D_MODEL // (3 * 256)) * 256
D_PLE = 256
DEPTH = 1
ALPHA = (2 * DEPTH) ** 0.25
LN_EPS = 1e-5
RMS_EPS = 1e-6
NEG_INF = -1e30
REMOVED_SCORE = -3e38
LOG2_E = 1.4426950408889634
SLC_BLOCK_LOG2 = SLC_BLOCK.bit_length() - 1

LANES = 128
V_ROWS = 80
TQ_NSA = 256
R_NSA = NSA_HPG * TQ_NSA
TK_SLC = 256
BLOCKS_PER_CHUNK = TK_SLC // SLC_BLOCK
N_SLC_CHUNKS = LANES // BLOCKS_PER_CHUNK
SLC_UNROLL = 2
WIN_KEYS = WINDOW + TQ_NSA
TQ_MLA = 512
TK_MLA = 512
MLA_UNROLL = 4
MLA_DEPTH = 2
TM_PROJ = 512
TF_FFN = 256
VMEM_LIMIT = 56 * 1024 * 1024

_C_NQ = 0
_C_KC = _C_NQ + N_NSA_HEADS * NSA_D
_C_VC = _C_KC + LANES
_C_KS = _C_VC + LANES
_C_KW = _C_KS + LANES
_C_GL = _C_KW + LANES
_C_CQ = _C_GL + LANES
_C_CKV = _C_CQ + MLA_Q_LORA
_C_KPA = _C_CKV + MLA_KV_LORA
_C_END = _C_KPA + LANES


def _feat_base(g):
    return NSA_D * (1 - g)


def _nt_dot(a, b):
    return lax.dot_general(a, b, (((1,), (1,)), ((), ())), preferred_element_type=F32)


def _dot(a, b):
    return jnp.dot(a, b, preferred_element_type=F32)


def _layer_norm(v, g, b):
    mu = jnp.mean(v, axis=-1, keepdims=True)
    d = v - mu
    var = jnp.mean(d * d, axis=-1, keepdims=True)
    return d * lax.rsqrt(var + LN_EPS) * g + b


def _rms_norm(v, g):
    return v * lax.rsqrt(jnp.mean(v * v, axis=-1, keepdims=True) + RMS_EPS) * g


def _in_proj_kernel(x_ref, w_ref, wvst_ref, wvwt_ref, wqa_ref, wka_ref, wv_ref,
                    gq_ref, gkv_ref, ft0_ref, ft1_ref, cq_ref, sq_ref, ck_ref, sk_ref,
                    qn_ref, kc_ref, vc_ref, ks0_ref, ks1_ref, vst_ref, kw0_ref, kw1_ref, vwt_ref,
                    gate_ref, qm_ref, km_ref, vm_ref):
    xb = x_ref[...].astype(BF16)
    h = _dot(xb, w_ref[...])
    tm = h.shape[0]
    lane = lax.broadcasted_iota(jnp.int32, (tm, LANES), 1)
    in_half = [lane < NSA_D, lane >= NSA_D]

    def seg(c0, width):
        return h[:, c0:c0 + width]

    def swap_rotary(v):
        half = MLA_ROPE // 2
        first = (lane >= MLA_NOPE) & (lane < MLA_NOPE + half)
        second = (lane >= MLA_NOPE + half) & (lane < MLA_NOPE + MLA_ROPE)
        return jnp.where(first, pltpu.roll(v, LANES - half, 1),
                         jnp.where(second, pltpu.roll(v, half, 1), 0.0))

    for i in range(N_NSA_HEADS):
        g = i // NSA_HPG
        src = seg(_C_NQ + (i // 2) * LANES, LANES)
        if i % 2 != g:
            src = pltpu.roll(src, NSA_D, 1)
        qn_ref[0, i] = (jnp.where(in_half[g], src, 0.0) * (NSA_D ** -0.5)).astype(BF16)
    kc_ref[...] = seg(_C_KC, LANES)
    vc_ref[...] = seg(_C_VC, LANES)
    h_ks, h_kw = seg(_C_KS, LANES), seg(_C_KW, LANES)
    ks0_ref[...] = (jnp.where(in_half[0], h_ks, 0.0) + ft0_ref[...]).astype(BF16)
    ks1_ref[...] = (jnp.where(in_half[1], h_ks, 0.0) + ft1_ref[...]).astype(BF16)
    kw0_ref[...] = (jnp.where(in_half[0], h_kw, 0.0) + ft0_ref[...]).astype(BF16)
    kw1_ref[...] = (jnp.where(in_half[1], h_kw, 0.0) + ft1_ref[...]).astype(BF16)
    def ones_rows(n):
        r = lax.broadcasted_iota(jnp.int32, (n, 1), 0) & (LANES - 1)
        return jnp.where(r == NSA_D, 1.0, 0.0)

    vst_ref[0] = (_nt_dot(wvst_ref[...], xb) + ones_rows(NSA_GROUPS * LANES)).astype(BF16)
    vwt_ref[0] = (_nt_dot(wvwt_ref[...], xb) + ones_rows(NSA_GROUPS * LANES)).astype(BF16)
    gate_ref[...] = jax.nn.sigmoid(seg(_C_GL, LANES))

    cqn = _rms_norm(seg(_C_CQ, MLA_Q_LORA), gq_ref[...]).astype(BF16)
    cq_t = cq_ref[...]
    sq_t = sq_ref[...]
    qa = _dot(cqn, wqa_ref[...])
    for hd in range(N_MLA_HEADS):
        hs = slice(hd * LANES, (hd + 1) * LANES)
        qm_ref[:, hs] = (qa[:, hs] * cq_t + swap_rotary(qa[:, hs]) * sq_t).astype(BF16)

    kvn = _rms_norm(seg(_C_CKV, MLA_KV_LORA), gkv_ref[...]).astype(BF16)
    kpa = seg(_C_KPA, LANES)
    kpe = kpa * ck_ref[...] + swap_rotary(kpa) * sk_ref[...]
    ka = _dot(kvn, wka_ref[...])
    for hd in range(N_MLA_HEADS):
        hs = slice(hd * LANES, (hd + 1) * LANES)
        km_ref[:, hs] = (ka[:, hs] + kpe).astype(BF16)
    vm_ref[0] = (_nt_dot(wv_ref[...], kvn) + ones_rows(N_MLA_HEADS * LANES)).astype(BF16)


def _in_proj(x2, w_all, wvst, wvwt, wqa, wka, wv, gq, gkv, ft0, ft1, cq_t, sq_t, ck_t, sk_t,
             B, T):
    N = B * T
    tm = TM_PROJ
    tpb = T // tm
    full = lambda a: pl.BlockSpec(a.shape, lambda i: (0,) * a.ndim)
    rows = lambda w: pl.BlockSpec((tm, w), lambda i: (i, 0))
    tab = pl.BlockSpec((tm, LANES), lambda i: (i % tpb, 0))
    cols_t = lambda w: pl.BlockSpec((1, w, tm), lambda i: (i // tpb, 0, i % tpb))
    tok = lambda dt: jax.ShapeDtypeStruct((N, LANES), dt)
    out_shape = (
        jax.ShapeDtypeStruct((B, N_NSA_HEADS, T, LANES), BF16),
        tok(F32), tok(F32),
        tok(BF16), tok(BF16),
        jax.ShapeDtypeStruct((B, NSA_GROUPS * LANES, T), BF16),
        tok(BF16), tok(BF16),
        jax.ShapeDtypeStruct((B, NSA_GROUPS * LANES, T), BF16),
        tok(F32),
        jax.ShapeDtypeStruct((N, N_MLA_HEADS * LANES), BF16),
        jax.ShapeDtypeStruct((N, N_MLA_HEADS * LANES), BF16),
        jax.ShapeDtypeStruct((B, N_MLA_HEADS * LANES, T), BF16),
    )
    out_specs = (
        pl.BlockSpec((1, N_NSA_HEADS, tm, LANES), lambda i: (i // tpb, 0, i % tpb, 0)),
        rows(LANES), rows(LANES), rows(LANES), rows(LANES), cols_t(NSA_GROUPS * LANES),
        rows(LANES), rows(LANES), cols_t(NSA_GROUPS * LANES), rows(LANES),
        rows(N_MLA_HEADS * LANES), rows(N_MLA_HEADS * LANES), cols_t(N_MLA_HEADS * LANES),
    )
    return pl.pallas_call(
        _in_proj_kernel,
        grid=(N // tm,),
        in_specs=[rows(D_MODEL), full(w_all), full(wvst), full(wvwt), full(wqa),
                  full(wka), full(wv), full(gq), full(gkv), tab, tab, tab, tab, tab, tab],
        out_specs=out_specs,
        out_shape=out_shape,
        compiler_params=pltpu.CompilerParams(dimension_semantics=("arbitrary",),
                                             vmem_limit_bytes=VMEM_LIMIT),
        name="in_proj",
    )(x2, w_all, wvst, wvwt, wqa, wka, wv, gq, gkv, ft0, ft1, cq_t, sq_t, ck_t, sk_t)


def _compress_kernel(kc_ref, vc_ref, pk_ref, pv_ref, wk_lo_ref, wk_hi_ref, wk2_ref,
                     wv_lo_ref, wv_hi_ref, wv2t_ref, ko_ref, vo_ref):
    nsub = kc_ref.shape[1]

    def hidden(src_ref, pos_ref, lo_ref, hi_ref):
        a = src_ref[0]
        pos = pos_ref[...]
        p0 = _dot((a + pos[0:1]).astype(BF16), lo_ref[...])
        p1 = _dot((a + pos[1:2]).astype(BF16), hi_ref[...])
        hid = p0 + pltpu.roll(p1, nsub - 1, 0)
        return jax.nn.gelu(hid).astype(BF16)

    k_out = _dot(hidden(kc_ref, pk_ref, wk_lo_ref, wk_hi_ref), wk2_ref[...])
    row = lax.broadcasted_iota(jnp.int32, k_out.shape, 0)
    ko_ref[0] = jnp.where(row < nsub - 1, k_out, 0.0).astype(BF16)
    v_out = _nt_dot(wv2t_ref[...], hidden(vc_ref, pv_ref, wv_lo_ref, wv_hi_ref))
    col = lax.broadcasted_iota(jnp.int32, v_out.shape, 1)
    vo_ref[0] = jnp.where(col < nsub - 1, v_out, 0.0).astype(BF16)


def _compress(kc3, vc3, pk, pv, wk_lo, wk_hi, wk2, wv_lo, wv_hi, wv2t):
    B, nsub, width = kc3.shape
    full = lambda a: pl.BlockSpec(a.shape, lambda b: (0,) * a.ndim)
    per_b = pl.BlockSpec((1, nsub, width), lambda b: (b, 0, 0))
    return pl.pallas_call(
        _compress_kernel,
        grid=(B,),
        in_specs=[per_b, per_b, full(pk), full(pv), full(wk_lo), full(wk_hi), full(wk2),
                  full(wv_lo), full(wv_hi), full(wv2t)],
        out_specs=(pl.BlockSpec((1, nsub, LANES), lambda b: (b, 0, 0)),
                   pl.BlockSpec((1, LANES, nsub), lambda b: (b, 0, 0))),
        out_shape=(jax.ShapeDtypeStruct((B, nsub, LANES), BF16),
                   jax.ShapeDtypeStruct((B, LANES, nsub), BF16)),
        compiler_params=pltpu.CompilerParams(dimension_semantics=("arbitrary",),
                                             vmem_limit_bytes=VMEM_LIMIT),
        name="nsa_compress",
    )(kc3, vc3, pk, pv, wk_lo, wk_hi, wk2, wv_lo, wv_hi, wv2t)


def _nsa_kernel(qn_ref, kcmp_ref, vcmpt_ref, cfeat_ref, ks0_ref, ks1_ref, vst_ref,
                kw0_ref, kw1_ref, vwt_ref, onehot_ref, band_ref, mt_ref, qf_ref, gate_ref,
                o_ref, qaug_ref, lst_ref, m_ref, acc_ref, *, n_top):
    qb = pl.program_id(1)
    q0 = qb * TQ_NSA
    R = R_NSA
    ncmp = kcmp_ref.shape[1]

    col = lax.broadcasted_iota(jnp.int32, (1, R), 1)
    t_row = q0 + (col & (TQ_NSA - 1))
    gates_t = gate_ref[...].T
    c_diag = qb // (TK_SLC // TQ_NSA)
    kl_col = lax.broadcasted_iota(jnp.int32, (TK_SLC, 1), 0)
    w0 = pl.multiple_of(q0, TQ_NSA)
    qb_f = qb.astype(F32)

    q_feats, o_wins, o_cmps, imps = [], [], [], []
    for g in range(NSA_GROUPS):
        kw_ref = (kw0_ref, kw1_ref)[g]
        q = qn_ref[0, g * NSA_HPG:(g + 1) * NSA_HPG].reshape(R, LANES)
        q_feat = (q.astype(F32) + qf_ref[g, 0] + qb_f * qf_ref[g, 1]).astype(BF16)
        q_feats.append(q_feat)

        v_rows = slice(g * LANES, g * LANES + V_ROWS)

        sw = _nt_dot(kw_ref[0, pl.ds(w0, WIN_KEYS), :], q_feat) + band_ref[...]
        pw = jnp.exp((sw - jnp.max(sw, axis=0, keepdims=True)).astype(BF16))
        o_win = _dot(vwt_ref[0, v_rows, pl.ds(w0, WIN_KEYS)], pw)
        o_wins.append(o_win[0:NSA_D, :] / o_win[NSA_D:NSA_D + 1, :])

        kc_aug = jnp.concatenate([kcmp_ref[0], cfeat_ref[...]], axis=1)
        q_cmp = jnp.concatenate([q, (qf_ref[g, 2] + qb_f * qf_ref[g, 3]).astype(BF16)], axis=1)
        cend = lax.broadcasted_iota(jnp.int32, (ncmp, 1), 0) * CMP_STRIDE + (CMP_LEN - 1)
        cmask = cend <= t_row
        sc = jnp.where(cmask, _nt_dot(kc_aug, q_cmp), NEG_INF)
        e = jnp.exp(sc - jnp.maximum(jnp.max(sc, axis=0, keepdims=True), 0.1 * NEG_INF))
        den = jnp.sum(e, axis=0, keepdims=True)
        p_cmp = e * jnp.where(den > 0.0, 1.0 / den, 0.0)
        o_cmps.append(_dot(vcmpt_ref[0, g * NSA_D:(g + 1) * NSA_D, :], p_cmp.astype(BF16)))

        p4 = p_cmp[:, 0:TQ_NSA]
        for h in range(1, NSA_HPG):
            p4 = p4 + p_cmp[:, h * TQ_NSA:(h + 1) * TQ_NSA]
        hi = p4.astype(BF16)
        r1 = p4 - hi.astype(F32)
        mid = r1.astype(BF16)
        lo = (r1 - mid.astype(F32)).astype(BF16)
        mt = mt_ref[...]
        imps.append(_dot(mt, hi) + _dot(mt, mid) + _dot(mt, lo))

    imp = jnp.concatenate(imps, axis=1)
    width = NSA_GROUPS * TQ_NSA
    jb = lax.broadcasted_iota(jnp.int32, (LANES, width), 0)
    tl2 = lax.broadcasted_iota(jnp.int32, (LANES, width), 1) & (TQ_NSA - 1)
    cur = (q0 + tl2) >> SLC_BLOCK_LOG2
    forced = (jb == 0) | (jb == cur) | (jb == cur - 1)
    future = jb > cur
    jf = jb.astype(F32)
    rest = jnp.where(forced, REMOVED_SCORE, jnp.where(future, -1.0, imp))
    picked = forced
    for _ in range(max(n_top - 3, 0)):
        mx = jnp.max(rest, axis=0, keepdims=True)
        first = jnp.min(jnp.where(rest == mx, jf, float(LANES)), axis=0, keepdims=True)
        hit = jf == first
        picked = picked | hit
        rest = jnp.where(hit, REMOVED_SCORE, rest)
    allowed2 = jnp.where(picked & jnp.logical_not(future), 1.0, 0.0)

    group_out = []
    for g in range(NSA_GROUPS):
        ks_ref = (ks0_ref, ks1_ref)[g]
        v_rows = slice(g * LANES, g * LANES + V_ROWS)
        o_cmp, o_win = o_cmps[g], o_wins[g]
        allowed = allowed2[:, g * TQ_NSA:(g + 1) * TQ_NSA]
        selneg_t = jnp.where(allowed > 0.0, 0.0, NEG_INF).T.astype(BF16)

        qaug_ref[:, 0:LANES] = q_feats[g]
        for h in range(NSA_HPG):
            qaug_ref[h * TQ_NSA:(h + 1) * TQ_NSA, LANES:2 * LANES] = selneg_t

        count = jnp.int32(0)
        for c in range(N_SLC_CHUNKS):
            used = jnp.max(allowed[c * BLOCKS_PER_CHUNK:(c + 1) * BLOCKS_PER_CHUNK, :]) > 0.0
            lst_ref[count] = jnp.int32(c)
            count = count + (used & (c < c_diag)).astype(jnp.int32)

        def slc_scores(c, causal, ks_ref=ks_ref):
            k0 = pl.multiple_of(c * TK_SLC, TK_SLC)
            k_aug = jnp.concatenate([ks_ref[0, pl.ds(k0, TK_SLC), :],
                                     onehot_ref[pl.ds(k0, TK_SLC), :]], axis=1)
            st = _nt_dot(k_aug, qaug_ref[...])
            if causal:
                st = jnp.where(k0 + kl_col <= t_row, st, NEG_INF)
            return st, jnp.max(st, axis=0, keepdims=True)

        def absorb(tile, c, v_rows=v_rows):
            st, mx = tile
            k0 = pl.multiple_of(c * TK_SLC, TK_SLC)
            m_old = m_ref[...]
            m_new = jnp.maximum(m_old, mx)
            p = jnp.exp((st - m_new).astype(BF16))
            pv = _dot(vst_ref[0, v_rows, pl.ds(k0, TK_SLC)], p)
            acc_ref[...] = jnp.exp(m_old - m_new) * acc_ref[...] + pv
            m_ref[...] = m_new

        def steps(chunks, causal=False, slc_scores=slc_scores, absorb=absorb):
            tiles = [slc_scores(c, causal) for c in chunks]
            for c, tile in zip(chunks, tiles):
                absorb(tile, c)

        m_ref[...] = jnp.full((1, R), NEG_INF, F32)
        acc_ref[...] = jnp.zeros((V_ROWS, R), F32)
        steps([c_diag], causal=True)

        def body(i, carry, steps=steps):
            steps([lst_ref[SLC_UNROLL * i + j] for j in range(SLC_UNROLL)])
            return carry

        n_loop = count // SLC_UNROLL
        lax.fori_loop(0, n_loop, body, 0)
        for j in range(SLC_UNROLL - 1):
            @pl.when(count - n_loop * SLC_UNROLL > j)
            def _(j=j, steps=steps):
                steps([lst_ref[n_loop * SLC_UNROLL + j]])
        o_slc = acc_ref[0:NSA_D, :] / acc_ref[NSA_D:NSA_D + 1, :]

        def gate_row(branch):
            rows = [gates_t[(g * NSA_HPG + h) * 3 + branch:(g * NSA_HPG + h) * 3 + branch + 1, :]
                    for h in range(NSA_HPG)]
            return jnp.concatenate(rows, axis=1)

        group_out.append(gate_row(0) * o_cmp + gate_row(1) * o_slc + gate_row(2) * o_win)

    out_t = jnp.concatenate(group_out, axis=0)
    for h in range(NSA_HPG):
        o_ref[:, h * LANES:(h + 1) * LANES] = out_t[:, h * TQ_NSA:(h + 1) * TQ_NSA].T.astype(BF16)


def _nsa(qn, kcmp, vcmpt, cfeat, ks0, ks1, vst, kw0, kw1, vwt, onehot, band, mt, qf, gates, B, T):
    nq = T // TQ_NSA
    n_top = min(SLC_TOP_N, T // SLC_BLOCK)
    per_b = lambda a: pl.BlockSpec((1,) + a.shape[1:], lambda b, i: (b,) + (0,) * (a.ndim - 1),
                                   pipeline_mode=pl.Buffered(1))
    full = lambda a: pl.BlockSpec(a.shape, lambda b, i: (0,) * a.ndim,
                                  pipeline_mode=pl.Buffered(1))
    return pl.pallas_call(
        functools.partial(_nsa_kernel, n_top=n_top),
        grid=(B, nq),
        in_specs=[pl.BlockSpec((1, N_NSA_HEADS, TQ_NSA, LANES), lambda b, i: (b, 0, i, 0)),
                  per_b(kcmp), per_b(vcmpt), full(cfeat), per_b(ks0), per_b(ks1), per_b(vst),
                  per_b(kw0), per_b(kw1), per_b(vwt), full(onehot), full(band), full(mt), full(qf),
                  pl.BlockSpec((TQ_NSA, LANES), lambda b, i: (b * nq + i, 0))],
        out_specs=pl.BlockSpec((TQ_NSA, NSA_HPG * LANES), lambda b, i: (b * nq + i, 0)),
        out_shape=jax.ShapeDtypeStruct((B * T, NSA_HPG * LANES), BF16),
        scratch_shapes=[pltpu.VMEM((R_NSA, 2 * LANES), BF16),
                        pltpu.SMEM((N_SLC_CHUNKS + 1,), jnp.int32),
                        pltpu.VMEM((1, R_NSA), F32), pltpu.VMEM((V_ROWS, R_NSA), F32)],
        compiler_params=pltpu.CompilerParams(dimension_semantics=("arbitrary", "arbitrary"),
                                             vmem_limit_bytes=VMEM_LIMIT),
        name="nsa_attention",
    )(qn, kcmp, vcmpt, cfeat, ks0, ks1, vst, kw0, kw1, vwt, onehot, band, mt, qf, gates)


def _mla_kernel(q_ref, k_ref, vt_ref, o_ref, m_ref, acc_ref):
    qi = pl.program_id(2)
    q0 = qi * TQ_MLA
    tpos = q0 + lax.broadcasted_iota(jnp.int32, (1, TQ_MLA), 1)
    kl = lax.broadcasted_iota(jnp.int32, (TK_MLA, 1), 0)
    qs = [q_ref[:, hh * LANES:(hh + 1) * LANES] for hh in range(2)]

    def qk(hh, c, causal):
        k0 = pl.multiple_of(c * TK_MLA, TK_MLA)
        st = _nt_dot(k_ref[0, pl.ds(k0, TK_MLA), hh * LANES:(hh + 1) * LANES], qs[hh])
        if causal:
            st = jnp.where(k0 + kl <= tpos, st, NEG_INF)
        return st, jnp.max(st, axis=0, keepdims=True)

    def absorb(hh, tile, c):
        st, mx = tile
        k0 = pl.multiple_of(c * TK_MLA, TK_MLA)
        m_old = m_ref[hh]
        m_new = jnp.maximum(m_old, mx)
        p = jnp.exp2((st - m_new).astype(BF16))
        pv = _dot(vt_ref[0, hh * LANES:(hh + 1) * LANES, pl.ds(k0, TK_MLA)], p)
        acc_ref[hh] = jnp.exp2(m_old - m_new) * acc_ref[hh] + pv
        m_ref[hh] = m_new

    def steps(chunks, causal=False):
        work = [(hh, c) for c in chunks for hh in range(2)]
        tiles = [qk(hh, c, causal) for hh, c in work[:MLA_DEPTH]]
        for i, (hh, c) in enumerate(work):
            absorb(hh, tiles[i], c)
            if i + MLA_DEPTH < len(work):
                nh, nc = work[i + MLA_DEPTH]
                tiles.append(qk(nh, nc, causal))

    m_ref[...] = jnp.full(m_ref.shape, NEG_INF, F32)
    acc_ref[...] = jnp.zeros(acc_ref.shape, F32)
    c_diag = qi // (TK_MLA // TQ_MLA)

    half = TK_MLA // 2
    k_diag = pl.multiple_of(c_diag * TK_MLA, TK_MLA)
    kl_h = lax.broadcasted_iota(jnp.int32, (half, 1), 0)
    diag_work = [(hh, part, cols) for hh in range(2)
                 for part, cols in ((0, slice(0, TQ_MLA)), (1, slice(half, TQ_MLA)))]
    diag_tiles = []
    for hh, part, cols in diag_work:
        k0 = pl.multiple_of(k_diag + part * half, half)
        st = _nt_dot(k_ref[0, pl.ds(k0, half), hh * LANES:(hh + 1) * LANES], qs[hh][cols, :])
        diag_tiles.append(jnp.where(k0 + kl_h <= tpos[:, cols], st, NEG_INF))
    for (hh, part, cols), st in zip(diag_work, diag_tiles):
        k0 = pl.multiple_of(k_diag + part * half, half)
        m_old = m_ref[hh, :, cols]
        m_new = jnp.maximum(m_old, jnp.max(st, axis=0, keepdims=True))
        p = jnp.exp2((st - m_new).astype(BF16))
        pv = _dot(vt_ref[0, hh * LANES:(hh + 1) * LANES, pl.ds(k0, half)], p)
        acc_ref[hh, :, cols] = jnp.exp2(m_old - m_new) * acc_ref[hh, :, cols] + pv
        m_ref[hh, :, cols] = m_new

    def body(i, carry):
        steps([MLA_UNROLL * i + j for j in range(MLA_UNROLL)])
        return carry

    n_loop = c_diag // MLA_UNROLL
    lax.fori_loop(0, n_loop, body, 0)
    for j in range(MLA_UNROLL - 1):
        @pl.when(c_diag - n_loop * MLA_UNROLL > j)
        def _(j=j):
            steps([n_loop * MLA_UNROLL + j])

    outs = [acc_ref[hh, 0:MLA_V, :] / acc_ref[hh, MLA_V:MLA_V + 1, :] for hh in range(2)]
    o_ref[...] = jnp.concatenate(outs, axis=0).T.astype(BF16)


def _mla(qm, km3, vmt, B, T):
    nq = T // TQ_MLA
    npair = N_MLA_HEADS // 2
    return pl.pallas_call(
        _mla_kernel,
        grid=(B, npair, nq),
        in_specs=[pl.BlockSpec((TQ_MLA, 2 * LANES), lambda b, h, i: (b * nq + i, h)),
                  pl.BlockSpec((1, T, 2 * LANES), lambda b, h, i: (b, 0, h)),
                  pl.BlockSpec((1, 2 * LANES, T), lambda b, h, i: (b, h, 0))],
        out_specs=pl.BlockSpec((TQ_MLA, LANES), lambda b, h, i: (b * nq + i, h)),
        out_shape=jax.ShapeDtypeStruct((B * T, npair * LANES), BF16),
        scratch_shapes=[pltpu.VMEM((2, 1, TQ_MLA), F32), pltpu.VMEM((2, LANES, TQ_MLA), F32)],
        compiler_params=pltpu.CompilerParams(
            dimension_semantics=("arbitrary", "arbitrary", "arbitrary"),
            vmem_limit_bytes=VMEM_LIMIT,
            ),
        name="mla_attention",
    )(qm, km3, vmt)


def _post_kernel(on_ref, om_ref, x_ref, p_ref, wn_ref, wm_ref, wup_ref, wd_ref,
                 wpg_ref, wp_ref, ln_ref, o_ref):
    mix = _dot(on_ref[...], wn_ref[...]) + _dot(om_ref[...], wm_ref[...])
    x1 = _layer_norm(ALPHA * x_ref[...] + mix, ln_ref[0:1, :], ln_ref[1:2, :])
    xb = x1.astype(BF16)
    ffn = jnp.zeros_like(x1)
    for c in range(D_FF // TF_FFN):
        cols = slice(c * TF_FFN, (c + 1) * TF_FFN)
        up_cols = slice(D_FF + c * TF_FFN, D_FF + (c + 1) * TF_FFN)
        hid = jax.nn.silu(_dot(xb, wup_ref[:, cols])) * _dot(xb, wup_ref[:, up_cols])
        ffn = ffn + _dot(hid.astype(BF16), wd_ref[cols, :])
    x2 = _layer_norm(ALPHA * x1 + ffn, ln_ref[2:3, :], ln_ref[3:4, :])
    gate = jax.nn.sigmoid(_dot(x2.astype(BF16), wpg_ref[...]))
    ple = gate * _dot(p_ref[...].astype(BF16), wp_ref[...])
    o_ref[...] = _layer_norm(ALPHA * x2 + ple, ln_ref[4:5, :], ln_ref[5:6, :])


def _post(o_nsa, o_mla, x2, p2, wn, wm, w_up, w_down, wpg, wp, ln):
    N = x2.shape[0]
    tm = TM_PROJ
    rows = lambda w: pl.BlockSpec((tm, w), lambda i: (i, 0))
    full = lambda a: pl.BlockSpec(a.shape, lambda i: (0,) * a.ndim)
    return pl.pallas_call(
        _post_kernel,
        grid=(N // tm,),
        in_specs=[rows(o_nsa.shape[1]), rows(o_mla.shape[1]), rows(D_MODEL), rows(D_PLE),
                  full(wn), full(wm), full(w_up), full(w_down), full(wpg), full(wp), full(ln)],
        out_specs=rows(D_MODEL),
        out_shape=jax.ShapeDtypeStruct((N, D_MODEL), F32),
        compiler_params=pltpu.CompilerParams(dimension_semantics=("arbitrary",),
                                             vmem_limit_bytes=VMEM_LIMIT),
        name="post_attention",
    )(o_nsa, o_mla, x2, p2, wn, wm, w_up, w_down, wpg, wp, ln)


def _arrange_in_weights(w_in):
    splits = (N_NSA_HEADS * NSA_D,) + (NSA_GROUPS * NSA_D,) * 6 + (3 * N_NSA_HEADS, MLA_Q_LORA,
                                                                  MLA_KV_LORA, MLA_ROPE)
    offs = np.cumsum((0,) + splits)
    part = lambda i: w_in[:, offs[i]:offs[i + 1]]

    def place(w, lane0):
        return jnp.pad(w, ((0, 0), (lane0, LANES - lane0 - w.shape[1])))

    cols = [part(0), part(1), part(2), part(3), part(5), place(part(7), 0), part(8), part(9),
            place(part(10), MLA_NOPE)]
    w_all = jnp.concatenate(cols, axis=1)
    assert w_all.shape[1] == _C_END

    def value_rows(w):
        return jnp.concatenate([place(w[:, g * NSA_D:(g + 1) * NSA_D], 0)
                                for g in range(NSA_GROUPS)], axis=1).T

    return w_all.astype(BF16), value_rows(part(4)).astype(BF16), value_rows(part(6)).astype(BF16)


def _arrange_mla_weights(w_uq, w_ukv):
    wq = w_uq.reshape(MLA_Q_LORA, N_MLA_HEADS, MLA_NOPE + MLA_ROPE)

    def place(w, lane0):
        return jnp.pad(w, ((0, 0), (0, 0), (lane0, LANES - lane0 - w.shape[2])))

    wqa = place(wq, 0)
    wkv = w_ukv.reshape(MLA_KV_LORA, N_MLA_HEADS, MLA_NOPE + MLA_V)
    wka = place(wkv[:, :, :MLA_NOPE], 0)
    wv = place(wkv[:, :, MLA_NOPE:], 0)
    flat = lambda a: a.reshape(a.shape[0], -1).astype(BF16)
    return flat(wqa), flat(wka), flat(wv).T


def _rope_tables(T):
    half = MLA_ROPE // 2
    pos = jnp.arange(T, dtype=F32)
    inv_freq = ROPE_THETA ** (-jnp.arange(half, dtype=F32) / half)
    ang = pos[:, None] * inv_freq[None, :]
    cos, sin = jnp.cos(ang), jnp.sin(ang)
    scale = (MLA_NOPE + MLA_ROPE) ** -0.5 * LOG2_E
    ones = jnp.ones((T, MLA_NOPE), F32)
    tail = jnp.zeros((T, LANES - MLA_NOPE - MLA_ROPE), F32)
    cos_row = lambda head: jnp.concatenate([head, cos, cos, tail], axis=1)
    sin_row = jnp.concatenate([0.0 * ones, -sin, sin, tail], axis=1)
    return cos_row(ones) * scale, sin_row * scale, cos_row(0.0 * ones), sin_row


def _compress_weights(w1, w2, pos):
    G, D = NSA_GROUPS, NSA_D
    w1r = w1.reshape(CMP_LEN, D, CMP_HIDDEN)
    eye = jnp.eye(G, dtype=bool)
    halves = []
    for j in range(CMP_LEN // CMP_STRIDE):
        part = w1r[j * CMP_STRIDE:(j + 1) * CMP_STRIDE]
        wide = jnp.where(eye[None, :, None, :, None], part[:, None, :, None, :], 0.0)
        halves.append(wide.reshape(CMP_STRIDE * G * D, G * CMP_HIDDEN).astype(BF16))
    w2bd = jnp.where(eye[:, None, :, None], w2[None, :, None, :], 0.0)
    w2bd = w2bd.reshape(G * CMP_HIDDEN, G * D).astype(BF16)
    posr = pos.reshape(CMP_LEN // CMP_STRIDE, CMP_STRIDE, 1, D)
    posw = jnp.broadcast_to(posr, (CMP_LEN // CMP_STRIDE, CMP_STRIDE, G, D)).reshape(
        CMP_LEN // CMP_STRIDE, CMP_STRIDE * G * D)
    return halves[0], halves[1], w2bd, posw


def _position_features(pos):
    return np.stack([(pos // LANES) * LANES, pos % LANES, np.ones_like(pos), np.ones_like(pos)],
                    axis=1).astype(np.float32)


def _nsa_tables(T):
    kpos = np.arange(T)
    onehot = np.zeros((T, LANES), np.float32)
    onehot[kpos, kpos // SLC_BLOCK] = 1.0
    key_feats, pad_rows = [], []
    for g in range(NSA_GROUPS):
        base = _feat_base(g)
        ft = np.zeros((T, LANES), np.float32)
        ft[:, base:base + 4] = _position_features(kpos)
        key_feats.append(jnp.asarray(ft))
        pad = np.zeros((WINDOW, LANES), np.float32)
        pad[:, base + 4] = 1.0
        pad_rows.append(jnp.asarray(pad, BF16))
    nsub = T // CMP_STRIDE
    c = np.arange(nsub)
    cfeat = np.zeros((nsub, LANES), np.float32)
    cfeat[:, 0:4] = _position_features(c * CMP_STRIDE + CMP_LEN - 1)
    kl = np.arange(WIN_KEYS)[:, None]
    tl = np.arange(TQ_NSA)[None, :]
    band = np.where((kl > tl) & (kl <= tl + WINDOW), 0.0, NEG_INF).astype(np.float32)
    band = np.tile(band, (1, NSA_HPG))
    ratio = SLC_BLOCK // CMP_STRIDE
    mt = np.zeros((LANES, nsub), np.float32)
    valid = c < nsub - 1
    for j in range(CMP_LEN // CMP_STRIDE):
        np.add.at(mt, ((c[valid] + j) // ratio, c[valid]), 1.0)
    qf = np.zeros((NSA_GROUPS, 4, R_NSA, LANES), np.float32)
    tl_rows = np.tile(np.arange(TQ_NSA), NSA_HPG).astype(np.float32)
    for g in range(NSA_GROUPS):
        slope = np.repeat(2.0 ** -(g * NSA_HPG + np.arange(NSA_HPG) + 1.0), TQ_NSA).astype(np.float32)
        for k, base in enumerate((_feat_base(g), 0)):
            qf[g, 2 * k, :, base] = slope
            qf[g, 2 * k, :, base + 1] = slope
            qf[g, 2 * k, :, base + 3] = -slope * tl_rows
            qf[g, 2 * k, :, base + 4] = NEG_INF
            qf[g, 2 * k + 1, :, base + 2] = -slope * TQ_NSA
    return (jnp.asarray(onehot, BF16), key_feats, pad_rows, jnp.asarray(cfeat, BF16),
            jnp.asarray(band), jnp.asarray(mt, BF16), jnp.asarray(qf))


def kernel(x, p, w_in, w_ck1, w_ck2, pos_ck, w_cv1, w_cv2, pos_cv, mla_q_norm, w_uq, mla_kv_norm,
           w_ukv, w_out, ln1_g, ln1_b, w_up, w_down, ln2_g, ln2_b, w_ple_gate, w_ple, ln3_g, ln3_b):
    B, T, _ = x.shape
    N = B * T
    assert T % TQ_MLA == 0 and T % TM_PROJ == 0 and T // SLC_BLOCK <= LANES
    cq_t, sq_t, ck_t, sk_t = _rope_tables(T)
    onehot, key_feats, pad_rows, cfeat, band, mt, qf = _nsa_tables(T)
    row2 = lambda v: v.reshape(1, -1)
    xc = x.reshape(N, D_MODEL)
    for i in range(DEPTH):
        w_all, wvst, wvwt = _arrange_in_weights(w_in[i])
        wqa, wka, wv = _arrange_mla_weights(w_uq[i], w_ukv[i])
        (qn, kc, vc, ks0, ks1, vst, kw0, kw1, vwt, gates, qm, km, vmt) = _in_proj(
            xc, w_all, wvst, wvwt, wqa, wka, wv, row2(mla_q_norm[i]), row2(mla_kv_norm[i]),
            key_feats[0], key_feats[1], cq_t, sq_t, ck_t, sk_t, B, T)
        wk_lo, wk_hi, wk2, pk = _compress_weights(w_ck1[i], w_ck2[i], pos_ck[i])
        wv_lo, wv_hi, wv2, pv = _compress_weights(w_cv1[i], w_cv2[i], pos_cv[i])
        nsub = T // CMP_STRIDE
        kcmp, vcmpt = _compress(kc.reshape(B, nsub, CMP_STRIDE * LANES),
                                vc.reshape(B, nsub, CMP_STRIDE * LANES),
                                pk, pv, wk_lo, wk_hi, wk2, wv_lo, wv_hi, wv2.T)
        pad_k = lambda a, g: jnp.concatenate(
            [jnp.broadcast_to(pad_rows[g], (B, WINDOW, LANES)), a.reshape(B, T, LANES)], axis=1)
        o_nsa = _nsa(qn, kcmp, vcmpt, cfeat, ks0.reshape(B, T, LANES), ks1.reshape(B, T, LANES), vst,
                     pad_k(kw0, 0), pad_k(kw1, 1), jnp.pad(vwt, ((0, 0), (0, 0), (WINDOW, 0))),
                     onehot, band, mt, qf, gates, B, T)
        o_mla = _mla(qm, km.reshape(B, T, -1), vmt, B, T)
        wo = w_out[i]
        nsa_w = N_NSA_HEADS * NSA_D
        wn = wo[:nsa_w].reshape(NSA_GROUPS, NSA_HPG, NSA_D, D_MODEL).transpose(1, 0, 2, 3)
        wn = wn.reshape(nsa_w, D_MODEL).astype(BF16)
        wm = wo[nsa_w:].astype(BF16)
        ln = jnp.stack([ln1_g[i], ln1_b[i], ln2_g[i], ln2_b[i], ln3_g[i], ln3_b[i]])
        xc = _post(o_nsa, o_mla, xc, p[i].reshape(N, D_PLE), wn, wm, w_up[i].astype(BF16),
                   w_down[i].astype(BF16), w_ple_gate[i].astype(BF16), w_ple[i].astype(BF16), ln)
    return xc.reshape(B, T, D_MODEL)
```
